```python
import math
import jax, jax.numpy as jnp
from jax import lax
import numpy as np

D_MODEL = 1024
BATCH = 8
SEQ = 2048
DEPTH = 2
DEC_BATCH = 128
DEC_SEQ = 4
PAST_LEN = 16384
PAGE_SIZE = 128

N_MIXERS = 2
N_SSM_LAYERS = (DEPTH + 1) // 2
N_GMLP_LAYERS = DEPTH // 2

SSM_WIDTH = D_MODEL
SSM_GROUP = 16
SSM_GROUPS = SSM_WIDTH // SSM_GROUP
SSM_STATE = 64
DT_MIN = 1e-3
DT_MAX = 1e-1

GMLP_HALF = 2 * D_MODEL
GMLP_HEADS = 8
GMLP_HEAD_DIM = GMLP_HALF // GMLP_HEADS
CHUNK = 128

PEER_HEADS = 8
N_KEYS = 128
N_EXPERTS = N_KEYS * N_KEYS
PEER_QUERY = 256
PEER_HALF = PEER_QUERY // 2
PEER_TOPK = 16
TOK_BLOCK = 256

EPS = 1e-6

kernel_name = "s5_gmlp_peer_hybrid_step"


def _rmsnorm(x, g):
    xf = x.astype(jnp.float32)
    y = xf * lax.rsqrt(jnp.mean(xf * xf, axis=-1, keepdims=True) + EPS)
    return (y * g.astype(jnp.float32)).astype(x.dtype)


def _layernorm(x, g, b):
    xf = x.astype(jnp.float32)
    mu = jnp.mean(xf, axis=-1, keepdims=True)
    xc = xf - mu
    y = xc * lax.rsqrt(jnp.mean(xc * xc, axis=-1, keepdims=True) + EPS)
    return (y * g.astype(jnp.float32) + b.astype(jnp.float32)).astype(x.dtype)


def _s5_discretize(a_re, a_im, log_dt, b_re, b_im):
    f32 = jnp.float32
    a_re = a_re.astype(f32)
    a_im = a_im.astype(f32)
    b_re = b_re.astype(f32)
    b_im = b_im.astype(f32)
    dt = jnp.exp(log_dt.astype(f32))[:, None]
    mag = jnp.exp(a_re * dt)
    abar_re = mag * jnp.cos(a_im * dt)
    abar_im = mag * jnp.sin(a_im * dt)
    num_re = abar_re - 1.0
    num_im = abar_im
    den = a_re * a_re + a_im * a_im
    f_re = (num_re * a_re + num_im * a_im) / den
    f_im = (num_im * a_re - num_re * a_im) / den
    bb_re = f_re[..., None] * b_re - f_im[..., None] * b_im
    bb_im = f_re[..., None] * b_im + f_im[..., None] * b_re
    return abar_re, abar_im, bb_re, bb_im


def _complex_affine_combine(e1, e2):
    a1r, a1i, b1r, b1i = e1
    a2r, a2i, b2r, b2i = e2
    ar = a1r * a2r - a1i * a2i
    ai = a1r * a2i + a1i * a2r
    br = a2r * b1r - a2i * b1i + b2r
    bi = a2r * b1i + a2i * b1r + b2i
    return (ar, ai, br, bi)


def _s5_mixer(x, h0_re, h0_im, w_in, a_re, a_im, log_dt, b_re, b_im, c_re, c_im, d_skip, w_glu, b_glu, w_out):
    f32 = jnp.float32
    bsz, t, _ = x.shape
    abar_re, abar_im, bb_re, bb_im = _s5_discretize(a_re, a_im, log_dt, b_re, b_im)
    u = (x @ w_in).astype(f32).reshape(bsz, t, SSM_GROUPS, SSM_GROUP)
    bu_re = jnp.einsum('bthc,hpc->tbhp', u, bb_re)
    bu_im = jnp.einsum('bthc,hpc->tbhp', u, bb_im)
    h0_re = h0_re.astype(f32)
    h0_im = h0_im.astype(f32)
    bu_re = bu_re.at[0].add(abar_re * h0_re - abar_im * h0_im)
    bu_im = bu_im.at[0].add(abar_re * h0_im + abar_im * h0_re)
    a_seq_re = jnp.broadcast_to(abar_re, (t, 1) + abar_re.shape)
    a_seq_im = jnp.broadcast_to(abar_im, (t, 1) + abar_im.shape)
    _, _, h_re, h_im = lax.associative_scan(
        _complex_affine_combine, (a_seq_re, a_seq_im, bu_re, bu_im), axis=0)
    y = (jnp.einsum('tbhp,hcp->bthc', h_re, c_re.astype(f32))
         - jnp.einsum('tbhp,hcp->bthc', h_im, c_im.astype(f32))
         + d_skip.astype(f32) * u)
    y = y.reshape(bsz, t, SSM_WIDTH).astype(x.dtype)
    g = jax.nn.gelu(y, approximate=False)
    out = g * jax.nn.sigmoid(g @ w_glu + b_glu)
    return out @ w_out, h_re[-1], h_im[-1]


def _chunk_mix(v, w_s, b_s):
    bsz, t, _ = v.shape
    n_chunks = -(-t // CHUNK)
    pad = n_chunks * CHUNK - t
    vp = jnp.pad(v, ((0, 0), (0, pad), (0, 0)))
    vc = vp.reshape(bsz, n_chunks, CHUNK, GMLP_HEADS, GMLP_HEAD_DIM)
    ws = w_s * jnp.tril(jnp.ones((CHUNK, CHUNK), w_s.dtype))
    out = jnp.einsum('hij,bcjhd->bcihd', ws, vc) + jnp.transpose(b_s)[None, None, :, :, None]
    return out.reshape(bsz, n_chunks * CHUNK, GMLP_HALF)[:, :t]


def _gmlp_mixer(x, w_in, b_in, ln_g, ln_b, w_s, b_s, w_out):
    z = jax.nn.gelu(x @ w_in + b_in, approximate=False)
    u, v = z[..., :GMLP_HALF], z[..., GMLP_HALF:]
    v = _layernorm(v, ln_g, ln_b)
    s = u * _chunk_mix(v, w_s, b_s)
    return s @ w_out, v


def _peer(x, w_q, sub_k1, sub_k2, u_emb, v_emb):
    bsz, t, d = x.shape
    n = bsz * t
    xt = x.reshape(n, d)
    q = (xt @ w_q).reshape(n, PEER_HEADS, PEER_QUERY)
    q1, q2 = q[..., :PEER_HALF], q[..., PEER_HALF:]
    s1 = jnp.einsum('nhk,mk->nhm', q1, sub_k1).astype(jnp.float32)
    s2 = jnp.einsum('nhk,mk->nhm', q2, sub_k2).astype(jnp.float32)
    v1, i1 = lax.top_k(s1, PEER_TOPK)
    v2, i2 = lax.top_k(s2, PEER_TOPK)
    cand = (v1[..., :, None] + v2[..., None, :]).reshape(n, PEER_HEADS, PEER_TOPK * PEER_TOPK)
    cidx = (i1[..., :, None] * N_KEYS + i2[..., None, :]).reshape(n, PEER_HEADS, PEER_TOPK * PEER_TOPK)
    best, pos = lax.top_k(cand, PEER_TOPK)
    idx = jnp.take_along_axis(cidx, pos, axis=-1)
    gate = jax.nn.softmax(best, axis=-1)
    idx = idx.reshape(n, PEER_HEADS * PEER_TOPK)
    gate = gate.reshape(n, PEER_HEADS * PEER_TOPK)
    n_blk = -(-n // TOK_BLOCK)
    pad = n_blk * TOK_BLOCK - n
    xb = jnp.pad(xt, ((0, pad), (0, 0))).reshape(n_blk, TOK_BLOCK, d)
    ib = jnp.pad(idx, ((0, pad), (0, 0))).reshape(n_blk, TOK_BLOCK, PEER_HEADS * PEER_TOPK)
    gb = jnp.pad(gate, ((0, pad), (0, 0))).reshape(n_blk, TOK_BLOCK, PEER_HEADS * PEER_TOPK)

    def block(args):
        xb_, ib_, gb_ = args
        ue = jnp.take(u_emb, ib_, axis=0)
        h = jnp.einsum('tkd,td->tk', ue, xb_).astype(jnp.float32)
        w = (gb_ * jax.nn.gelu(h, approximate=False)).astype(x.dtype)
        ve = jnp.take(v_emb, ib_, axis=0)
        return jnp.einsum('tk,tkd->td', w, ve)

    out = lax.map(block, (xb, ib, gb)).reshape(n_blk * TOK_BLOCK, d)[:n]
    return out.reshape(bsz, t, d)


def setup_inputs(seed: int = 0) -> dict:
    key = jax.random.key(seed)
    ks = jax.random.split(key, 32)
    f32 = jnp.float32
    nrm = lambda k, shape, scale: jax.random.normal(k, shape, f32) * scale
    x_prompt = nrm(ks[0], (BATCH, SEQ, D_MODEL), 1.0)
    x_sample = nrm(ks[1], (DEC_BATCH, DEC_SEQ, D_MODEL), 1.0)
    state_ssm_re = nrm(ks[2], (N_SSM_LAYERS, DEC_BATCH, SSM_GROUPS, SSM_STATE), 0.5)
    state_ssm_im = nrm(ks[3], (N_SSM_LAYERS, DEC_BATCH, SSM_GROUPS, SSM_STATE), 0.5)
    norm_mix_g = 1.0 + nrm(ks[4], (DEPTH, D_MODEL), 0.02)
    norm_ffn_g = 1.0 + nrm(ks[5], (DEPTH, D_MODEL), 0.02)
    norm_final_g = 1.0 + nrm(ks[6], (D_MODEL,), 0.02)
    ssm_w_in = nrm(ks[7], (N_SSM_LAYERS, D_MODEL, SSM_WIDTH), D_MODEL ** -0.5)
    ssm_a_re = -0.5 + nrm(ks[8], (N_SSM_LAYERS, SSM_GROUPS, SSM_STATE), 0.01)
    ssm_a_im = (math.pi * jnp.arange(SSM_STATE, dtype=f32))[None, None, :] + nrm(ks[9], (N_SSM_LAYERS, SSM_GROUPS, SSM_STATE), 0.01)
    ssm_log_dt = jax.random.uniform(ks[10], (N_SSM_LAYERS, SSM_GROUPS), f32, math.log(DT_MIN), math.log(DT_MAX))
    ssm_b_re = nrm(ks[11], (N_SSM_LAYERS, SSM_GROUPS, SSM_STATE, SSM_GROUP), (2 * SSM_GROUP) ** -0.5)
    ssm_b_im = nrm(ks[12], (N_SSM_LAYERS, SSM_GROUPS, SSM_STATE, SSM_GROUP), (2 * SSM_GROUP) ** -0.5)
    ssm_c_re = nrm(ks[13], (N_SSM_LAYERS, SSM_GROUPS, SSM_GROUP, SSM_STATE), SSM_STATE ** -0.5)
    ssm_c_im = nrm(ks[14], (N_SSM_LAYERS, SSM_GROUPS, SSM_GROUP, SSM_STATE), SSM_STATE ** -0.5)
    ssm_d = nrm(ks[15], (N_SSM_LAYERS, SSM_GROUPS, SSM_GROUP), 1.0)
    ssm_w_glu = nrm(ks[16], (N_SSM_LAYERS, SSM_WIDTH, SSM_WIDTH), SSM_WIDTH ** -0.5)
    ssm_b_glu = nrm(ks[17], (N_SSM_LAYERS, SSM_WIDTH), 0.01)
    ssm_w_out = nrm(ks[18], (N_SSM_LAYERS, SSM_WIDTH, D_MODEL), SSM_WIDTH ** -0.5)
    gmlp_w_in = nrm(ks[19], (N_GMLP_LAYERS, D_MODEL, 2 * GMLP_HALF), D_MODEL ** -0.5)
    gmlp_b_in = nrm(ks[20], (N_GMLP_LAYERS, 2 * GMLP_HALF), 0.01)
    gmlp_ln_g = 1.0 + nrm(ks[21], (N_GMLP_LAYERS, GMLP_HALF), 0.02)
    gmlp_ln_b = nrm(ks[22], (N_GMLP_LAYERS, GMLP_HALF), 0.01)
    gmlp_w_s = nrm(ks[23], (N_GMLP_LAYERS, GMLP_HEADS, CHUNK, CHUNK), 0.5 * CHUNK ** -0.5)
    gmlp_b_s = 1.0 + nrm(ks[24], (N_GMLP_LAYERS, GMLP_HEADS, CHUNK), 0.1)
    gmlp_w_out = nrm(ks[25], (N_GMLP_LAYERS, GMLP_HALF, D_MODEL), GMLP_HALF ** -0.5)
    peer_w_q = nrm(ks[26], (DEPTH, D_MODEL, PEER_HEADS * PEER_QUERY), D_MODEL ** -0.5)
    peer_k1 = nrm(ks[27], (DEPTH, N_KEYS, PEER_HALF), PEER_HALF ** -0.5)
    peer_k2 = nrm(ks[28], (DEPTH, N_KEYS, PEER_HALF), PEER_HALF ** -0.5)
    peer_u = nrm(ks[29], (DEPTH, N_EXPERTS, D_MODEL), D_MODEL ** -0.5)
    peer_v = nrm(ks[30], (DEPTH, N_EXPERTS, D_MODEL), 0.3)
    return {
        "x_prompt": x_prompt, "x_sample": x_sample,
        "state_ssm_re": state_ssm_re, "state_ssm_im": state_ssm_im,
        "norm_mix_g": norm_mix_g, "norm_ffn_g": norm_ffn_g, "norm_final_g": norm_final_g,
        "ssm_w_in": ssm_w_in, "ssm_a_re": ssm_a_re, "ssm_a_im": ssm_a_im, "ssm_log_dt": ssm_log_dt,
        "ssm_b_re": ssm_b_re, "ssm_b_im": ssm_b_im, "ssm_c_re": ssm_c_re, "ssm_c_im": ssm_c_im,
        "ssm_d": ssm_d, "ssm_w_glu": ssm_w_glu, "ssm_b_glu": ssm_b_glu, "ssm_w_out": ssm_w_out,
        "gmlp_w_in": gmlp_w_in, "gmlp_b_in": gmlp_b_in, "gmlp_ln_g": gmlp_ln_g, "gmlp_ln_b": gmlp_ln_b,
        "gmlp_w_s": gmlp_w_s, "gmlp_b_s": gmlp_b_s, "gmlp_w_out": gmlp_w_out,
        "peer_w_q": peer_w_q, "peer_k1": peer_k1, "peer_k2": peer_k2, "peer_u": peer_u, "peer_v": peer_v,
    }


def reference(x_prompt, x_sample, state_ssm_re, state_ssm_im,
              norm_mix_g, norm_ffn_g, norm_final_g,
              ssm_w_in, ssm_a_re, ssm_a_im, ssm_log_dt, ssm_b_re, ssm_b_im, ssm_c_re, ssm_c_im,
              ssm_d, ssm_w_glu, ssm_b_glu, ssm_w_out,
              gmlp_w_in, gmlp_b_in, gmlp_ln_g, gmlp_ln_b, gmlp_w_s, gmlp_b_s, gmlp_w_out,
              peer_w_q, peer_k1, peer_k2, peer_u, peer_v):

    def trunk(x, h0_re, h0_im, keep_chunk_rows):
        ssm_re, ssm_im, chunk_v = [], [], []
        for i in range(DEPTH):
            j = i // N_MIXERS
            hn = _rmsnorm(x, norm_mix_g[i])
            if i % N_MIXERS == 0:
                mix, hr, hi = _s5_mixer(hn, h0_re[j], h0_im[j], ssm_w_in[j], ssm_a_re[j], ssm_a_im[j],
                                        ssm_log_dt[j], ssm_b_re[j], ssm_b_im[j], ssm_c_re[j], ssm_c_im[j],
                                        ssm_d[j], ssm_w_glu[j], ssm_b_glu[j], ssm_w_out[j])
                ssm_re.append(hr)
                ssm_im.append(hi)
            else:
                mix, v = _gmlp_mixer(hn, gmlp_w_in[j], gmlp_b_in[j], gmlp_ln_g[j], gmlp_ln_b[j],
                                     gmlp_w_s[j], gmlp_b_s[j], gmlp_w_out[j])
                if keep_chunk_rows:
                    chunk_v.append(v)
            x = x + mix
            x = x + _peer(_rmsnorm(x, norm_ffn_g[i]), peer_w_q[i], peer_k1[i], peer_k2[i], peer_u[i], peer_v[i])
        y = _rmsnorm(x, norm_final_g)
        return y, jnp.stack(ssm_re), jnp.stack(ssm_im), chunk_v

    zeros = jnp.zeros((N_SSM_LAYERS, x_prompt.shape[0], SSM_GROUPS, SSM_STATE), jnp.float32)
    y_prompt, ssm_re_p, ssm_im_p, _ = trunk(x_prompt, zeros, zeros, False)
    y_sample, ssm_re_s, ssm_im_s, chunk_rows = trunk(x_sample, state_ssm_re, state_ssm_im, True)
    chunk_v_s = jnp.stack(chunk_rows)
    return (y_prompt, y_sample, ssm_re_p, ssm_im_p, ssm_re_s, ssm_im_s, chunk_v_s)
```

```python
import functools
import math

import jax
import jax.numpy as jnp
from jax import lax
from jax.experimental import pallas as pl
from jax.experimental.pallas import tpu as pltpu

F32 = jnp.float32
BF16 = jnp.bfloat16

EPS = 1e-6
D_MODEL = 1024
SSM_GROUP = 16
SSM_GROUPS = 64
SSM_STATE = 64
SSM_COLS = SSM_GROUPS * SSM_STATE
GMLP_HALF = 2 * D_MODEL
GMLP_HEADS = 8
GMLP_HEAD_DIM = GMLP_HALF // GMLP_HEADS
CHUNK = 128
PEER_HEADS = 8
N_KEYS = 128
N_EXPERTS = N_KEYS * N_KEYS
PEER_QUERY = 256
PEER_HALF = 128
PEER_TOPK = 16

LANES = 128
SUBLANES = 8
MXU_DIM = 256
VMEM_LIMIT = 56 * 1024 * 1024

NEG_INF = float("-inf")
POS_INF = float("inf")
SQRT_HALF = math.sqrt(0.5)


def _dot(a, b):
    return jnp.dot(a, b, preferred_element_type=F32)


def _dot_nt(a, b):
    return lax.dot_general(a, b, (((1,), (1,)), ((), ())), preferred_element_type=F32)


def _rms(x, g):
    ms = jnp.mean(x * x, axis=-1, keepdims=True)
    return x * lax.rsqrt(ms + EPS) * g


def _gelu(x):
    return 0.5 * x * (1.0 + lax.erf(x * SQRT_HALF))


def _params(sem):
    return pltpu.CompilerParams(dimension_semantics=sem, vmem_limit_bytes=VMEM_LIMIT)


def _norm_matmul_kernel(x_ref, g_ref, w_ref, o_ref):
    xn = _rms(x_ref[...], g_ref[...]).astype(BF16)
    o_ref[...] = _dot(xn, w_ref[...])


def _norm_matmul(x2d, g, w_bf, *, n_batch, tile):
    rows, d = x2d.shape
    n = w_bf.shape[1]
    t = rows // n_batch
    tiles = t // tile
    out = pl.pallas_call(
        _norm_matmul_kernel,
        grid=(n_batch, tiles),
        in_specs=[
            pl.BlockSpec((tile, d), lambda b, i: (b * tiles + i, 0)),
            pl.BlockSpec((1, d), lambda b, i: (0, 0)),
            pl.BlockSpec((d, n), lambda b, i: (0, 0)),
        ],
        out_specs=pl.BlockSpec((tile, n), lambda b, i: (i, b)),
        out_shape=jax.ShapeDtypeStruct((t, n_batch * n), F32),
        compiler_params=_params(("parallel", "parallel")),
        name="s5_in_proj",
    )(x2d, g, w_bf)
    return out.reshape(t * n_batch, n)


SCAN_COLS = 512
N_SCAN_BLOCKS = SSM_COLS // SCAN_COLS
N_DIAG_BLOCKS = D_MODEL // MXU_DIM
DIAG_COLS = SSM_COLS // N_DIAG_BLOCKS
SCAN_PER_DIAG = DIAG_COLS // SCAN_COLS


def _s5_scan_kernel(u_ref, h0r_ref, h0i_ref, ar_ref, ai_ref, bbr_ref, bbi_ref, ccr_ref, cci_ref, d_ref,
                    g_ref, hr_ref, hi_ref, bur, bui, st_r, st_i, *, steps, batch):
    c = pl.program_id(0)
    n_sub = batch // SUBLANES

    @pl.when(c == 0)
    def _():
        for cb in range(N_SCAN_BLOCKS):
            st_r[cb] = h0r_ref[:, cb * SCAN_COLS:(cb + 1) * SCAN_COLS]
            st_i[cb] = h0i_ref[:, cb * SCAN_COLS:(cb + 1) * SCAN_COLS]

    u = u_ref[...]
    ub = u.astype(BF16)
    for kb in range(N_DIAG_BLOCKS):
        lhs = ub[:, kb * MXU_DIM:(kb + 1) * MXU_DIM]
        pr = _dot(lhs, bbr_ref[kb])
        pi = _dot(lhs, bbi_ref[kb])
        for j in range(SCAN_PER_DIAG):
            bur[kb * SCAN_PER_DIAG + j] = pr[:, j * SCAN_COLS:(j + 1) * SCAN_COLS]
            bui[kb * SCAN_PER_DIAG + j] = pi[:, j * SCAN_COLS:(j + 1) * SCAN_COLS]

    def scan_block(idx, carry):
        cb = idx // n_sub
        s = idx % n_sub
        ar = jnp.broadcast_to(ar_ref[cb], (SUBLANES, SCAN_COLS))
        ai = jnp.broadcast_to(ai_ref[cb], (SUBLANES, SCAN_COLS))
        row0 = pl.multiple_of(s * SUBLANES, SUBLANES)
        h_r = st_r[cb, pl.ds(row0, SUBLANES), :]
        h_i = st_i[cb, pl.ds(row0, SUBLANES), :]

        def step(t, h):
            hr, hi = h
            r = pl.multiple_of(t * batch + row0, SUBLANES)
            nr = ar * hr - ai * hi + bur[cb, pl.ds(r, SUBLANES), :]
            ni = ar * hi + ai * hr + bui[cb, pl.ds(r, SUBLANES), :]
            bur[cb, pl.ds(r, SUBLANES), :] = nr
            bui[cb, pl.ds(r, SUBLANES), :] = ni
            return nr, ni

        h_r, h_i = lax.fori_loop(0, steps, step, (h_r, h_i), unroll=min(steps, 8))
        st_r[cb, pl.ds(row0, SUBLANES), :] = h_r
        st_i[cb, pl.ds(row0, SUBLANES), :] = h_i
        return carry

    lax.fori_loop(0, N_SCAN_BLOCKS * n_sub, scan_block, 0)

    for cb in range(N_SCAN_BLOCKS):
        hr_ref[:, cb * SCAN_COLS:(cb + 1) * SCAN_COLS] = st_r[cb]
        hi_ref[:, cb * SCAN_COLS:(cb + 1) * SCAN_COLS] = st_i[cb]

    for nb in range(N_DIAG_BLOCKS):
        acc = None
        for j in range(SCAN_PER_DIAG):
            cb = nb * SCAN_PER_DIAG + j
            part = (_dot(bur[cb].astype(BF16), ccr_ref[nb, j * SCAN_COLS:(j + 1) * SCAN_COLS, :])
                    + _dot(bui[cb].astype(BF16), cci_ref[nb, j * SCAN_COLS:(j + 1) * SCAN_COLS, :]))
            acc = part if acc is None else acc + part
        cols = slice(nb * MXU_DIM, (nb + 1) * MXU_DIM)
        y = acc + d_ref[:, cols] * u[:, cols]
        g_ref[:, cols] = _gelu(y)


def _s5_scan(u_tm, h0_re, h0_im, disc, *, batch, steps):
    rows, d = u_tm.shape
    t = rows // batch
    n_chunks = t // steps
    blk = steps * batch
    ar, ai, bbr, bbi, ccr, cci, dsk = disc
    const2 = lambda c: (0, 0)
    const3 = lambda c: (0, 0, 0)
    kern = functools.partial(_s5_scan_kernel, steps=steps, batch=batch)
    return pl.pallas_call(
        kern,
        grid=(n_chunks,),
        in_specs=[
            pl.BlockSpec((blk, d), lambda c: (c, 0)),
            pl.BlockSpec((batch, SSM_COLS), const2),
            pl.BlockSpec((batch, SSM_COLS), const2),
            pl.BlockSpec((N_SCAN_BLOCKS, 1, SCAN_COLS), const3),
            pl.BlockSpec((N_SCAN_BLOCKS, 1, SCAN_COLS), const3),
            pl.BlockSpec((N_DIAG_BLOCKS, MXU_DIM, DIAG_COLS), const3),
            pl.BlockSpec((N_DIAG_BLOCKS, MXU_DIM, DIAG_COLS), const3),
            pl.BlockSpec((N_DIAG_BLOCKS, DIAG_COLS, MXU_DIM), const3),
            pl.BlockSpec((N_DIAG_BLOCKS, DIAG_COLS, MXU_DIM), const3),
            pl.BlockSpec((1, d), const2),
        ],
        out_specs=[
            pl.BlockSpec((blk, d), lambda c: (c, 0)),
            pl.BlockSpec((batch, SSM_COLS), const2),
            pl.BlockSpec((batch, SSM_COLS), const2),
        ],
        out_shape=[
            jax.ShapeDtypeStruct((rows, d), F32),
            jax.ShapeDtypeStruct((batch, SSM_COLS), F32),
            jax.ShapeDtypeStruct((batch, SSM_COLS), F32),
        ],
        scratch_shapes=[
            pltpu.VMEM((N_SCAN_BLOCKS, blk, SCAN_COLS), F32),
            pltpu.VMEM((N_SCAN_BLOCKS, blk, SCAN_COLS), F32),
            pltpu.VMEM((N_SCAN_BLOCKS, batch, SCAN_COLS), F32),
            pltpu.VMEM((N_SCAN_BLOCKS, batch, SCAN_COLS), F32),
        ],
        compiler_params=_params(("arbitrary",)),
        name="s5_scan",
    )(u_tm, h0_re, h0_im, ar, ai, bbr, bbi, ccr, cci, dsk)


def _glu_out_kernel(g_ref, x_ref, wg_ref, bg_ref, wo_ref, o_ref):
    g = g_ref[...]
    z = _dot(g.astype(BF16), wg_ref[...]) + bg_ref[...]
    o = g * jax.nn.sigmoid(z)
    o_ref[...] = x_ref[...] + _dot(o.astype(BF16), wo_ref[...])


def _glu_out(g_tm, x2d, w_glu_bf, b_glu, w_out_bf, *, n_batch, tile):
    rows, d = x2d.shape
    t = rows // n_batch
    tiles = t // tile
    g2 = g_tm.reshape(t, n_batch * d)
    const2 = lambda b, i: (0, 0)
    return pl.pallas_call(
        _glu_out_kernel,
        grid=(n_batch, tiles),
        in_specs=[
            pl.BlockSpec((tile, d), lambda b, i: (i, b)),
            pl.BlockSpec((tile, d), lambda b, i: (b * tiles + i, 0)),
            pl.BlockSpec((d, d), const2),
            pl.BlockSpec((1, d), const2),
            pl.BlockSpec((d, d), const2),
        ],
        out_specs=pl.BlockSpec((tile, d), lambda b, i: (b * tiles + i, 0)),
        out_shape=jax.ShapeDtypeStruct((rows, d), F32),
        compiler_params=_params(("parallel", "parallel")),
        name="s5_glu_out",
    )(g2, x2d, w_glu_bf, b_glu, w_out_bf)


def _s5_discretize(a_re, a_im, log_dt, b_re, b_im, c_re, c_im, d_skip):
    dt = jnp.exp(log_dt)[:, None]
    mag = jnp.exp(a_re * dt)
    abar_re = mag * jnp.cos(a_im * dt)
    abar_im = mag * jnp.sin(a_im * dt)
    num_re = abar_re - 1.0
    num_im = abar_im
    den = a_re * a_re + a_im * a_im
    f_re = (num_re * a_re + num_im * a_im) / den
    f_im = (num_im * a_re - num_re * a_im) / den
    bb_re = f_re[..., None] * b_re - f_im[..., None] * b_im
    bb_im = f_re[..., None] * b_im + f_im[..., None] * b_re
    gpb = MXU_DIM // SSM_GROUP
    eye = jnp.eye(gpb, dtype=F32)

    def in_blocks(bb):
        bt = jnp.transpose(bb, (0, 2, 1)).reshape(N_DIAG_BLOCKS, gpb, SSM_GROUP, SSM_STATE)
        full = bt[:, :, :, None, :] * eye[None, :, None, :, None]
        return full.reshape(N_DIAG_BLOCKS, MXU_DIM, DIAG_COLS).astype(BF16)

    def out_blocks(cc):
        ct = jnp.transpose(cc, (0, 2, 1)).reshape(N_DIAG_BLOCKS, gpb, SSM_STATE, SSM_GROUP)
        full = ct[:, :, :, None, :] * eye[None, :, None, :, None]
        return full.reshape(N_DIAG_BLOCKS, DIAG_COLS, MXU_DIM).astype(BF16)

    ar = abar_re.reshape(N_SCAN_BLOCKS, 1, SCAN_COLS)
    ai = abar_im.reshape(N_SCAN_BLOCKS, 1, SCAN_COLS)
    return (ar, ai, in_blocks(bb_re), in_blocks(bb_im), out_blocks(c_re), out_blocks(-c_im),
            d_skip.reshape(1, D_MODEL))


def _gmlp_kernel(x_ref, g_ref, win_ref, bin_ref, lng_ref, lnb_ref, mix_ref, mixb_ref, wout_ref,
                 o_ref, v_ref, s_scr, *, rows):
    x = x_ref[...]
    hn = _rms(x, g_ref[...]).astype(BF16)
    z = _gelu(_dot(hn, win_ref[...]) + bin_ref[...])
    u = z[:, :GMLP_HALF]
    v = z[:, GMLP_HALF:]
    mu = jnp.mean(v, axis=-1, keepdims=True)
    vc = v - mu
    vn = vc * lax.rsqrt(jnp.mean(vc * vc, axis=-1, keepdims=True) + EPS) * lng_ref[...] + lnb_ref[...]
    v_ref[...] = vn
    vb = vn.astype(BF16)
    for r in range(rows // CHUNK):
        rs = slice(r * CHUNK, (r + 1) * CHUNK)
        for h in range(GMLP_HEADS):
            cs = slice(h * GMLP_HEAD_DIM, (h + 1) * GMLP_HEAD_DIM)
            mixed = _dot(mix_ref[h], vb[rs, cs]) + mixb_ref[h]
            s_scr[rs, cs] = (u[rs, cs] * mixed).astype(BF16)
    o_ref[...] = x + _dot(s_scr[...], wout_ref[...])


def _gmlp(x2d, g, w_in_bf, b_in, ln_g, ln_b, mix_bf, mix_bias, w_out_bf, *, tile):
    rows, d = x2d.shape
    const2 = lambda i: (0, 0)
    const3 = lambda i: (0, 0, 0)
    kern = functools.partial(_gmlp_kernel, rows=tile)
    return pl.pallas_call(
        kern,
        grid=(rows // tile,),
        in_specs=[
            pl.BlockSpec((tile, d), lambda i: (i, 0)),
            pl.BlockSpec((1, d), const2),
            pl.BlockSpec((d, 2 * GMLP_HALF), const2),
            pl.BlockSpec((1, 2 * GMLP_HALF), const2),
            pl.BlockSpec((1, GMLP_HALF), const2),
            pl.BlockSpec((1, GMLP_HALF), const2),
            pl.BlockSpec((GMLP_HEADS, CHUNK, CHUNK), const3),
            pl.BlockSpec((GMLP_HEADS, CHUNK, GMLP_HEAD_DIM), const3),
            pl.BlockSpec((GMLP_HALF, d), const2),
        ],
        out_specs=[
            pl.BlockSpec((tile, d), lambda i: (i, 0)),
            pl.BlockSpec((tile, GMLP_HALF), lambda i: (i, 0)),
        ],
        out_shape=[
            jax.ShapeDtypeStruct((rows, d), F32),
            jax.ShapeDtypeStruct((rows, GMLP_HALF), F32),
        ],
        scratch_shapes=[pltpu.VMEM((tile, GMLP_HALF), BF16)],
        compiler_params=_params(("parallel",)),
        name="gmlp",
    )(x2d, g, w_in_bf, b_in, ln_g, ln_b, mix_bf, mix_bias, w_out_bf)


def _top16_desc(a):
    rows = []
    arr = jnp.full((PEER_TOPK, a.shape[1]), NEG_INF, F32)
    rid = lax.broadcasted_iota(jnp.int32, (PEER_TOPK, a.shape[1]), 0)
    m = None
    for r in range(PEER_TOPK):
        cur = a if m is None else jnp.where(a < m, a, NEG_INF)
        m = jnp.max(cur, axis=0, keepdims=True)
        rows.append(m)
        arr = jnp.where(rid == r, m, arr)
    return rows, arr


def _peer_thresholds(s1, s2):
    v1, _ = _top16_desc(s1)
    v2, v2arr = _top16_desc(s2)
    v2lo = v2arr[0:SUBLANES]
    v2hi = v2arr[SUBLANES:PEER_TOPK]
    rid = lax.broadcasted_iota(jnp.int32, v2lo.shape, 0)
    cands = [(0, v2lo, v1[0] + v2lo), (0, v2hi, v1[0] + v2hi)]
    for a in range(1, PEER_TOPK):
        nb = PEER_TOPK // (a + 1)
        cands.append((a, v2lo, jnp.where(rid < nb, v1[a] + v2lo, NEG_INF)))
    m = None
    for r in range(PEER_TOPK):
        cur = None
        for _, _, cv in cands:
            x = cv if m is None else jnp.where(cv < m, cv, NEG_INF)
            cur = x if cur is None else jnp.maximum(cur, x)
        m = jnp.max(cur, axis=0, keepdims=True)
    tau = m
    cmax = v1[0] + v2[0]
    z = None
    th_rank = [None] * PEER_TOPK
    for a, v2part, cv in cands:
        sel = cv >= tau
        ez = jnp.sum(jnp.where(sel, jnp.exp(cv - cmax), 0.0), axis=0, keepdims=True)
        z = ez if z is None else z + ez
        tmin = jnp.min(jnp.where(sel, v2part, POS_INF), axis=0, keepdims=True)
        th_rank[a] = tmin if th_rank[a] is None else jnp.minimum(th_rank[a], tmin)
    th = jnp.full(s1.shape, POS_INF, F32)
    for a in range(PEER_TOPK):
        th = jnp.where(s1 == v1[a], th_rank[a], th)
    c1 = jnp.exp(s1 - v1[0]) / z
    e2 = jnp.exp(s2 - v2[0])
    return th, c1, e2


def _peer_kernel(x_ref, g_ref, gf_ref, wq_ref, k1_ref, k2_ref, u_ref, vt_ref, o_ref,
                 xn_scr, s1_scr, s2_scr, th_scr, c1_scr, e2_scr, h_scr, w_scr, acc_scr,
                 *, tm, eb, final_norm):
    e = pl.program_id(1)
    n_e = pl.num_programs(1)
    n_tc = tm // LANES
    keys_per_block = eb // N_KEYS

    @pl.when(e == 0)
    def _():
        xn = _rms(x_ref[...], g_ref[...]).astype(BF16)
        xn_scr[...] = xn
        for h in range(PEER_HEADS):
            qt = _dot_nt(wq_ref[h * PEER_QUERY:(h + 1) * PEER_QUERY, :], xn)
            s1 = _dot(k1_ref[...], qt[:PEER_HALF].astype(BF16))
            s2 = _dot(k2_ref[...], qt[PEER_HALF:].astype(BF16))
            for tc in range(n_tc):
                s1_scr[h, tc] = s1[:, tc * LANES:(tc + 1) * LANES]
                s2_scr[h, tc] = s2[:, tc * LANES:(tc + 1) * LANES]

        def thresholds(idx, carry):
            h = idx // n_tc
            tc = idx % n_tc
            th, c1, e2 = _peer_thresholds(s1_scr[h, tc], s2_scr[h, tc])
            th_scr[h, tc] = th
            c1_scr[h, tc] = c1
            e2_scr[h, tc] = e2
            return carry

        lax.fori_loop(0, PEER_HEADS * n_tc, thresholds, 0)
        acc_scr[...] = jnp.zeros_like(acc_scr)

    h_scr[...] = _dot_nt(u_ref[...], xn_scr[...])

    def key_row(kl, carry):
        i1 = e * keys_per_block + kl
        r0 = pl.multiple_of(kl * N_KEYS, N_KEYS)
        for tc in range(n_tc):
            gate = jnp.zeros((N_KEYS, LANES), F32)
            for h in range(PEER_HEADS):
                th = th_scr[h, tc, pl.ds(i1, 1), :]
                c1 = c1_scr[h, tc, pl.ds(i1, 1), :]
                gate = gate + jnp.where(s2_scr[h, tc] >= th, c1 * e2_scr[h, tc], 0.0)
            hv = h_scr[pl.ds(r0, N_KEYS), tc * LANES:(tc + 1) * LANES]
            w_scr[pl.ds(r0, N_KEYS), tc * LANES:(tc + 1) * LANES] = (gate * _gelu(hv)).astype(BF16)
        return carry

    lax.fori_loop(0, keys_per_block, key_row, 0)
    acc_scr[...] += _dot(vt_ref[...], w_scr[...])

    @pl.when(e == n_e - 1)
    def _():
        y = x_ref[...] + acc_scr[...].T
        if final_norm:
            y = _rms(y, gf_ref[...])
        o_ref[...] = y


def _peer(x2d, g, g_final, wq_t_bf, k1_bf, k2_bf, u_bf, v_t_bf, *, final_norm, tm=512, eb=1024):
    n, d = x2d.shape
    n_tc = tm // LANES
    const2 = lambda i, e: (0, 0)
    kern = functools.partial(_peer_kernel, tm=tm, eb=eb, final_norm=final_norm)
    tile4 = (PEER_HEADS, n_tc, N_KEYS, LANES)
    return pl.pallas_call(
        kern,
        grid=(n // tm, N_EXPERTS // eb),
        in_specs=[
            pl.BlockSpec((tm, d), lambda i, e: (i, 0)),
            pl.BlockSpec((1, d), const2),
            pl.BlockSpec((1, d), const2),
            pl.BlockSpec((PEER_HEADS * PEER_QUERY, d), const2),
            pl.BlockSpec((N_KEYS, PEER_HALF), const2),
            pl.BlockSpec((N_KEYS, PEER_HALF), const2),
            pl.BlockSpec((eb, d), lambda i, e: (e, 0)),
            pl.BlockSpec((d, eb), lambda i, e: (0, e)),
        ],
        out_specs=pl.BlockSpec((tm, d), lambda i, e: (i, 0)),
        out_shape=jax.ShapeDtypeStruct((n, d), F32),
        scratch_shapes=[
            pltpu.VMEM((tm, d), BF16),
            pltpu.VMEM(tile4, F32),
            pltpu.VMEM(tile4, F32),
            pltpu.VMEM(tile4, F32),
            pltpu.VMEM(tile4, F32),
            pltpu.VMEM(tile4, F32),
            pltpu.VMEM((eb, tm), F32),
            pltpu.VMEM((eb, tm), BF16),
            pltpu.VMEM((d, tm), F32),
        ],
        compiler_params=_params(("parallel", "arbitrary")),
        name="peer",
    )(x2d, g, g_final, wq_t_bf, k1_bf, k2_bf, u_bf, v_t_bf)


def _trunk(x, h0_re, h0_im, w, *, scan_steps):
    bsz, t, d = x.shape
    n = bsz * t
    x2d = x.reshape(n, d)
    row_tile = min(512, t)

    if t >= row_tile and t % row_tile == 0 and bsz <= 8:
        u_tm = _norm_matmul(x2d, w["norm_mix_g"][0], w["ssm_w_in"], n_batch=bsz, tile=row_tile)
        g_tm, hr, hi = _s5_scan(u_tm, h0_re, h0_im, w["disc"], batch=bsz, steps=scan_steps)
        x2d = _glu_out(g_tm, x2d, w["ssm_w_glu"], w["ssm_b_glu"], w["ssm_w_out"], n_batch=bsz, tile=row_tile)
    else:
        u_bm = _norm_matmul(x2d, w["norm_mix_g"][0], w["ssm_w_in"], n_batch=1, tile=n)
        u_tm = u_bm.reshape(bsz, t, d).transpose(1, 0, 2).reshape(n, d)
        g_tm, hr, hi = _s5_scan(u_tm, h0_re, h0_im, w["disc"], batch=bsz, steps=scan_steps)
        g_bm = g_tm.reshape(t, bsz, d).transpose(1, 0, 2).reshape(n, d)
        x2d = _glu_out(g_bm, x2d, w["ssm_w_glu"], w["ssm_b_glu"], w["ssm_w_out"], n_batch=1, tile=n)
    x2d = _peer(x2d, w["norm_ffn_g"][0], w["norm_final_g"], *w["peer"][0], final_norm=False)

    if t % CHUNK == 0:
        mix, mix_bias = w["mix_full"]
    else:
        mix, mix_bias = w["mix_short"]
    x2d, v = _gmlp(x2d, w["norm_mix_g"][1], w["gmlp_w_in"], w["gmlp_b_in"], w["gmlp_ln_g"], w["gmlp_ln_b"],
                   mix, mix_bias, w["gmlp_w_out"], tile=256)
    y2d = _peer(x2d, w["norm_ffn_g"][1], w["norm_final_g"], *w["peer"][1], final_norm=True)
    return y2d.reshape(bsz, t, d), hr, hi, v


def kernel(x_prompt, x_sample, state_ssm_re, state_ssm_im, norm_mix_g, norm_ffn_g, norm_final_g, ssm_w_in, ssm_a_re, ssm_a_im, ssm_log_dt, ssm_b_re, ssm_b_im, ssm_c_re, ssm_c_im, ssm_d, ssm_w_glu, ssm_b_glu, ssm_w_out, gmlp_w_in, gmlp_b_in, gmlp_ln_g, gmlp_ln_b, gmlp_w_s, gmlp_b_s, gmlp_w_out, peer_w_q, peer_k1, peer_k2, peer_u, peer_v):
    bp, tp, d = x_prompt.shape
    bs, ts, _ = x_sample.shape
    assert tp % CHUNK == 0 and CHUNK % ts == 0 and d == D_MODEL

    tril = jnp.tril(jnp.ones((CHUNK, CHUNK), F32))
    ws = gmlp_w_s[0] * tril
    bias_full = jnp.broadcast_to(gmlp_b_s[0][:, :, None], (GMLP_HEADS, CHUNK, GMLP_HEAD_DIM))
    reps = CHUNK // ts
    ws_short = jnp.einsum("ab,hij->haibj", jnp.eye(reps, dtype=F32), ws[:, :ts, :ts]).reshape(GMLP_HEADS, CHUNK, CHUNK)
    bias_short = jnp.broadcast_to(jnp.tile(gmlp_b_s[0][:, :ts], (1, reps))[:, :, None],
                                  (GMLP_HEADS, CHUNK, GMLP_HEAD_DIM))

    w = {
        "norm_mix_g": norm_mix_g[:, None, :],
        "norm_ffn_g": norm_ffn_g[:, None, :],
        "norm_final_g": norm_final_g[None, :],
        "ssm_w_in": ssm_w_in[0].astype(BF16),
        "disc": _s5_discretize(ssm_a_re[0], ssm_a_im[0], ssm_log_dt[0], ssm_b_re[0], ssm_b_im[0],
                               ssm_c_re[0], ssm_c_im[0], ssm_d[0]),
        "ssm_w_glu": ssm_w_glu[0].astype(BF16),
        "ssm_b_glu": ssm_b_glu[0][None, :],
        "ssm_w_out": ssm_w_out[0].astype(BF16),
        "gmlp_w_in": gmlp_w_in[0].astype(BF16),
        "gmlp_b_in": gmlp_b_in[0][None, :],
        "gmlp_ln_g": gmlp_ln_g[0][None, :],
        "gmlp_ln_b": gmlp_ln_b[0][None, :],
        "mix_full": (ws.astype(BF16), bias_full),
        "mix_short": (ws_short.astype(BF16), bias_short),
        "gmlp_w_out": gmlp_w_out[0].astype(BF16),
        "peer": [
            (peer_w_q[i].T.astype(BF16), peer_k1[i].astype(BF16), peer_k2[i].astype(BF16),
             peer_u[i].astype(BF16), peer_v[i].T.astype(BF16))
            for i in range(2)
        ],
    }

    zeros = jnp.zeros((bp, SSM_COLS), F32)
    y_p, hr_p, hi_p, _ = _trunk(x_prompt, zeros, zeros, w, scan_steps=32)
    y_s, hr_s, hi_s, v_s = _trunk(x_sample, state_ssm_re[0].reshape(bs, SSM_COLS),
                                  state_ssm_im[0].reshape(bs, SSM_COLS), w, scan_steps=ts)
    st = lambda a, b: a.reshape(1, b, SSM_GROUPS, SSM_STATE)
    return (y_p, y_s, st(hr_p, bp), st(hi_p, bp), st(hr_s, bs), st(hi_s, bs),
            v_s.reshape(1, bs, ts, GMLP_HALF))
```

```python
import functools
import math

import jax
import jax.numpy as jnp
from jax import lax
from jax.experimental import pallas as pl
from jax.experimental.pallas import tpu as pltpu

F32 = jnp.float32
BF16 = jnp.bfloat16

EPS = 1e-6
D_MODEL = 1024
SSM_GROUP = 16
SSM_GROUPS = 64
SSM_STATE = 64
SSM_COLS = SSM_GROUPS * SSM_STATE
GMLP_HALF = 2 * D_MODEL
GMLP_HEADS = 8
GMLP_HEAD_DIM = GMLP_HALF // GMLP_HEADS
CHUNK = 128
PEER_HEADS = 8
N_KEYS = 128
N_EXPERTS = N_KEYS * N_KEYS
PEER_QUERY = 256
PEER_HALF = 128
PEER_TOPK = 16

LANES = 128
SUBLANES = 8
MXU_DIM = 256
VMEM_LIMIT = 56 * 1024 * 1024

NEG_INF = float("-inf")
POS_INF = float("inf")
SQRT_HALF = math.sqrt(0.5)


def _dot(a, b):
    return jnp.dot(a, b, preferred_element_type=F32)


def _rms(x, g):
    ms = jnp.mean(x * x, axis=-1, keepdims=True)
    return x * lax.rsqrt(ms + EPS) * g


def _gelu(x):
    return 0.5 * x * (1.0 + lax.erf(x * SQRT_HALF))


def _params(sem, flags=None):
    return pltpu.CompilerParams(dimension_semantics=sem, vmem_limit_bytes=VMEM_LIMIT, flags=flags)


def _norm_matmul_kernel(x_ref, g_ref, w_ref, o_ref):
    xn = _rms(x_ref[...], g_ref[...]).astype(BF16)
    o_ref[...] = _dot(xn, w_ref[...])


def _norm_matmul(x2d, g, w_bf, *, n_batch, tile):
    rows, d = x2d.shape
    n = w_bf.shape[1]
    t = rows // n_batch
    tiles = t // tile
    out = pl.pallas_call(
        _norm_matmul_kernel,
        grid=(n_batch, tiles),
        in_specs=[
            pl.BlockSpec((tile, d), lambda b, i: (b * tiles + i, 0)),
            pl.BlockSpec((1, d), lambda b, i: (0, 0)),
            pl.BlockSpec((d, n), lambda b, i: (0, 0)),
        ],
        out_specs=pl.BlockSpec((tile, n), lambda b, i: (i, b)),
        out_shape=jax.ShapeDtypeStruct((t, n_batch * n), F32),
        compiler_params=_params(("parallel", "parallel")),
        name="s5_in_proj",
    )(x2d, g, w_bf)
    return out.reshape(t * n_batch, n)


SCAN_COLS = 512
N_SCAN_BLOCKS = SSM_COLS // SCAN_COLS
N_DIAG_BLOCKS = D_MODEL // MXU_DIM
DIAG_COLS = SSM_COLS // N_DIAG_BLOCKS
SCAN_PER_DIAG = DIAG_COLS // SCAN_COLS


def _s5_scan_kernel(u_ref, h0r_ref, h0i_ref, ar_ref, ai_ref, bbr_ref, bbi_ref, ccr_ref, cci_ref, d_ref,
                    g_ref, hr_ref, hi_ref, bur, bui, st_r, st_i, *, steps, batch):
    c = pl.program_id(0)
    n_sub = batch // SUBLANES

    @pl.when(c == 0)
    def _():
        for cb in range(N_SCAN_BLOCKS):
            st_r[cb] = h0r_ref[:, cb * SCAN_COLS:(cb + 1) * SCAN_COLS]
            st_i[cb] = h0i_ref[:, cb * SCAN_COLS:(cb + 1) * SCAN_COLS]

    u = u_ref[...]
    ub = u.astype(BF16)
    for kb in range(N_DIAG_BLOCKS):
        lhs = ub[:, kb * MXU_DIM:(kb + 1) * MXU_DIM]
        pr = _dot(lhs, bbr_ref[kb])
        pi = _dot(lhs, bbi_ref[kb])
        for j in range(SCAN_PER_DIAG):
            bur[kb * SCAN_PER_DIAG + j] = pr[:, j * SCAN_COLS:(j + 1) * SCAN_COLS]
            bui[kb * SCAN_PER_DIAG + j] = pi[:, j * SCAN_COLS:(j + 1) * SCAN_COLS]

    def scan_block(idx, carry):
        cb = idx // n_sub
        s = idx % n_sub
        ar = jnp.broadcast_to(ar_ref[cb], (SUBLANES, SCAN_COLS))
        ai = jnp.broadcast_to(ai_ref[cb], (SUBLANES, SCAN_COLS))
        row0 = pl.multiple_of(s * SUBLANES, SUBLANES)
        h_r = st_r[cb, pl.ds(row0, SUBLANES), :]
        h_i = st_i[cb, pl.ds(row0, SUBLANES), :]

        def step(t, h):
            hr, hi = h
            r = pl.multiple_of(t * batch + row0, SUBLANES)
            nr = ar * hr - ai * hi + bur[cb, pl.ds(r, SUBLANES), :]
            ni = ar * hi + ai * hr + bui[cb, pl.ds(r, SUBLANES), :]
            bur[cb, pl.ds(r, SUBLANES), :] = nr
            bui[cb, pl.ds(r, SUBLANES), :] = ni
            return nr, ni

        h_r, h_i = lax.fori_loop(0, steps, step, (h_r, h_i), unroll=min(steps, 8))
        st_r[cb, pl.ds(row0, SUBLANES), :] = h_r
        st_i[cb, pl.ds(row0, SUBLANES), :] = h_i
        return carry

    lax.fori_loop(0, N_SCAN_BLOCKS * n_sub, scan_block, 0)

    for cb in range(N_SCAN_BLOCKS):
        hr_ref[:, cb * SCAN_COLS:(cb + 1) * SCAN_COLS] = st_r[cb]
        hi_ref[:, cb * SCAN_COLS:(cb + 1) * SCAN_COLS] = st_i[cb]

    for nb in range(N_DIAG_BLOCKS):
        acc = None
        for j in range(SCAN_PER_DIAG):
            cb = nb * SCAN_PER_DIAG + j
            part = (_dot(bur[cb].astype(BF16), ccr_ref[nb, j * SCAN_COLS:(j + 1) * SCAN_COLS, :])
                    + _dot(bui[cb].astype(BF16), cci_ref[nb, j * SCAN_COLS:(j + 1) * SCAN_COLS, :]))
            acc = part if acc is None else acc + part
        cols = slice(nb * MXU_DIM, (nb + 1) * MXU_DIM)
        y = acc + d_ref[:, cols] * u[:, cols]
        g_ref[:, cols] = _gelu(y)


def _s5_scan(u_tm, h0_re, h0_im, disc, *, batch, steps):
    rows, d = u_tm.shape
    t = rows // batch
    n_chunks = t // steps
    blk = steps * batch
    ar, ai, bbr, bbi, ccr, cci, dsk = disc
    const2 = lambda c: (0, 0)
    const3 = lambda c: (0, 0, 0)
    kern = functools.partial(_s5_scan_kernel, steps=steps, batch=batch)
    return pl.pallas_call(
        kern,
        grid=(n_chunks,),
        in_specs=[
            pl.BlockSpec((blk, d), lambda c: (c, 0)),
            pl.BlockSpec((batch, SSM_COLS), const2),
            pl.BlockSpec((batch, SSM_COLS), const2),
            pl.BlockSpec((N_SCAN_BLOCKS, 1, SCAN_COLS), const3),
            pl.BlockSpec((N_SCAN_BLOCKS, 1, SCAN_COLS), const3),
            pl.BlockSpec((N_DIAG_BLOCKS, MXU_DIM, DIAG_COLS), const3),
            pl.BlockSpec((N_DIAG_BLOCKS, MXU_DIM, DIAG_COLS), const3),
            pl.BlockSpec((N_DIAG_BLOCKS, DIAG_COLS, MXU_DIM), const3),
            pl.BlockSpec((N_DIAG_BLOCKS, DIAG_COLS, MXU_DIM), const3),
            pl.BlockSpec((1, d), const2),
        ],
        out_specs=[
            pl.BlockSpec((blk, d), lambda c: (c, 0)),
            pl.BlockSpec((batch, SSM_COLS), const2),
            pl.BlockSpec((batch, SSM_COLS), const2),
        ],
        out_shape=[
            jax.ShapeDtypeStruct((rows, d), F32),
            jax.ShapeDtypeStruct((batch, SSM_COLS), F32),
            jax.ShapeDtypeStruct((batch, SSM_COLS), F32),
        ],
        scratch_shapes=[
            pltpu.VMEM((N_SCAN_BLOCKS, blk, SCAN_COLS), F32),
            pltpu.VMEM((N_SCAN_BLOCKS, blk, SCAN_COLS), F32),
            pltpu.VMEM((N_SCAN_BLOCKS, batch, SCAN_COLS), F32),
            pltpu.VMEM((N_SCAN_BLOCKS, batch, SCAN_COLS), F32),
        ],
        compiler_params=_params(("arbitrary",)),
        name="s5_scan",
    )(u_tm, h0_re, h0_im, ar, ai, bbr, bbi, ccr, cci, dsk)


def _glu_out_kernel(g_ref, x_ref, wg_ref, bg_ref, wo_ref, o_ref):
    g = g_ref[...]
    z = _dot(g.astype(BF16), wg_ref[...]) + bg_ref[...]
    o = g * jax.nn.sigmoid(z)
    o_ref[...] = x_ref[...] + _dot(o.astype(BF16), wo_ref[...])


def _glu_out(g_tm, x2d, w_glu_bf, b_glu, w_out_bf, *, n_batch, tile):
    rows, d = x2d.shape
    t = rows // n_batch
    tiles = t // tile
    g2 = g_tm.reshape(t, n_batch * d)
    const2 = lambda b, i: (0, 0)
    return pl.pallas_call(
        _glu_out_kernel,
        grid=(n_batch, tiles),
        in_specs=[
            pl.BlockSpec((tile, d), lambda b, i: (i, b)),
            pl.BlockSpec((tile, d), lambda b, i: (b * tiles + i, 0)),
            pl.BlockSpec((d, d), const2),
            pl.BlockSpec((1, d), const2),
            pl.BlockSpec((d, d), const2),
        ],
        out_specs=pl.BlockSpec((tile, d), lambda b, i: (b * tiles + i, 0)),
        out_shape=jax.ShapeDtypeStruct((rows, d), F32),
        compiler_params=_params(("parallel", "parallel")),
        name="s5_glu_out",
    )(g2, x2d, w_glu_bf, b_glu, w_out_bf)


def _s5_discretize(a_re, a_im, log_dt, b_re, b_im, c_re, c_im, d_skip):
    dt = jnp.exp(log_dt)[:, None]
    mag = jnp.exp(a_re * dt)
    abar_re = mag * jnp.cos(a_im * dt)
    abar_im = mag * jnp.sin(a_im * dt)
    num_re = abar_re - 1.0
    num_im = abar_im
    den = a_re * a_re + a_im * a_im
    f_re = (num_re * a_re + num_im * a_im) / den
    f_im = (num_im * a_re - num_re * a_im) / den
    bb_re = f_re[..., None] * b_re - f_im[..., None] * b_im
    bb_im = f_re[..., None] * b_im + f_im[..., None] * b_re
    gpb = MXU_DIM // SSM_GROUP
    eye = jnp.eye(gpb, dtype=F32)

    def in_blocks(bb):
        bt = jnp.transpose(bb, (0, 2, 1)).reshape(N_DIAG_BLOCKS, gpb, SSM_GROUP, SSM_STATE)
        full = bt[:, :, :, None, :] * eye[None, :, None, :, None]
        return full.reshape(N_DIAG_BLOCKS, MXU_DIM, DIAG_COLS).astype(BF16)

    def out_blocks(cc):
        ct = jnp.transpose(cc, (0, 2, 1)).reshape(N_DIAG_BLOCKS, gpb, SSM_STATE, SSM_GROUP)
        full = ct[:, :, :, None, :] * eye[None, :, None, :, None]
        return full.reshape(N_DIAG_BLOCKS, DIAG_COLS, MXU_DIM).astype(BF16)

    ar = abar_re.reshape(N_SCAN_BLOCKS, 1, SCAN_COLS)
    ai = abar_im.reshape(N_SCAN_BLOCKS, 1, SCAN_COLS)
    return (ar, ai, in_blocks(bb_re), in_blocks(bb_im), out_blocks(c_re), out_blocks(-c_im),
            d_skip.reshape(1, D_MODEL))


def _gmlp_kernel(x_ref, g_ref, win_ref, bin_ref, lng_ref, lnb_ref, mix_ref, mixb_ref, wout_ref,
                 o_ref, v_ref, s_scr, *, rows):
    x = x_ref[...]
    hn = _rms(x, g_ref[...]).astype(BF16)
    z = _gelu(_dot(hn, win_ref[...]) + bin_ref[...])
    u = z[:, :GMLP_HALF]
    v = z[:, GMLP_HALF:]
    mu = jnp.mean(v, axis=-1, keepdims=True)
    vc = v - mu
    vn = vc * lax.rsqrt(jnp.mean(vc * vc, axis=-1, keepdims=True) + EPS) * lng_ref[...] + lnb_ref[...]
    v_ref[...] = vn
    vb = vn.astype(BF16)
    for r in range(rows // CHUNK):
        rs = slice(r * CHUNK, (r + 1) * CHUNK)
        for h in range(GMLP_HEADS):
            cs = slice(h * GMLP_HEAD_DIM, (h + 1) * GMLP_HEAD_DIM)
            mixed = _dot(mix_ref[h], vb[rs, cs]) + mixb_ref[h]
            s_scr[rs, cs] = (u[rs, cs] * mixed).astype(BF16)
    o_ref[...] = x + _dot(s_scr[...], wout_ref[...])


def _gmlp(x2d, g, w_in_bf, b_in, ln_g, ln_b, mix_bf, mix_bias, w_out_bf, *, tile):
    rows, d = x2d.shape
    const2 = lambda i: (0, 0)
    const3 = lambda i: (0, 0, 0)
    kern = functools.partial(_gmlp_kernel, rows=tile)
    return pl.pallas_call(
        kern,
        grid=(rows // tile,),
        in_specs=[
            pl.BlockSpec((tile, d), lambda i: (i, 0)),
            pl.BlockSpec((1, d), const2),
            pl.BlockSpec((d, 2 * GMLP_HALF), const2),
            pl.BlockSpec((1, 2 * GMLP_HALF), const2),
            pl.BlockSpec((1, GMLP_HALF), const2),
            pl.BlockSpec((1, GMLP_HALF), const2),
            pl.BlockSpec((GMLP_HEADS, CHUNK, CHUNK), const3),
            pl.BlockSpec((GMLP_HEADS, CHUNK, GMLP_HEAD_DIM), const3),
            pl.BlockSpec((GMLP_HALF, d), const2),
        ],
        out_specs=[
            pl.BlockSpec((tile, d), lambda i: (i, 0)),
            pl.BlockSpec((tile, GMLP_HALF), lambda i: (i, 0)),
        ],
        out_shape=[
            jax.ShapeDtypeStruct((rows, d), F32),
            jax.ShapeDtypeStruct((rows, GMLP_HALF), F32),
        ],
        scratch_shapes=[pltpu.VMEM((tile, GMLP_HALF), BF16)],
        compiler_params=_params(("parallel",)),
        name="gmlp",
    )(x2d, g, w_in_bf, b_in, ln_g, ln_b, mix_bf, mix_bias, w_out_bf)


def _top16_desc(a, want_rank):
    rows = []
    arr = jnp.full((PEER_TOPK, a.shape[1]), NEG_INF, F32)
    rid = lax.broadcasted_iota(jnp.int32, (PEER_TOPK, a.shape[1]), 0)
    rank = jnp.zeros_like(a) if want_rank else None
    m = None
    for r in range(PEER_TOPK):
        if m is None:
            cur = a
        else:
            below = a < m
            cur = jnp.where(below, a, NEG_INF)
            if want_rank:
                rank = jnp.where(below, float(r), rank)
        m = jnp.max(cur, axis=0, keepdims=True)
        rows.append(m)
        arr = jnp.where(rid == r, m, arr)
    if want_rank:
        rank = jnp.where(a < m, float(PEER_TOPK), rank)
    return rows, arr, rank


def _peer_thresholds(s1, s2):
    v1, _, _ = _top16_desc(s1, False)
    v2, v2arr, rank2 = _top16_desc(s2, True)
    v2lo = v2arr[0:SUBLANES]
    v2hi = v2arr[SUBLANES:PEER_TOPK]
    rid = lax.broadcasted_iota(jnp.int32, v2lo.shape, 0)
    cands = [(0, v1[0] + v2lo), (0, v1[0] + v2hi)]
    for a in range(1, PEER_TOPK):
        nb = PEER_TOPK // (a + 1)
        cands.append((a, jnp.where(rid < nb, v1[a] + v2lo, NEG_INF)))
    m = None
    for r in range(PEER_TOPK):
        cur = None
        for _, cv in cands:
            x = cv if m is None else jnp.where(cv < m, cv, NEG_INF)
            cur = x if cur is None else jnp.maximum(cur, x)
        m = jnp.max(cur, axis=0, keepdims=True)
    tau = m
    cmax = v1[0] + v2[0]
    z = None
    nb_rank = [None] * PEER_TOPK
    for a, cv in cands:
        sel = cv >= tau
        ez = jnp.sum(jnp.where(sel, jnp.exp(cv - cmax), 0.0), axis=0, keepdims=True)
        z = ez if z is None else z + ez
        cnt = jnp.sum(jnp.where(sel, 1.0, 0.0), axis=0, keepdims=True)
        nb_rank[a] = cnt if nb_rank[a] is None else nb_rank[a] + cnt
    nb = jnp.zeros_like(s1)
    for a in range(PEER_TOPK):
        nb = jnp.where(s1 == v1[a], nb_rank[a], nb)
    c1 = jnp.exp(s1 - v1[0]) * (0.5 / z)
    e2 = jnp.exp(s2 - v2[0])
    return nb, rank2, c1, e2


def _peer_kernel(x_ref, g_ref, gf_ref, wq_ref, k1_ref, k2_ref, u_ref, vt_ref, o_ref,
                 xn_scr, s1_scr, s2_scr, nb_scr, c1_scr, r2_scr, e2_scr, h_scr, w_scr, acc_scr,
                 *, tm, eb, final_norm):
    e = pl.program_id(1)
    n_e = pl.num_programs(1)
    n_tc = tm // LANES
    keys_per_block = eb // N_KEYS
    keys_per_trip = 2

    @pl.when(e == 0)
    def prepare():
        xn_scr[...] = _rms(x_ref[...], g_ref[...]).T.astype(BF16)
        for h in range(PEER_HEADS):
            qt = _dot(wq_ref[h * PEER_QUERY:(h + 1) * PEER_QUERY, :], xn_scr[...])
            s1 = _dot(k1_ref[...], qt[:PEER_HALF].astype(BF16))
            s2 = _dot(k2_ref[...], qt[PEER_HALF:].astype(BF16))
            for tc in range(n_tc):
                s1_scr[h, tc] = s1[:, tc * LANES:(tc + 1) * LANES]
                s2_scr[h, tc] = s2[:, tc * LANES:(tc + 1) * LANES]

        def thresholds(idx, carry):
            h = idx // n_tc
            tc = idx % n_tc
            nb, rank2, c1, e2 = _peer_thresholds(s1_scr[h, tc], s2_scr[h, tc])
            nb_scr[h, tc] = nb
            c1_scr[h, tc] = c1
            r2_scr[h, tc] = rank2.astype(BF16)
            e2_scr[h, tc] = e2.astype(BF16)
            return carry

        lax.fori_loop(0, PEER_HEADS * n_tc, thresholds, 0)
        acc_scr[...] = jnp.zeros_like(acc_scr)

    h_scr[...] = _dot(u_ref[...], xn_scr[...])

    def key_rows(p, carry):
        i1 = e * keys_per_block + p * keys_per_trip
        for tc in range(n_tc):
            lanes = slice(tc * LANES, (tc + 1) * LANES)
            gates = [jnp.zeros((N_KEYS, LANES), BF16) for _ in range(keys_per_trip)]
            for h in range(PEER_HEADS):
                r2 = r2_scr[h, tc]
                e2 = e2_scr[h, tc]
                for k in range(keys_per_trip):
                    nb = jnp.broadcast_to(nb_scr[h, tc, pl.ds(i1 + k, 1), :], (N_KEYS, LANES)).astype(BF16)
                    c1 = jnp.broadcast_to(c1_scr[h, tc, pl.ds(i1 + k, 1), :], (N_KEYS, LANES)).astype(BF16)
                    gates[k] = gates[k] + jnp.where(r2 < nb, c1 * e2, 0.0)
            for k in range(keys_per_trip):
                rows = pl.ds(pl.multiple_of((p * keys_per_trip + k) * N_KEYS, N_KEYS), N_KEYS)
                hv = h_scr[rows, lanes]
                w_scr[rows, lanes] = gates[k] * (hv * (1.0 + lax.erf(hv * SQRT_HALF))).astype(BF16)
        return carry

    lax.fori_loop(0, keys_per_block // keys_per_trip, key_rows, 0)
    acc_scr[...] += _dot(vt_ref[...], w_scr[...])

    @pl.when(e == n_e - 1)
    def _():
        y = x_ref[...] + acc_scr[...].T
        if final_norm:
            y = _rms(y, gf_ref[...])
        o_ref[...] = y


def _peer(x2d, g, g_final, wq_t_bf, k1_bf, k2_bf, u_bf, v_t_bf, *, final_norm, tm=512, eb=1024):
    n, d = x2d.shape
    n_tc = tm // LANES
    n_e = N_EXPERTS // eb
    const2 = lambda i, e: (0, 0)
    kern = functools.partial(_peer_kernel, tm=tm, eb=eb, final_norm=final_norm)
    tile4 = (PEER_HEADS, n_tc, N_KEYS, LANES)
    return pl.pallas_call(
        kern,
        grid=(n // tm, n_e),
        in_specs=[
            pl.BlockSpec((tm, d), lambda i, e: (i, 0)),
            pl.BlockSpec((1, d), const2),
            pl.BlockSpec((1, d), const2),
            pl.BlockSpec((PEER_HEADS * PEER_QUERY, d), const2),
            pl.BlockSpec((N_KEYS, PEER_HALF), const2),
            pl.BlockSpec((N_KEYS, PEER_HALF), const2),
            pl.BlockSpec((eb, d), lambda i, e: (e, 0)),
            pl.BlockSpec((d, eb), lambda i, e: (0, e)),
        ],
        out_specs=pl.BlockSpec((tm, d), lambda i, e: (i, 0)),
        out_shape=jax.ShapeDtypeStruct((n, d), F32),
        scratch_shapes=[
            pltpu.VMEM((d, tm), BF16),
            pltpu.VMEM(tile4, F32),
            pltpu.VMEM(tile4, F32),
            pltpu.VMEM(tile4, F32),
            pltpu.VMEM(tile4, F32),
            pltpu.VMEM(tile4, BF16),
            pltpu.VMEM(tile4, BF16),
            pltpu.VMEM((eb, tm), F32),
            pltpu.VMEM((eb, tm), BF16),
            pltpu.VMEM((d, tm), F32),
        ],
        compiler_params=_params(("parallel", "arbitrary")),
        name="peer",
    )(x2d, g, g_final, wq_t_bf, k1_bf, k2_bf, u_bf, v_t_bf)


def _trunk(x, h0_re, h0_im, w, *, scan_steps):
    bsz, t, d = x.shape
    n = bsz * t
    x2d = x.reshape(n, d)
    row_tile = min(512, t)

    if t >= row_tile and t % row_tile == 0 and bsz <= 8:
        u_tm = _norm_matmul(x2d, w["norm_mix_g"][0], w["ssm_w_in"], n_batch=bsz, tile=row_tile)
        g_tm, hr, hi = _s5_scan(u_tm, h0_re, h0_im, w["disc"], batch=bsz, steps=scan_steps)
        x2d = _glu_out(g_tm, x2d, w["ssm_w_glu"], w["ssm_b_glu"], w["ssm_w_out"], n_batch=bsz, tile=row_tile)
    else:
        u_bm = _norm_matmul(x2d, w["norm_mix_g"][0], w["ssm_w_in"], n_batch=1, tile=n)
        u_tm = u_bm.reshape(bsz, t, d).transpose(1, 0, 2).reshape(n, d)
        g_tm, hr, hi = _s5_scan(u_tm, h0_re, h0_im, w["disc"], batch=bsz, steps=scan_steps)
        g_bm = g_tm.reshape(t, bsz, d).transpose(1, 0, 2).reshape(n, d)
        x2d = _glu_out(g_bm, x2d, w["ssm_w_glu"], w["ssm_b_glu"], w["ssm_w_out"], n_batch=1, tile=n)
    x2d = _peer(x2d, w["norm_ffn_g"][0], w["norm_final_g"], *w["peer"][0], final_norm=False)

    if t % CHUNK == 0:
        mix, mix_bias = w["mix_full"]
    else:
        mix, mix_bias = w["mix_short"]
    x2d, v = _gmlp(x2d, w["norm_mix_g"][1], w["gmlp_w_in"], w["gmlp_b_in"], w["gmlp_ln_g"], w["gmlp_ln_b"],
                   mix, mix_bias, w["gmlp_w_out"], tile=256)
    y2d = _peer(x2d, w["norm_ffn_g"][1], w["norm_final_g"], *w["peer"][1], final_norm=True)
    return y2d.reshape(bsz, t, d), hr, hi, v


def kernel(x_prompt, x_sample, state_ssm_re, state_ssm_im, norm_mix_g, norm_ffn_g, norm_final_g, ssm_w_in, ssm_a_re, ssm_a_im, ssm_log_dt, ssm_b_re, ssm_b_im, ssm_c_re, ssm_c_im, ssm_d, ssm_w_glu, ssm_b_glu, ssm_w_out, gmlp_w_in, gmlp_b_in, gmlp_ln_g, gmlp_ln_b, gmlp_w_s, gmlp_b_s, gmlp_w_out, peer_w_q, peer_k1, peer_k2, peer_u, peer_v):
    bp, tp, d = x_prompt.shape
    bs, ts, _ = x_sample.shape
    assert tp % CHUNK == 0 and CHUNK % ts == 0 and d == D_MODEL

    tril = jnp.tril(jnp.ones((CHUNK, CHUNK), F32))
    ws = gmlp_w_s[0] * tril
    bias_full = jnp.broadcast_to(gmlp_b_s[0][:, :, None], (GMLP_HEADS, CHUNK, GMLP_HEAD_DIM))
    reps = CHUNK // ts
    ws_short = jnp.einsum("ab,hij->haibj", jnp.eye(reps, dtype=F32), ws[:, :ts, :ts]).reshape(GMLP_HEADS, CHUNK, CHUNK)
    bias_short = jnp.broadcast_to(jnp.tile(gmlp_b_s[0][:, :ts], (1, reps))[:, :, None],
                                  (GMLP_HEADS, CHUNK, GMLP_HEAD_DIM))

    w = {
        "norm_mix_g": norm_mix_g[:, None, :],
        "norm_ffn_g": norm_ffn_g[:, None, :],
        "norm_final_g": norm_final_g[None, :],
        "ssm_w_in": ssm_w_in[0].astype(BF16),
        "disc": _s5_discretize(ssm_a_re[0], ssm_a_im[0], ssm_log_dt[0], ssm_b_re[0], ssm_b_im[0],
                               ssm_c_re[0], ssm_c_im[0], ssm_d[0]),
        "ssm_w_glu": ssm_w_glu[0].astype(BF16),
        "ssm_b_glu": ssm_b_glu[0][None, :],
        "ssm_w_out": ssm_w_out[0].astype(BF16),
        "gmlp_w_in": gmlp_w_in[0].astype(BF16),
        "gmlp_b_in": gmlp_b_in[0][None, :],
        "gmlp_ln_g": gmlp_ln_g[0][None, :],
        "gmlp_ln_b": gmlp_ln_b[0][None, :],
        "mix_full": (ws.astype(BF16), bias_full),
        "mix_short": (ws_short.astype(BF16), bias_short),
        "gmlp_w_out": gmlp_w_out[0].astype(BF16),
        "peer": [
            (peer_w_q[i].T.astype(BF16), peer_k1[i].astype(BF16), peer_k2[i].astype(BF16),
             peer_u[i].astype(BF16), peer_v[i].T.astype(BF16))
            for i in range(2)
        ],
    }

    zeros = jnp.zeros((bp, SSM_COLS), F32)
    y_p, hr_p, hi_p, _ = _trunk(x_prompt, zeros, zeros, w, scan_steps=32)
    y_s, hr_s, hi_s, v_s = _trunk(x_sample, state_ssm_re[0].reshape(bs, SSM_COLS),
                                  state_ssm_im[0].reshape(bs, SSM_COLS), w, scan_steps=ts)
    st = lambda a, b: a.reshape(1, b, SSM_GROUPS, SSM_STATE)
    return (y_p, y_s, st(hr_p, bp), st(hi_p, bp), st(hr_s, bs), st(hi_s, bs),
            v_s.reshape(1, bs, ts, GMLP_HALF))
```

```python
import functools
import math

import jax
import jax.numpy as jnp
from jax import lax
from jax.experimental import pallas as pl
from jax.experimental.pallas import tpu as pltpu

F32 = jnp.float32
BF16 = jnp.bfloat16

EPS = 1e-6
D_MODEL = 1024
SSM_GROUP = 16
SSM_GROUPS = 64
SSM_STATE = 64
SSM_COLS = SSM_GROUPS * SSM_STATE
GMLP_HALF = 2 * D_MODEL
GMLP_HEADS = 8
GMLP_HEAD_DIM = GMLP_HALF // GMLP_HEADS
CHUNK = 128
PEER_HEADS = 8
N_KEYS = 128
N_EXPERTS = N_KEYS * N_KEYS
PEER_QUERY = 256
PEER_HALF = 128
PEER_TOPK = 16

LANES = 128
SUBLANES = 8
MXU_DIM = 256
VMEM_LIMIT = 56 * 1024 * 1024

NEG_INF = float("-inf")
POS_INF = float("inf")
SQRT_HALF = math.sqrt(0.5)


def _dot(a, b):
    return jnp.dot(a, b, preferred_element_type=F32)


def _rms(x, g):
    ms = jnp.mean(x * x, axis=-1, keepdims=True)
    return x * lax.rsqrt(ms + EPS) * g


def _gelu(x):
    return 0.5 * x * (1.0 + lax.erf(x * SQRT_HALF))


def _params(sem, flags=None):
    return pltpu.CompilerParams(dimension_semantics=sem, vmem_limit_bytes=VMEM_LIMIT, flags=flags)


def _norm_matmul_kernel(x_ref, g_ref, w_ref, o_ref):
    xn = _rms(x_ref[...], g_ref[...]).astype(BF16)
    o_ref[...] = _dot(xn, w_ref[...])


def _norm_matmul(x2d, g, w_bf, *, n_batch, tile):
    rows, d = x2d.shape
    n = w_bf.shape[1]
    t = rows // n_batch
    tiles = t // tile
    out = pl.pallas_call(
        _norm_matmul_kernel,
        grid=(n_batch, tiles),
        in_specs=[
            pl.BlockSpec((tile, d), lambda b, i: (b * tiles + i, 0)),
            pl.BlockSpec((1, d), lambda b, i: (0, 0)),
            pl.BlockSpec((d, n), lambda b, i: (0, 0)),
        ],
        out_specs=pl.BlockSpec((tile, n), lambda b, i: (i, b)),
        out_shape=jax.ShapeDtypeStruct((t, n_batch * n), F32),
        compiler_params=_params(("parallel", "parallel")),
        name="s5_in_proj",
    )(x2d, g, w_bf)
    return out.reshape(t * n_batch, n)


SCAN_COLS = 512
N_SCAN_BLOCKS = SSM_COLS // SCAN_COLS
N_DIAG_BLOCKS = D_MODEL // MXU_DIM
DIAG_COLS = SSM_COLS // N_DIAG_BLOCKS
SCAN_PER_DIAG = DIAG_COLS // SCAN_COLS


def _s5_scan_kernel(u_ref, h0r_ref, h0i_ref, ar_ref, ai_ref, bbr_ref, bbi_ref, ccr_ref, cci_ref, d_ref,
                    g_ref, hr_ref, hi_ref, bur, bui, st_r, st_i, *, steps, batch):
    c = pl.program_id(0)
    n_sub = batch // SUBLANES

    @pl.when(c == 0)
    def _():
        for cb in range(N_SCAN_BLOCKS):
            st_r[cb] = h0r_ref[:, cb * SCAN_COLS:(cb + 1) * SCAN_COLS]
            st_i[cb] = h0i_ref[:, cb * SCAN_COLS:(cb + 1) * SCAN_COLS]

    u = u_ref[...]
    ub = u.astype(BF16)
    for kb in range(N_DIAG_BLOCKS):
        lhs = ub[:, kb * MXU_DIM:(kb + 1) * MXU_DIM]
        pr = _dot(lhs, bbr_ref[kb])
        pi = _dot(lhs, bbi_ref[kb])
        for j in range(SCAN_PER_DIAG):
            bur[kb * SCAN_PER_DIAG + j] = pr[:, j * SCAN_COLS:(j + 1) * SCAN_COLS]
            bui[kb * SCAN_PER_DIAG + j] = pi[:, j * SCAN_COLS:(j + 1) * SCAN_COLS]

    def scan_block(idx, carry):
        cb = idx // n_sub
        s = idx % n_sub
        ar = jnp.broadcast_to(ar_ref[cb], (SUBLANES, SCAN_COLS))
        ai = jnp.broadcast_to(ai_ref[cb], (SUBLANES, SCAN_COLS))
        row0 = pl.multiple_of(s * SUBLANES, SUBLANES)
        h_r = st_r[cb, pl.ds(row0, SUBLANES), :]
        h_i = st_i[cb, pl.ds(row0, SUBLANES), :]

        def step(t, h):
            hr, hi = h
            r = pl.multiple_of(t * batch + row0, SUBLANES)
            nr = ar * hr - ai * hi + bur[cb, pl.ds(r, SUBLANES), :]
            ni = ar * hi + ai * hr + bui[cb, pl.ds(r, SUBLANES), :]
            bur[cb, pl.ds(r, SUBLANES), :] = nr
            bui[cb, pl.ds(r, SUBLANES), :] = ni
            return nr, ni

        h_r, h_i = lax.fori_loop(0, steps, step, (h_r, h_i), unroll=min(steps, 8))
        st_r[cb, pl.ds(row0, SUBLANES), :] = h_r
        st_i[cb, pl.ds(row0, SUBLANES), :] = h_i
        return carry

    lax.fori_loop(0, N_SCAN_BLOCKS * n_sub, scan_block, 0)

    for cb in range(N_SCAN_BLOCKS):
        hr_ref[:, cb * SCAN_COLS:(cb + 1) * SCAN_COLS] = st_r[cb]
        hi_ref[:, cb * SCAN_COLS:(cb + 1) * SCAN_COLS] = st_i[cb]

    for nb in range(N_DIAG_BLOCKS):
        acc = None
        for j in range(SCAN_PER_DIAG):
            cb = nb * SCAN_PER_DIAG + j
            part = (_dot(bur[cb].astype(BF16), ccr_ref[nb, j * SCAN_COLS:(j + 1) * SCAN_COLS, :])
                    + _dot(bui[cb].astype(BF16), cci_ref[nb, j * SCAN_COLS:(j + 1) * SCAN_COLS, :]))
            acc = part if acc is None else acc + part
        cols = slice(nb * MXU_DIM, (nb + 1) * MXU_DIM)
        y = acc + d_ref[:, cols] * u[:, cols]
        g_ref[:, cols] = _gelu(y)


def _s5_scan(u_tm, h0_re, h0_im, disc, *, batch, steps):
    rows, d = u_tm.shape
    t = rows // batch
    n_chunks = t // steps
    blk = steps * batch
    ar, ai, bbr, bbi, ccr, cci, dsk = disc
    const2 = lambda c: (0, 0)
    const3 = lambda c: (0, 0, 0)
    kern = functools.partial(_s5_scan_kernel, steps=steps, batch=batch)
    return pl.pallas_call(
        kern,
        grid=(n_chunks,),
        in_specs=[
            pl.BlockSpec((blk, d), lambda c: (c, 0)),
            pl.BlockSpec((batch, SSM_COLS), const2),
            pl.BlockSpec((batch, SSM_COLS), const2),
            pl.BlockSpec((N_SCAN_BLOCKS, 1, SCAN_COLS), const3),
            pl.BlockSpec((N_SCAN_BLOCKS, 1, SCAN_COLS), const3),
            pl.BlockSpec((N_DIAG_BLOCKS, MXU_DIM, DIAG_COLS), const3),
            pl.BlockSpec((N_DIAG_BLOCKS, MXU_DIM, DIAG_COLS), const3),
            pl.BlockSpec((N_DIAG_BLOCKS, DIAG_COLS, MXU_DIM), const3),
            pl.BlockSpec((N_DIAG_BLOCKS, DIAG_COLS, MXU_DIM), const3),
            pl.BlockSpec((1, d), const2),
        ],
        out_specs=[
            pl.BlockSpec((blk, d), lambda c: (c, 0)),
            pl.BlockSpec((batch, SSM_COLS), const2),
            pl.BlockSpec((batch, SSM_COLS), const2),
        ],
        out_shape=[
            jax.ShapeDtypeStruct((rows, d), F32),
            jax.ShapeDtypeStruct((batch, SSM_COLS), F32),
            jax.ShapeDtypeStruct((batch, SSM_COLS), F32),
        ],
        scratch_shapes=[
            pltpu.VMEM((N_SCAN_BLOCKS, blk, SCAN_COLS), F32),
            pltpu.VMEM((N_SCAN_BLOCKS, blk, SCAN_COLS), F32),
            pltpu.VMEM((N_SCAN_BLOCKS, batch, SCAN_COLS), F32),
            pltpu.VMEM((N_SCAN_BLOCKS, batch, SCAN_COLS), F32),
        ],
        compiler_params=_params(("arbitrary",)),
        name="s5_scan",
    )(u_tm, h0_re, h0_im, ar, ai, bbr, bbi, ccr, cci, dsk)


def _glu_out_kernel(g_ref, x_ref, wg_ref, bg_ref, wo_ref, o_ref):
    g = g_ref[...]
    z = _dot(g.astype(BF16), wg_ref[...]) + bg_ref[...]
    o = g * jax.nn.sigmoid(z)
    o_ref[...] = x_ref[...] + _dot(o.astype(BF16), wo_ref[...])


def _glu_out(g_tm, x2d, w_glu_bf, b_glu, w_out_bf, *, n_batch, tile):
    rows, d = x2d.shape
    t = rows // n_batch
    tiles = t // tile
    g2 = g_tm.reshape(t, n_batch * d)
    const2 = lambda b, i: (0, 0)
    return pl.pallas_call(
        _glu_out_kernel,
        grid=(n_batch, tiles),
        in_specs=[
            pl.BlockSpec((tile, d), lambda b, i: (i, b)),
            pl.BlockSpec((tile, d), lambda b, i: (b * tiles + i, 0)),
            pl.BlockSpec((d, d), const2),
            pl.BlockSpec((1, d), const2),
            pl.BlockSpec((d, d), const2),
        ],
        out_specs=pl.BlockSpec((tile, d), lambda b, i: (b * tiles + i, 0)),
        out_shape=jax.ShapeDtypeStruct((rows, d), F32),
        compiler_params=_params(("parallel", "parallel")),
        name="s5_glu_out",
    )(g2, x2d, w_glu_bf, b_glu, w_out_bf)


def _s5_discretize(a_re, a_im, log_dt, b_re, b_im, c_re, c_im, d_skip):
    dt = jnp.exp(log_dt)[:, None]
    mag = jnp.exp(a_re * dt)
    abar_re = mag * jnp.cos(a_im * dt)
    abar_im = mag * jnp.sin(a_im * dt)
    num_re = abar_re - 1.0
    num_im = abar_im
    den = a_re * a_re + a_im * a_im
    f_re = (num_re * a_re + num_im * a_im) / den
    f_im = (num_im * a_re - num_re * a_im) / den
    bb_re = f_re[..., None] * b_re - f_im[..., None] * b_im
    bb_im = f_re[..., None] * b_im + f_im[..., None] * b_re
    gpb = MXU_DIM // SSM_GROUP
    eye = jnp.eye(gpb, dtype=F32)

    def in_blocks(bb):
        bt = jnp.transpose(bb, (0, 2, 1)).reshape(N_DIAG_BLOCKS, gpb, SSM_GROUP, SSM_STATE)
        full = bt[:, :, :, None, :] * eye[None, :, None, :, None]
        return full.reshape(N_DIAG_BLOCKS, MXU_DIM, DIAG_COLS).astype(BF16)

    def out_blocks(cc):
        ct = jnp.transpose(cc, (0, 2, 1)).reshape(N_DIAG_BLOCKS, gpb, SSM_STATE, SSM_GROUP)
        full = ct[:, :, :, None, :] * eye[None, :, None, :, None]
        return full.reshape(N_DIAG_BLOCKS, DIAG_COLS, MXU_DIM).astype(BF16)

    ar = abar_re.reshape(N_SCAN_BLOCKS, 1, SCAN_COLS)
    ai = abar_im.reshape(N_SCAN_BLOCKS, 1, SCAN_COLS)
    return (ar, ai, in_blocks(bb_re), in_blocks(bb_im), out_blocks(c_re), out_blocks(-c_im),
            d_skip.reshape(1, D_MODEL))


def _gmlp_kernel(x_ref, g_ref, win_ref, bin_ref, lng_ref, lnb_ref, mix_ref, mixb_ref, wout_ref,
                 o_ref, v_ref, s_scr, *, rows):
    x = x_ref[...]
    hn = _rms(x, g_ref[...]).astype(BF16)
    z = _gelu(_dot(hn, win_ref[...]) + bin_ref[...])
    u = z[:, :GMLP_HALF]
    v = z[:, GMLP_HALF:]
    mu = jnp.mean(v, axis=-1, keepdims=True)
    vc = v - mu
    vn = vc * lax.rsqrt(jnp.mean(vc * vc, axis=-1, keepdims=True) + EPS) * lng_ref[...] + lnb_ref[...]
    v_ref[...] = vn
    vb = vn.astype(BF16)
    for r in range(rows // CHUNK):
        rs = slice(r * CHUNK, (r + 1) * CHUNK)
        for h in range(GMLP_HEADS):
            cs = slice(h * GMLP_HEAD_DIM, (h + 1) * GMLP_HEAD_DIM)
            mixed = _dot(mix_ref[h], vb[rs, cs]) + mixb_ref[h]
            s_scr[rs, cs] = (u[rs, cs] * mixed).astype(BF16)
    o_ref[...] = x + _dot(s_scr[...], wout_ref[...])


def _gmlp(x2d, g, w_in_bf, b_in, ln_g, ln_b, mix_bf, mix_bias, w_out_bf, *, tile):
    rows, d = x2d.shape
    const2 = lambda i: (0, 0)
    const3 = lambda i: (0, 0, 0)
    kern = functools.partial(_gmlp_kernel, rows=tile)
    return pl.pallas_call(
        kern,
        grid=(rows // tile,),
        in_specs=[
            pl.BlockSpec((tile, d), lambda i: (i, 0)),
            pl.BlockSpec((1, d), const2),
            pl.BlockSpec((d, 2 * GMLP_HALF), const2),
            pl.BlockSpec((1, 2 * GMLP_HALF), const2),
            pl.BlockSpec((1, GMLP_HALF), const2),
            pl.BlockSpec((1, GMLP_HALF), const2),
            pl.BlockSpec((GMLP_HEADS, CHUNK, CHUNK), const3),
            pl.BlockSpec((GMLP_HEADS, CHUNK, GMLP_HEAD_DIM), const3),
            pl.BlockSpec((GMLP_HALF, d), const2),
        ],
        out_specs=[
            pl.BlockSpec((tile, d), lambda i: (i, 0)),
            pl.BlockSpec((tile, GMLP_HALF), lambda i: (i, 0)),
        ],
        out_shape=[
            jax.ShapeDtypeStruct((rows, d), F32),
            jax.ShapeDtypeStruct((rows, GMLP_HALF), F32),
        ],
        scratch_shapes=[pltpu.VMEM((tile, GMLP_HALF), BF16)],
        compiler_params=_params(("parallel",)),
        name="gmlp",
    )(x2d, g, w_in_bf, b_in, ln_g, ln_b, mix_bf, mix_bias, w_out_bf)


KEY_PITCH = N_KEYS + SUBLANES


def _sorting_network(n):
    pairs = []
    p = 1
    while p < n:
        k = p
        while k >= 1:
            for j in range(k % p, n - k, 2 * k):
                for i in range(min(k, n - j - k)):
                    if (i + j) // (2 * p) == (i + j + k) // (2 * p):
                        pairs.append((i + j, i + j + k))
            k //= 2
        p *= 2
    return pairs


SORT16 = _sorting_network(PEER_TOPK)


def _compare_exchange(v, i, j):
    v[i], v[j] = jnp.maximum(v[i], v[j]), jnp.minimum(v[i], v[j])


def _sort16_desc(v):
    v = list(v)
    for i, j in SORT16:
        _compare_exchange(v, i, j)
    return v


def _merge_top16(a, b):
    c = [jnp.maximum(a[i], b[PEER_TOPK - 1 - i]) for i in range(PEER_TOPK)]
    d = PEER_TOPK // 2
    while d >= 1:
        for i in range(PEER_TOPK):
            if i & d == 0:
                _compare_exchange(c, i, i + d)
        d //= 2
    return c


def _top16_of(values):
    lists = [_sort16_desc(values[i:i + PEER_TOPK]) for i in range(0, len(values), PEER_TOPK)]
    while len(lists) > 1:
        merged = [_merge_top16(lists[i], lists[i + 1]) for i in range(0, len(lists) - 1, 2)]
        if len(lists) % 2:
            merged.append(lists[-1])
        lists = merged
    return lists[0]


def _peer_thresholds(s1_t, s2_t, e2_t, s1_v, s2_v, th_v, c1_v, grp):
    base = grp * (SUBLANES * KEY_PITCH)

    def key_rows(k):
        return pl.ds(base + k, SUBLANES, stride=KEY_PITCH)

    for k in range(N_KEYS):
        s1_v[k] = s1_t[key_rows(k), :]
        s2_v[k] = s2_t[key_rows(k), :]
    v1 = _top16_of([s1_v[k] for k in range(N_KEYS)])
    v2 = _top16_of([s2_v[k] for k in range(N_KEYS)])

    pairs = [(a, b) for a in range(PEER_TOPK) for b in range(PEER_TOPK // (a + 1))]
    cand = {ab: v1[ab[0]] + v2[ab[1]] for ab in pairs}
    first_row = [cand[(0, b)] for b in range(PEER_TOPK)]
    rest = [cand[ab] for ab in pairs if ab[0] > 0]
    rest += [jnp.full_like(v1[0], NEG_INF)] * (-len(rest) % PEER_TOPK)
    tau = _merge_top16(first_row, _top16_of(rest))[PEER_TOPK - 1]

    cmax = cand[(0, 0)]
    z = jnp.zeros_like(tau)
    th_rank = [jnp.full_like(tau, POS_INF) for _ in range(PEER_TOPK)]
    for a, b in pairs:
        sel = cand[(a, b)] >= tau
        z = z + jnp.where(sel, jnp.exp(cand[(a, b)] - cmax), 0.0)
        th_rank[a] = jnp.where(sel, v2[b], th_rank[a])
    c_scale = 0.5 / z

    def per_key(k, carry):
        s1k = s1_v[k]
        th = jnp.full_like(s1k, POS_INF)
        for a in reversed(range(PEER_TOPK)):
            th = jnp.where(s1k >= v1[a], th_rank[a], th)
        th_v[grp, k] = th
        c1_v[grp, k] = jnp.exp(s1k - v1[0]) * c_scale
        e2_t[key_rows(k), :] = jnp.exp(s2_v[k] - v2[0])
        return carry

    lax.fori_loop(0, N_KEYS, per_key, 0)


def _peer_kernel(x_ref, g_ref, gf_ref, wq_ref, k1_ref, k2_ref, u_ref, vt_ref, o_ref,
                 xn_scr, s1_t, s2_t, e2_t, s1_v, s2_v, th_v, c1_v, h_scr, w_scr, acc_scr,
                 *, tm, eb, final_norm):
    e = pl.program_id(1)
    n_e = pl.num_programs(1)
    n_tc = tm // LANES
    keys_per_block = eb // N_KEYS
    n_groups = PEER_HEADS * n_tc // SUBLANES

    def tile_row(h, tc):
        return (h * n_tc + tc) * KEY_PITCH

    @pl.when(e == 0)
    def prepare():
        xn_scr[...] = _rms(x_ref[...], g_ref[...]).T.astype(BF16)
        for h in range(PEER_HEADS):
            qt = _dot(wq_ref[h * PEER_QUERY:(h + 1) * PEER_QUERY, :], xn_scr[...])
            s1 = _dot(k1_ref[...], qt[:PEER_HALF].astype(BF16))
            s2 = _dot(k2_ref[...], qt[PEER_HALF:].astype(BF16))
            for tc in range(n_tc):
                rows = pl.ds(tile_row(h, tc), N_KEYS)
                s1_t[rows, :] = s1[:, tc * LANES:(tc + 1) * LANES]
                s2_t[rows, :] = s2[:, tc * LANES:(tc + 1) * LANES]

        def thresholds(grp, carry):
            _peer_thresholds(s1_t, s2_t, e2_t, s1_v, s2_v, th_v, c1_v, grp)
            return carry

        lax.fori_loop(0, n_groups, thresholds, 0)
        acc_scr[...] = jnp.zeros_like(acc_scr)

    h_scr[...] = _dot(u_ref[...], xn_scr[...])

    def key_row(kl, carry):
        i1 = e * keys_per_block + kl
        rows = pl.ds(pl.multiple_of(kl * N_KEYS, N_KEYS), N_KEYS)
        for tc in range(n_tc):
            lanes = slice(tc * LANES, (tc + 1) * LANES)
            gate = jnp.zeros((N_KEYS, LANES), F32)
            for h in range(PEER_HEADS):
                grp, j = divmod(h * n_tc + tc, SUBLANES)
                tile = pl.ds(tile_row(h, tc), N_KEYS)
                th = th_v[grp, i1, j:j + 1, :]
                c1 = c1_v[grp, i1, j:j + 1, :]
                gate = gate + jnp.where(s2_t[tile, :] >= th, c1 * e2_t[tile, :], 0.0)
            hv = h_scr[rows, lanes]
            w_scr[rows, lanes] = (gate * (hv * (1.0 + lax.erf(hv * SQRT_HALF)))).astype(BF16)
        return carry

    lax.fori_loop(0, keys_per_block, key_row, 0)
    acc_scr[...] += _dot(vt_ref[...], w_scr[...])

    @pl.when(e == n_e - 1)
    def _():
        y = x_ref[...] + acc_scr[...].T
        if final_norm:
            y = _rms(y, gf_ref[...])
        o_ref[...] = y


def _peer(x2d, g, g_final, wq_t_bf, k1_bf, k2_bf, u_bf, v_t_bf, *, final_norm, tm=512, eb=1024):
    n, d = x2d.shape
    n_tc = tm // LANES
    n_e = N_EXPERTS // eb
    const2 = lambda i, e: (0, 0)
    kern = functools.partial(_peer_kernel, tm=tm, eb=eb, final_norm=final_norm)
    n_tiles = PEER_HEADS * n_tc
    assert n_tiles % SUBLANES == 0
    tiles = (n_tiles * KEY_PITCH, LANES)
    per_key = (N_KEYS, SUBLANES, LANES)
    per_key_groups = (n_tiles // SUBLANES,) + per_key
    return pl.pallas_call(
        kern,
        grid=(n // tm, n_e),
        in_specs=[
            pl.BlockSpec((tm, d), lambda i, e: (i, 0)),
            pl.BlockSpec((1, d), const2),
            pl.BlockSpec((1, d), const2),
            pl.BlockSpec((PEER_HEADS * PEER_QUERY, d), const2),
            pl.BlockSpec((N_KEYS, PEER_HALF), const2),
            pl.BlockSpec((N_KEYS, PEER_HALF), const2),
            pl.BlockSpec((eb, d), lambda i, e: (e, 0)),
            pl.BlockSpec((d, eb), lambda i, e: (0, e)),
        ],
        out_specs=pl.BlockSpec((tm, d), lambda i, e: (i, 0)),
        out_shape=jax.ShapeDtypeStruct((n, d), F32),
        scratch_shapes=[
            pltpu.VMEM((d, tm), BF16),
            pltpu.VMEM(tiles, F32),
            pltpu.VMEM(tiles, F32),
            pltpu.VMEM(tiles, F32),
            pltpu.VMEM(per_key, F32),
            pltpu.VMEM(per_key, F32),
            pltpu.VMEM(per_key_groups, F32),
            pltpu.VMEM(per_key_groups, F32),
            pltpu.VMEM((eb, tm), F32),
            pltpu.VMEM((eb, tm), BF16),
            pltpu.VMEM((d, tm), F32),
        ],
        compiler_params=_params(("parallel", "arbitrary")),
        name="peer",
    )(x2d, g, g_final, wq_t_bf, k1_bf, k2_bf, u_bf, v_t_bf)


def _trunk(x, h0_re, h0_im, w, *, scan_steps):
    bsz, t, d = x.shape
    n = bsz * t
    x2d = x.reshape(n, d)
    row_tile = min(512, t)

    if t >= row_tile and t % row_tile == 0 and bsz <= 8:
        u_tm = _norm_matmul(x2d, w["norm_mix_g"][0], w["ssm_w_in"], n_batch=bsz, tile=row_tile)
        g_tm, hr, hi = _s5_scan(u_tm, h0_re, h0_im, w["disc"], batch=bsz, steps=scan_steps)
        x2d = _glu_out(g_tm, x2d, w["ssm_w_glu"], w["ssm_b_glu"], w["ssm_w_out"], n_batch=bsz, tile=row_tile)
    else:
        u_bm = _norm_matmul(x2d, w["norm_mix_g"][0], w["ssm_w_in"], n_batch=1, tile=n)
        u_tm = u_bm.reshape(bsz, t, d).transpose(1, 0, 2).reshape(n, d)
        g_tm, hr, hi = _s5_scan(u_tm, h0_re, h0_im, w["disc"], batch=bsz, steps=scan_steps)
        g_bm = g_tm.reshape(t, bsz, d).transpose(1, 0, 2).reshape(n, d)
        x2d = _glu_out(g_bm, x2d, w["ssm_w_glu"], w["ssm_b_glu"], w["ssm_w_out"], n_batch=1, tile=n)
    x2d = _peer(x2d, w["norm_ffn_g"][0], w["norm_final_g"], *w["peer"][0], final_norm=False)

    if t % CHUNK == 0:
        mix, mix_bias = w["mix_full"]
    else:
        mix, mix_bias = w["mix_short"]
    x2d, v = _gmlp(x2d, w["norm_mix_g"][1], w["gmlp_w_in"], w["gmlp_b_in"], w["gmlp_ln_g"], w["gmlp_ln_b"],
                   mix, mix_bias, w["gmlp_w_out"], tile=256)
    y2d = _peer(x2d, w["norm_ffn_g"][1], w["norm_final_g"], *w["peer"][1], final_norm=True)
    return y2d.reshape(bsz, t, d), hr, hi, v


def kernel(x_prompt, x_sample, state_ssm_re, state_ssm_im, norm_mix_g, norm_ffn_g, norm_final_g, ssm_w_in, ssm_a_re, ssm_a_im, ssm_log_dt, ssm_b_re, ssm_b_im, ssm_c_re, ssm_c_im, ssm_d, ssm_w_glu, ssm_b_glu, ssm_w_out, gmlp_w_in, gmlp_b_in, gmlp_ln_g, gmlp_ln_b, gmlp_w_s, gmlp_b_s, gmlp_w_out, peer_w_q, peer_k1, peer_k2, peer_u, peer_v):
    bp, tp, d = x_prompt.shape
    bs, ts, _ = x_sample.shape
    assert tp % CHUNK == 0 and CHUNK % ts == 0 and d == D_MODEL

    tril = jnp.tril(jnp.ones((CHUNK, CHUNK), F32))
    ws = gmlp_w_s[0] * tril
    bias_full = jnp.broadcast_to(gmlp_b_s[0][:, :, None], (GMLP_HEADS, CHUNK, GMLP_HEAD_DIM))
    reps = CHUNK // ts
    ws_short = jnp.einsum("ab,hij->haibj", jnp.eye(reps, dtype=F32), ws[:, :ts, :ts]).reshape(GMLP_HEADS, CHUNK, CHUNK)
    bias_short = jnp.broadcast_to(jnp.tile(gmlp_b_s[0][:, :ts], (1, reps))[:, :, None],
                                  (GMLP_HEADS, CHUNK, GMLP_HEAD_DIM))

    w = {
        "norm_mix_g": norm_mix_g[:, None, :],
        "norm_ffn_g": norm_ffn_g[:, None, :],
        "norm_final_g": norm_final_g[None, :],
        "ssm_w_in": ssm_w_in[0].astype(BF16),
        "disc": _s5_discretize(ssm_a_re[0], ssm_a_im[0], ssm_log_dt[0], ssm_b_re[0], ssm_b_im[0],
                               ssm_c_re[0], ssm_c_im[0], ssm_d[0]),
        "ssm_w_glu": ssm_w_glu[0].astype(BF16),
        "ssm_b_glu": ssm_b_glu[0][None, :],
        "ssm_w_out": ssm_w_out[0].astype(BF16),
        "gmlp_w_in": gmlp_w_in[0].astype(BF16),
        "gmlp_b_in": gmlp_b_in[0][None, :],
        "gmlp_ln_g": gmlp_ln_g[0][None, :],
        "gmlp_ln_b": gmlp_ln_b[0][None, :],
        "mix_full": (ws.astype(BF16), bias_full),
        "mix_short": (ws_short.astype(BF16), bias_short),
        "gmlp_w_out": gmlp_w_out[0].astype(BF16),
        "peer": [
            (peer_w_q[i].T.astype(BF16), peer_k1[i].astype(BF16), peer_k2[i].astype(BF16),
             peer_u[i].astype(BF16), peer_v[i].T.astype(BF16))
            for i in range(2)
        ],
    }

    zeros = jnp.zeros((bp, SSM_COLS), F32)
    y_p, hr_p, hi_p, _ = _trunk(x_prompt, zeros, zeros, w, scan_steps=32)
    y_s, hr_s, hi_s, v_s = _trunk(x_sample, state_ssm_re[0].reshape(bs, SSM_COLS),
                                  state_ssm_im[0].reshape(bs, SSM_COLS), w, scan_steps=ts)
    st = lambda a, b: a.reshape(1, b, SSM_GROUPS, SSM_STATE)
    return (y_p, y_s, st(hr_p, bp), st(hi_p, bp), st(hr_s, bs), st(hi_s, bs),
            v_s.reshape(1, bs, ts, GMLP_HALF))
```

```python
import functools
import math

import jax
import jax.numpy as jnp
from jax import lax
from jax.experimental import pallas as pl
from jax.experimental.pallas import tpu as pltpu

F32 = jnp.float32
BF16 = jnp.bfloat16

EPS = 1e-6
D_MODEL = 1024
SSM_GROUP = 16
SSM_GROUPS = 64
SSM_STATE = 64
SSM_COLS = SSM_GROUPS * SSM_STATE
GMLP_HALF = 2 * D_MODEL
GMLP_HEADS = 8
GMLP_HEAD_DIM = GMLP_HALF // GMLP_HEADS
CHUNK = 128
PEER_HEADS = 8
N_KEYS = 128
N_EXPERTS = N_KEYS * N_KEYS
PEER_QUERY = 256
PEER_HALF = 128
PEER_TOPK = 16

LANES = 128
SUBLANES = 8
MXU_DIM = 256
VMEM_LIMIT = 56 * 1024 * 1024

NEG_INF = float("-inf")
POS_INF = float("inf")
SQRT_HALF = math.sqrt(0.5)


def _dot(a, b):
    return jnp.dot(a, b, preferred_element_type=F32)


def _rms(x, g):
    ms = jnp.mean(x * x, axis=-1, keepdims=True)
    return x * lax.rsqrt(ms + EPS) * g


def _gelu(x):
    return 0.5 * x * (1.0 + lax.erf(x * SQRT_HALF))


def _params(sem, flags=None):
    return pltpu.CompilerParams(dimension_semantics=sem, vmem_limit_bytes=VMEM_LIMIT, flags=flags)


S5_ROW_TILE = 128
PERM_STEPS = MXU_DIM // SUBLANES


def _time_major_perm():
    r = jnp.arange(MXU_DIM)
    src = (r % SUBLANES) * PERM_STEPS + r // SUBLANES
    return (src[:, None] == r[None, :]).astype(BF16)


def _norm_matmul_kernel(x_ref, g_ref, p_ref, w_ref, o_ref, *, batch):
    tile = x_ref.shape[1]
    xn = [_rms(x_ref[b], g_ref[...]).astype(BF16) for b in range(batch)]
    if batch == 1:
        xp = xn[0]
    else:
        groups = []
        for tg in range(tile // PERM_STEPS):
            ts = slice(tg * PERM_STEPS, (tg + 1) * PERM_STEPS)
            rows_bt = jnp.concatenate([xn[b][ts] for b in range(batch)], axis=0)
            groups.append(_dot(p_ref[...], rows_bt).astype(BF16))
        xp = jnp.concatenate(groups, axis=0)
    o_ref[...] = _dot(xp, w_ref[...])


def _norm_matmul(x3d, g, perm, w_bf, *, tile):
    bsz, t, d = x3d.shape
    n = w_bf.shape[1]
    assert bsz == 1 or (bsz == SUBLANES and tile % PERM_STEPS == 0)
    return pl.pallas_call(
        functools.partial(_norm_matmul_kernel, batch=bsz),
        grid=(t // tile,),
        in_specs=[
            pl.BlockSpec((bsz, tile, d), lambda i: (0, i, 0)),
            pl.BlockSpec((1, d), lambda i: (0, 0)),
            pl.BlockSpec((MXU_DIM, MXU_DIM), lambda i: (0, 0)),
            pl.BlockSpec((d, n), lambda i: (0, 0)),
        ],
        out_specs=pl.BlockSpec((tile * bsz, n), lambda i: (i, 0)),
        out_shape=jax.ShapeDtypeStruct((t * bsz, n), F32),
        compiler_params=_params(("parallel",)),
        name="s5_in_proj",
    )(x3d, g, perm, w_bf)
SCAN_COLS = 512
N_SCAN_BLOCKS = SSM_COLS // SCAN_COLS
N_DIAG_BLOCKS = D_MODEL // MXU_DIM
DIAG_COLS = SSM_COLS // N_DIAG_BLOCKS
SCAN_PER_DIAG = DIAG_COLS // SCAN_COLS


def _s5_scan_kernel(u_ref, h0r_ref, h0i_ref, ar_ref, ai_ref, bbr_ref, bbi_ref, ccr_ref, cci_ref, d_ref,
                    g_ref, hr_ref, hi_ref, bur, bui, st_r, st_i, *, steps, batch):
    c = pl.program_id(0)
    n_sub = batch // SUBLANES

    @pl.when(c == 0)
    def _():
        for cb in range(N_SCAN_BLOCKS):
            st_r[cb] = h0r_ref[:, cb * SCAN_COLS:(cb + 1) * SCAN_COLS]
            st_i[cb] = h0i_ref[:, cb * SCAN_COLS:(cb + 1) * SCAN_COLS]

    u = u_ref[...]
    ub = u.astype(BF16)
    for kb in range(N_DIAG_BLOCKS):
        lhs = ub[:, kb * MXU_DIM:(kb + 1) * MXU_DIM]
        pr = _dot(lhs, bbr_ref[kb])
        pi = _dot(lhs, bbi_ref[kb])
        for j in range(SCAN_PER_DIAG):
            bur[kb * SCAN_PER_DIAG + j] = pr[:, j * SCAN_COLS:(j + 1) * SCAN_COLS]
            bui[kb * SCAN_PER_DIAG + j] = pi[:, j * SCAN_COLS:(j + 1) * SCAN_COLS]

    def scan_block(idx, carry):
        cb = idx // n_sub
        s = idx % n_sub
        ar = jnp.broadcast_to(ar_ref[cb], (SUBLANES, SCAN_COLS))
        ai = jnp.broadcast_to(ai_ref[cb], (SUBLANES, SCAN_COLS))
        row0 = pl.multiple_of(s * SUBLANES, SUBLANES)
        h_r = st_r[cb, pl.ds(row0, SUBLANES), :]
        h_i = st_i[cb, pl.ds(row0, SUBLANES), :]

        def step(t, h):
            hr, hi = h
            r = pl.multiple_of(t * batch + row0, SUBLANES)
            nr = ar * hr - ai * hi + bur[cb, pl.ds(r, SUBLANES), :]
            ni = ar * hi + ai * hr + bui[cb, pl.ds(r, SUBLANES), :]
            bur[cb, pl.ds(r, SUBLANES), :] = nr
            bui[cb, pl.ds(r, SUBLANES), :] = ni
            return nr, ni

        h_r, h_i = lax.fori_loop(0, steps, step, (h_r, h_i), unroll=min(steps, 8))
        st_r[cb, pl.ds(row0, SUBLANES), :] = h_r
        st_i[cb, pl.ds(row0, SUBLANES), :] = h_i
        return carry

    lax.fori_loop(0, N_SCAN_BLOCKS * n_sub, scan_block, 0)

    for cb in range(N_SCAN_BLOCKS):
        hr_ref[:, cb * SCAN_COLS:(cb + 1) * SCAN_COLS] = st_r[cb]
        hi_ref[:, cb * SCAN_COLS:(cb + 1) * SCAN_COLS] = st_i[cb]

    for nb in range(N_DIAG_BLOCKS):
        acc = None
        for j in range(SCAN_PER_DIAG):
            cb = nb * SCAN_PER_DIAG + j
            part = (_dot(bur[cb].astype(BF16), ccr_ref[nb, j * SCAN_COLS:(j + 1) * SCAN_COLS, :])
                    + _dot(bui[cb].astype(BF16), cci_ref[nb, j * SCAN_COLS:(j + 1) * SCAN_COLS, :]))
            acc = part if acc is None else acc + part
        cols = slice(nb * MXU_DIM, (nb + 1) * MXU_DIM)
        y = acc + d_ref[:, cols] * u[:, cols]
        g_ref[:, cols] = _gelu(y)


def _s5_scan(u_tm, h0_re, h0_im, disc, *, batch, steps):
    rows, d = u_tm.shape
    t = rows // batch
    n_chunks = t // steps
    blk = steps * batch
    ar, ai, bbr, bbi, ccr, cci, dsk = disc
    const2 = lambda c: (0, 0)
    const3 = lambda c: (0, 0, 0)
    kern = functools.partial(_s5_scan_kernel, steps=steps, batch=batch)
    return pl.pallas_call(
        kern,
        grid=(n_chunks,),
        in_specs=[
            pl.BlockSpec((blk, d), lambda c: (c, 0)),
            pl.BlockSpec((batch, SSM_COLS), const2),
            pl.BlockSpec((batch, SSM_COLS), const2),
            pl.BlockSpec((N_SCAN_BLOCKS, 1, SCAN_COLS), const3),
            pl.BlockSpec((N_SCAN_BLOCKS, 1, SCAN_COLS), const3),
            pl.BlockSpec((N_DIAG_BLOCKS, MXU_DIM, DIAG_COLS), const3),
            pl.BlockSpec((N_DIAG_BLOCKS, MXU_DIM, DIAG_COLS), const3),
            pl.BlockSpec((N_DIAG_BLOCKS, DIAG_COLS, MXU_DIM), const3),
            pl.BlockSpec((N_DIAG_BLOCKS, DIAG_COLS, MXU_DIM), const3),
            pl.BlockSpec((1, d), const2),
        ],
        out_specs=[
            pl.BlockSpec((blk, d), lambda c: (c, 0)),
            pl.BlockSpec((batch, SSM_COLS), const2),
            pl.BlockSpec((batch, SSM_COLS), const2),
        ],
        out_shape=[
            jax.ShapeDtypeStruct((rows, d), F32),
            jax.ShapeDtypeStruct((batch, SSM_COLS), F32),
            jax.ShapeDtypeStruct((batch, SSM_COLS), F32),
        ],
        scratch_shapes=[
            pltpu.VMEM((N_SCAN_BLOCKS, blk, SCAN_COLS), F32),
            pltpu.VMEM((N_SCAN_BLOCKS, blk, SCAN_COLS), F32),
            pltpu.VMEM((N_SCAN_BLOCKS, batch, SCAN_COLS), F32),
            pltpu.VMEM((N_SCAN_BLOCKS, batch, SCAN_COLS), F32),
        ],
        compiler_params=_params(("arbitrary",)),
        name="s5_scan",
    )(u_tm, h0_re, h0_im, ar, ai, bbr, bbi, ccr, cci, dsk)


def _glu_out_kernel(g_ref, x_ref, pt_ref, wg_ref, bg_ref, wo_ref, o_ref, *, batch):
    tile = x_ref.shape[1]
    g = g_ref[...]
    z = _dot(g.astype(BF16), wg_ref[...]) + bg_ref[...]
    o = (g * jax.nn.sigmoid(z)).astype(BF16)
    if batch == 1:
        o_ref[0] = x_ref[0] + _dot(o, wo_ref[...])
    else:
        n_groups = tile // PERM_STEPS
        groups = [_dot(pt_ref[...], o[tg * MXU_DIM:(tg + 1) * MXU_DIM]).astype(BF16) for tg in range(n_groups)]
        mix = _dot(jnp.concatenate(groups, axis=0), wo_ref[...])
        for tg in range(n_groups):
            for b in range(batch):
                ts = slice(tg * PERM_STEPS, (tg + 1) * PERM_STEPS)
                r0 = tg * MXU_DIM + b * PERM_STEPS
                o_ref[b, ts, :] = x_ref[b, ts, :] + mix[r0:r0 + PERM_STEPS]


def _glu_out(g_tm, x3d, perm_t, w_glu_bf, b_glu, w_out_bf, *, tile):
    bsz, t, d = x3d.shape
    assert bsz == 1 or (bsz == SUBLANES and tile % PERM_STEPS == 0)
    const2 = lambda i: (0, 0)
    return pl.pallas_call(
        functools.partial(_glu_out_kernel, batch=bsz),
        grid=(t // tile,),
        in_specs=[
            pl.BlockSpec((tile * bsz, d), lambda i: (i, 0)),
            pl.BlockSpec((bsz, tile, d), lambda i: (0, i, 0)),
            pl.BlockSpec((MXU_DIM, MXU_DIM), const2),
            pl.BlockSpec((d, d), const2),
            pl.BlockSpec((1, d), const2),
            pl.BlockSpec((d, d), const2),
        ],
        out_specs=pl.BlockSpec((bsz, tile, d), lambda i: (0, i, 0)),
        out_shape=jax.ShapeDtypeStruct((bsz, t, d), F32),
        compiler_params=_params(("parallel",)),
        name="s5_glu_out",
    )(g_tm, x3d, perm_t, w_glu_bf, b_glu, w_out_bf)


def _s5_discretize(a_re, a_im, log_dt, b_re, b_im, c_re, c_im, d_skip):
    dt = jnp.exp(log_dt)[:, None]
    mag = jnp.exp(a_re * dt)
    abar_re = mag * jnp.cos(a_im * dt)
    abar_im = mag * jnp.sin(a_im * dt)
    num_re = abar_re - 1.0
    num_im = abar_im
    den = a_re * a_re + a_im * a_im
    f_re = (num_re * a_re + num_im * a_im) / den
    f_im = (num_im * a_re - num_re * a_im) / den
    bb_re = f_re[..., None] * b_re - f_im[..., None] * b_im
    bb_im = f_re[..., None] * b_im + f_im[..., None] * b_re
    gpb = MXU_DIM // SSM_GROUP
    eye = jnp.eye(gpb, dtype=F32)

    def in_blocks(bb):
        bt = jnp.transpose(bb, (0, 2, 1)).reshape(N_DIAG_BLOCKS, gpb, SSM_GROUP, SSM_STATE)
        full = bt[:, :, :, None, :] * eye[None, :, None, :, None]
        return full.reshape(N_DIAG_BLOCKS, MXU_DIM, DIAG_COLS).astype(BF16)

    def out_blocks(cc):
        ct = jnp.transpose(cc, (0, 2, 1)).reshape(N_DIAG_BLOCKS, gpb, SSM_STATE, SSM_GROUP)
        full = ct[:, :, :, None, :] * eye[None, :, None, :, None]
        return full.reshape(N_DIAG_BLOCKS, DIAG_COLS, MXU_DIM).astype(BF16)

    ar = abar_re.reshape(N_SCAN_BLOCKS, 1, SCAN_COLS)
    ai = abar_im.reshape(N_SCAN_BLOCKS, 1, SCAN_COLS)
    return (ar, ai, in_blocks(bb_re), in_blocks(bb_im), out_blocks(c_re), out_blocks(-c_im),
            d_skip.reshape(1, D_MODEL))


def _gmlp_kernel(x_ref, g_ref, win_ref, bin_ref, lng_ref, lnb_ref, mix_ref, mixb_ref, wout_ref,
                 o_ref, v_ref, s_scr, *, rows):
    x = x_ref[...]
    hn = _rms(x, g_ref[...]).astype(BF16)
    z = _gelu(_dot(hn, win_ref[...]) + bin_ref[...])
    u = z[:, :GMLP_HALF]
    v = z[:, GMLP_HALF:]
    mu = jnp.mean(v, axis=-1, keepdims=True)
    vc = v - mu
    vn = vc * lax.rsqrt(jnp.mean(vc * vc, axis=-1, keepdims=True) + EPS) * lng_ref[...] + lnb_ref[...]
    v_ref[...] = vn
    vb = vn.astype(BF16)
    for r in range(rows // CHUNK):
        rs = slice(r * CHUNK, (r + 1) * CHUNK)
        for h in range(GMLP_HEADS):
            cs = slice(h * GMLP_HEAD_DIM, (h + 1) * GMLP_HEAD_DIM)
            mixed = _dot(mix_ref[h], vb[rs, cs]) + mixb_ref[h]
            s_scr[rs, cs] = (u[rs, cs] * mixed).astype(BF16)
    o_ref[...] = x + _dot(s_scr[...], wout_ref[...])


def _gmlp(x2d, g, w_in_bf, b_in, ln_g, ln_b, mix_bf, mix_bias, w_out_bf, *, tile):
    rows, d = x2d.shape
    const2 = lambda i: (0, 0)
    const3 = lambda i: (0, 0, 0)
    kern = functools.partial(_gmlp_kernel, rows=tile)
    return pl.pallas_call(
        kern,
        grid=(rows // tile,),
        in_specs=[
            pl.BlockSpec((tile, d), lambda i: (i, 0)),
            pl.BlockSpec((1, d), const2),
            pl.BlockSpec((d, 2 * GMLP_HALF), const2),
            pl.BlockSpec((1, 2 * GMLP_HALF), const2),
            pl.BlockSpec((1, GMLP_HALF), const2),
            pl.BlockSpec((1, GMLP_HALF), const2),
            pl.BlockSpec((GMLP_HEADS, CHUNK, CHUNK), const3),
            pl.BlockSpec((GMLP_HEADS, CHUNK, GMLP_HEAD_DIM), const3),
            pl.BlockSpec((GMLP_HALF, d), const2),
        ],
        out_specs=[
            pl.BlockSpec((tile, d), lambda i: (i, 0)),
            pl.BlockSpec((tile, GMLP_HALF), lambda i: (i, 0)),
        ],
        out_shape=[
            jax.ShapeDtypeStruct((rows, d), F32),
            jax.ShapeDtypeStruct((rows, GMLP_HALF), F32),
        ],
        scratch_shapes=[pltpu.VMEM((tile, GMLP_HALF), BF16)],
        compiler_params=_params(("parallel",)),
        name="gmlp",
    )(x2d, g, w_in_bf, b_in, ln_g, ln_b, mix_bf, mix_bias, w_out_bf)


KEY_PITCH = N_KEYS + SUBLANES


def _sorting_network(n):
    pairs = []
    p = 1
    while p < n:
        k = p
        while k >= 1:
            for j in range(k % p, n - k, 2 * k):
                for i in range(min(k, n - j - k)):
                    if (i + j) // (2 * p) == (i + j + k) // (2 * p):
                        pairs.append((i + j, i + j + k))
            k //= 2
        p *= 2
    return pairs


SORT16 = _sorting_network(PEER_TOPK)


def _compare_exchange(v, i, j):
    v[i], v[j] = jnp.maximum(v[i], v[j]), jnp.minimum(v[i], v[j])


def _sort16_desc(v):
    v = list(v)
    for i, j in SORT16:
        _compare_exchange(v, i, j)
    return v


def _merge_top16(a, b):
    c = [jnp.maximum(a[i], b[PEER_TOPK - 1 - i]) for i in range(PEER_TOPK)]
    d = PEER_TOPK // 2
    while d >= 1:
        for i in range(PEER_TOPK):
            if i & d == 0:
                _compare_exchange(c, i, i + d)
        d //= 2
    return c


def _top16_of(values):
    lists = [_sort16_desc(values[i:i + PEER_TOPK]) for i in range(0, len(values), PEER_TOPK)]
    while len(lists) > 1:
        merged = [_merge_top16(lists[i], lists[i + 1]) for i in range(0, len(lists) - 1, 2)]
        if len(lists) % 2:
            merged.append(lists[-1])
        lists = merged
    return lists[0]


def _peer_thresholds(s1_t, s2_t, e2_t, s1_v, s2_v, th_v, c1_v, grp):
    base = grp * (SUBLANES * KEY_PITCH)

    def key_rows(k):
        return pl.ds(base + k, SUBLANES, stride=KEY_PITCH)

    for k in range(N_KEYS):
        s1_v[k] = s1_t[key_rows(k), :]
        s2_v[k] = s2_t[key_rows(k), :]
    v1 = _top16_of([s1_v[k] for k in range(N_KEYS)])
    v2 = _top16_of([s2_v[k] for k in range(N_KEYS)])

    pairs = [(a, b) for a in range(PEER_TOPK) for b in range(PEER_TOPK // (a + 1))]
    cand = {ab: v1[ab[0]] + v2[ab[1]] for ab in pairs}
    first_row = [cand[(0, b)] for b in range(PEER_TOPK)]
    rest = [cand[ab] for ab in pairs if ab[0] > 0]
    rest += [jnp.full_like(v1[0], NEG_INF)] * (-len(rest) % PEER_TOPK)
    tau = _merge_top16(first_row, _top16_of(rest))[PEER_TOPK - 1]

    cmax = cand[(0, 0)]
    z = jnp.zeros_like(tau)
    th_rank = [jnp.full_like(tau, POS_INF) for _ in range(PEER_TOPK)]
    for a, b in pairs:
        sel = cand[(a, b)] >= tau
        z = z + jnp.where(sel, jnp.exp(cand[(a, b)] - cmax), 0.0)
        th_rank[a] = jnp.where(sel, v2[b], th_rank[a])
    c_scale = 0.5 / z

    def per_key(k, carry):
        s1k = s1_v[k]
        th = jnp.full_like(s1k, POS_INF)
        for a in reversed(range(PEER_TOPK)):
            th = jnp.where(s1k >= v1[a], th_rank[a], th)
        th_v[grp, k] = th
        c1_v[grp, k] = jnp.exp(s1k - v1[0]) * c_scale
        e2_t[key_rows(k), :] = jnp.exp(s2_v[k] - v2[0])
        return carry

    lax.fori_loop(0, N_KEYS, per_key, 0, unroll=4)


def _peer_kernel(x_ref, g_ref, gf_ref, wq_ref, k1_ref, k2_ref, u_ref, vt_ref, o_ref,
                 xn_scr, s1_t, s2_t, e2_t, s1_v, s2_v, th_v, c1_v, h_scr, w_scr, acc_scr,
                 *, tm, eb, final_norm):
    e = pl.program_id(1)
    n_e = pl.num_programs(1)
    n_tc = tm // LANES
    keys_per_block = eb // N_KEYS
    n_groups = PEER_HEADS * n_tc // SUBLANES

    def tile_row(h, tc):
        return (h * n_tc + tc) * KEY_PITCH

    @pl.when(e == 0)
    def prepare():
        xn_scr[...] = _rms(x_ref[...], g_ref[...]).T.astype(BF16)
        for h in range(PEER_HEADS):
            qt = _dot(wq_ref[h * PEER_QUERY:(h + 1) * PEER_QUERY, :], xn_scr[...])
            s1 = _dot(k1_ref[...], qt[:PEER_HALF].astype(BF16))
            s2 = _dot(k2_ref[...], qt[PEER_HALF:].astype(BF16))
            for tc in range(n_tc):
                rows = pl.ds(tile_row(h, tc), N_KEYS)
                s1_t[rows, :] = s1[:, tc * LANES:(tc + 1) * LANES]
                s2_t[rows, :] = s2[:, tc * LANES:(tc + 1) * LANES]

        def thresholds(grp, carry):
            _peer_thresholds(s1_t, s2_t, e2_t, s1_v, s2_v, th_v, c1_v, grp)
            return carry

        lax.fori_loop(0, n_groups, thresholds, 0)
        acc_scr[...] = jnp.zeros_like(acc_scr)

    h_scr[...] = _dot(u_ref[...], xn_scr[...])

    def key_row(kl, carry):
        i1 = e * keys_per_block + kl
        rows = pl.ds(pl.multiple_of(kl * N_KEYS, N_KEYS), N_KEYS)
        for tc in range(n_tc):
            lanes = slice(tc * LANES, (tc + 1) * LANES)
            gate = jnp.zeros((N_KEYS, LANES), F32)
            for h in range(PEER_HEADS):
                grp, j = divmod(h * n_tc + tc, SUBLANES)
                tile = pl.ds(tile_row(h, tc), N_KEYS)
                th = th_v[grp, i1, j:j + 1, :]
                c1 = c1_v[grp, i1, j:j + 1, :]
                gate = gate + jnp.where(s2_t[tile, :] >= th, c1 * e2_t[tile, :], 0.0)
            hv = h_scr[rows, lanes]
            w_scr[rows, lanes] = (gate * (hv * (1.0 + lax.erf(hv * SQRT_HALF)))).astype(BF16)
        return carry

    lax.fori_loop(0, keys_per_block, key_row, 0)
    acc_scr[...] += _dot(vt_ref[...], w_scr[...])

    @pl.when(e == n_e - 1)
    def _():
        y = x_ref[...] + acc_scr[...].T
        if final_norm:
            y = _rms(y, gf_ref[...])
        o_ref[...] = y


def _peer(x2d, g, g_final, wq_t_bf, k1_bf, k2_bf, u_bf, v_t_bf, *, final_norm, tm=512, eb=1024):
    n, d = x2d.shape
    n_tc = tm // LANES
    n_e = N_EXPERTS // eb
    const2 = lambda i, e: (0, 0)
    kern = functools.partial(_peer_kernel, tm=tm, eb=eb, final_norm=final_norm)
    n_tiles = PEER_HEADS * n_tc
    assert n_tiles % SUBLANES == 0
    tiles = (n_tiles * KEY_PITCH, LANES)
    per_key = (N_KEYS, SUBLANES, LANES)
    per_key_groups = (n_tiles // SUBLANES,) + per_key
    return pl.pallas_call(
        kern,
        grid=(n // tm, n_e),
        in_specs=[
            pl.BlockSpec((tm, d), lambda i, e: (i, 0)),
            pl.BlockSpec((1, d), const2),
            pl.BlockSpec((1, d), const2),
            pl.BlockSpec((PEER_HEADS * PEER_QUERY, d), const2),
            pl.BlockSpec((N_KEYS, PEER_HALF), const2),
            pl.BlockSpec((N_KEYS, PEER_HALF), const2),
            pl.BlockSpec((eb, d), lambda i, e: (e, 0)),
            pl.BlockSpec((d, eb), lambda i, e: (0, e)),
        ],
        out_specs=pl.BlockSpec((tm, d), lambda i, e: (i, 0)),
        out_shape=jax.ShapeDtypeStruct((n, d), F32),
        scratch_shapes=[
            pltpu.VMEM((d, tm), BF16),
            pltpu.VMEM(tiles, F32),
            pltpu.VMEM(tiles, F32),
            pltpu.VMEM(tiles, F32),
            pltpu.VMEM(per_key, F32),
            pltpu.VMEM(per_key, F32),
            pltpu.VMEM(per_key_groups, F32),
            pltpu.VMEM(per_key_groups, F32),
            pltpu.VMEM((eb, tm), F32),
            pltpu.VMEM((eb, tm), BF16),
            pltpu.VMEM((d, tm), F32),
        ],
        compiler_params=_params(("parallel", "arbitrary")),
        name="peer",
    )(x2d, g, g_final, wq_t_bf, k1_bf, k2_bf, u_bf, v_t_bf)


def _peer_tables_kernel(u_ref, v_ref, ub_ref, vt_ref):
    ub_ref[...] = u_ref[...].astype(BF16)
    vt_ref[0] = v_ref[0].T.astype(BF16)


def _peer_tables(peer_u, peer_v, *, tile=1024):
    n_layers, n_exp, d = peer_u.shape
    return pl.pallas_call(
        _peer_tables_kernel,
        grid=(n_layers, n_exp // tile),
        in_specs=[
            pl.BlockSpec((1, tile, d), lambda l, i: (l, i, 0)),
            pl.BlockSpec((1, tile, d), lambda l, i: (l, i, 0)),
        ],
        out_specs=[
            pl.BlockSpec((1, tile, d), lambda l, i: (l, i, 0)),
            pl.BlockSpec((1, d, tile), lambda l, i: (l, 0, i)),
        ],
        out_shape=[
            jax.ShapeDtypeStruct((n_layers, n_exp, d), BF16),
            jax.ShapeDtypeStruct((n_layers, d, n_exp), BF16),
        ],
        compiler_params=_params(("parallel", "parallel")),
        name="peer_tables",
    )(peer_u, peer_v)


def _trunk(x, h0_re, h0_im, w, *, scan_steps):
    bsz, t, d = x.shape
    n = bsz * t
    x2d = x.reshape(n, d)

    if bsz == SUBLANES and t % S5_ROW_TILE == 0:
        u_tm = _norm_matmul(x, w["norm_mix_g"][0], w["perm"], w["ssm_w_in"], tile=S5_ROW_TILE)
        g_tm, hr, hi = _s5_scan(u_tm, h0_re, h0_im, w["disc"], batch=bsz, steps=scan_steps)
        x2d = _glu_out(g_tm, x, w["perm"].T, w["ssm_w_glu"], w["ssm_b_glu"], w["ssm_w_out"],
                       tile=S5_ROW_TILE).reshape(n, d)
    else:
        u_bm = _norm_matmul(x2d[None], w["norm_mix_g"][0], w["perm"], w["ssm_w_in"], tile=n)
        u_tm = u_bm.reshape(bsz, t, d).transpose(1, 0, 2).reshape(n, d)
        g_tm, hr, hi = _s5_scan(u_tm, h0_re, h0_im, w["disc"], batch=bsz, steps=scan_steps)
        g_bm = g_tm.reshape(t, bsz, d).transpose(1, 0, 2).reshape(n, d)
        x2d = _glu_out(g_bm, x2d[None], w["perm"].T, w["ssm_w_glu"], w["ssm_b_glu"], w["ssm_w_out"],
                       tile=n).reshape(n, d)
    x2d = _peer(x2d, w["norm_ffn_g"][0], w["norm_final_g"], *w["peer"][0], final_norm=False)

    if t % CHUNK == 0:
        mix, mix_bias = w["mix_full"]
    else:
        mix, mix_bias = w["mix_short"]
    x2d, v = _gmlp(x2d, w["norm_mix_g"][1], w["gmlp_w_in"], w["gmlp_b_in"], w["gmlp_ln_g"], w["gmlp_ln_b"],
                   mix, mix_bias, w["gmlp_w_out"], tile=256)
    y2d = _peer(x2d, w["norm_ffn_g"][1], w["norm_final_g"], *w["peer"][1], final_norm=True)
    return y2d.reshape(bsz, t, d), hr, hi, v


def kernel(x_prompt, x_sample, state_ssm_re, state_ssm_im, norm_mix_g, norm_ffn_g, norm_final_g, ssm_w_in, ssm_a_re, ssm_a_im, ssm_log_dt, ssm_b_re, ssm_b_im, ssm_c_re, ssm_c_im, ssm_d, ssm_w_glu, ssm_b_glu, ssm_w_out, gmlp_w_in, gmlp_b_in, gmlp_ln_g, gmlp_ln_b, gmlp_w_s, gmlp_b_s, gmlp_w_out, peer_w_q, peer_k1, peer_k2, peer_u, peer_v):
    bp, tp, d = x_prompt.shape
    bs, ts, _ = x_sample.shape
    assert tp % CHUNK == 0 and CHUNK % ts == 0 and d == D_MODEL

    tril = jnp.tril(jnp.ones((CHUNK, CHUNK), F32))
    ws = gmlp_w_s[0] * tril
    bias_full = jnp.broadcast_to(gmlp_b_s[0][:, :, None], (GMLP_HEADS, CHUNK, GMLP_HEAD_DIM))
    reps = CHUNK // ts
    ws_short = jnp.einsum("ab,hij->haibj", jnp.eye(reps, dtype=F32), ws[:, :ts, :ts]).reshape(GMLP_HEADS, CHUNK, CHUNK)
    bias_short = jnp.broadcast_to(jnp.tile(gmlp_b_s[0][:, :ts], (1, reps))[:, :, None],
                                  (GMLP_HEADS, CHUNK, GMLP_HEAD_DIM))

    u_bf, v_t_bf = _peer_tables(peer_u, peer_v)
    w = {
        "norm_mix_g": norm_mix_g[:, None, :],
        "norm_ffn_g": norm_ffn_g[:, None, :],
        "norm_final_g": norm_final_g[None, :],
        "perm": _time_major_perm(),
        "ssm_w_in": ssm_w_in[0].astype(BF16),
        "disc": _s5_discretize(ssm_a_re[0], ssm_a_im[0], ssm_log_dt[0], ssm_b_re[0], ssm_b_im[0],
                               ssm_c_re[0], ssm_c_im[0], ssm_d[0]),
        "ssm_w_glu": ssm_w_glu[0].astype(BF16),
        "ssm_b_glu": ssm_b_glu[0][None, :],
        "ssm_w_out": ssm_w_out[0].astype(BF16),
        "gmlp_w_in": gmlp_w_in[0].astype(BF16),
        "gmlp_b_in": gmlp_b_in[0][None, :],
        "gmlp_ln_g": gmlp_ln_g[0][None, :],
        "gmlp_ln_b": gmlp_ln_b[0][None, :],
        "mix_full": (ws.astype(BF16), bias_full),
        "mix_short": (ws_short.astype(BF16), bias_short),
        "gmlp_w_out": gmlp_w_out[0].astype(BF16),
        "peer": [
            (peer_w_q[i].T.astype(BF16), peer_k1[i].astype(BF16), peer_k2[i].astype(BF16), u_bf[i], v_t_bf[i])
            for i in range(2)
        ],
    }

    zeros = jnp.zeros((bp, SSM_COLS), F32)
    y_p, hr_p, hi_p, _ = _trunk(x_prompt, zeros, zeros, w, scan_steps=32)
    y_s, hr_s, hi_s, v_s = _trunk(x_sample, state_ssm_re[0].reshape(bs, SSM_COLS),
                                  state_ssm_im[0].reshape(bs, SSM_COLS), w, scan_steps=ts)
    st = lambda a, b: a.reshape(1, b, SSM_GROUPS, SSM_STATE)
    return (y_p, y_s, st(hr_p, bp), st(hi_p, bp), st(hr_s, bs), st(hi_s, bs),
            v_s.reshape(1, bs, ts, GMLP_HALF))
```

```python
import functools
import math

import jax
import jax.numpy as jnp
from jax import lax
from jax.experimental import pallas as pl
from jax.experimental.pallas import tpu as pltpu

F32 = jnp.float32
BF16 = jnp.bfloat16

EPS = 1e-6
D_MODEL = 1024
SSM_GROUP = 16
SSM_GROUPS = 64
SSM_STATE = 64
SSM_COLS = SSM_GROUPS * SSM_STATE
GMLP_HALF = 2 * D_MODEL
GMLP_HEADS = 8
GMLP_HEAD_DIM = GMLP_HALF // GMLP_HEADS
CHUNK = 128
PEER_HEADS = 8
N_KEYS = 128
N_EXPERTS = N_KEYS * N_KEYS
PEER_QUERY = 256
PEER_HALF = 128
PEER_TOPK = 16

LANES = 128
SUBLANES = 8
MXU_DIM = 256
VMEM_LIMIT = 56 * 1024 * 1024

NEG_INF = float("-inf")
POS_INF = float("inf")
SQRT_HALF = math.sqrt(0.5)


def _dot(a, b):
    return jnp.dot(a, b, preferred_element_type=F32)


def _rms(x, g):
    ms = jnp.mean(x * x, axis=-1, keepdims=True)
    return x * lax.rsqrt(ms + EPS) * g


def _gelu(x):
    return 0.5 * x * (1.0 + lax.erf(x * SQRT_HALF))


def _params(sem, flags=None):
    return pltpu.CompilerParams(dimension_semantics=sem, vmem_limit_bytes=VMEM_LIMIT, flags=flags)


S5_ROW_TILE = 128
PERM_STEPS = MXU_DIM // SUBLANES


def _time_major_perm():
    r = jnp.arange(MXU_DIM)
    src = (r % SUBLANES) * PERM_STEPS + r // SUBLANES
    return (src[:, None] == r[None, :]).astype(BF16)


def _norm_matmul_kernel(x_ref, g_ref, p_ref, w_ref, o_ref, *, batch):
    tile = x_ref.shape[1]
    xn = [_rms(x_ref[b], g_ref[...]).astype(BF16) for b in range(batch)]
    if batch == 1:
        xp = xn[0]
    else:
        groups = []
        for tg in range(tile // PERM_STEPS):
            ts = slice(tg * PERM_STEPS, (tg + 1) * PERM_STEPS)
            rows_bt = jnp.concatenate([xn[b][ts] for b in range(batch)], axis=0)
            groups.append(_dot(p_ref[...], rows_bt).astype(BF16))
        xp = jnp.concatenate(groups, axis=0)
    o_ref[...] = _dot(xp, w_ref[...])


def _norm_matmul(x3d, g, perm, w_bf, *, tile):
    bsz, t, d = x3d.shape
    n = w_bf.shape[1]
    assert bsz == 1 or (bsz == SUBLANES and tile % PERM_STEPS == 0)
    return pl.pallas_call(
        functools.partial(_norm_matmul_kernel, batch=bsz),
        grid=(t // tile,),
        in_specs=[
            pl.BlockSpec((bsz, tile, d), lambda i: (0, i, 0)),
            pl.BlockSpec((1, d), lambda i: (0, 0)),
            pl.BlockSpec((MXU_DIM, MXU_DIM), lambda i: (0, 0)),
            pl.BlockSpec((d, n), lambda i: (0, 0)),
        ],
        out_specs=pl.BlockSpec((tile * bsz, n), lambda i: (i, 0)),
        out_shape=jax.ShapeDtypeStruct((t * bsz, n), F32),
        compiler_params=_params(("parallel",)),
        name="s5_in_proj",
    )(x3d, g, perm, w_bf)
SCAN_COLS = 512
N_SCAN_BLOCKS = SSM_COLS // SCAN_COLS
N_DIAG_BLOCKS = D_MODEL // MXU_DIM
DIAG_COLS = SSM_COLS // N_DIAG_BLOCKS
SCAN_PER_DIAG = DIAG_COLS // SCAN_COLS


def _s5_scan_kernel(u_ref, h0r_ref, h0i_ref, ar_ref, ai_ref, bbr_ref, bbi_ref, ccr_ref, cci_ref, d_ref,
                    g_ref, hr_ref, hi_ref, bur, bui, st_r, st_i, *, steps, batch):
    c = pl.program_id(0)
    n_sub = batch // SUBLANES

    @pl.when(c == 0)
    def _():
        for cb in range(N_SCAN_BLOCKS):
            st_r[cb] = h0r_ref[:, cb * SCAN_COLS:(cb + 1) * SCAN_COLS]
            st_i[cb] = h0i_ref[:, cb * SCAN_COLS:(cb + 1) * SCAN_COLS]

    u = u_ref[...]
    ub = u.astype(BF16)
    for kb in range(N_DIAG_BLOCKS):
        lhs = ub[:, kb * MXU_DIM:(kb + 1) * MXU_DIM]
        pr = _dot(lhs, bbr_ref[kb])
        pi = _dot(lhs, bbi_ref[kb])
        for j in range(SCAN_PER_DIAG):
            bur[kb * SCAN_PER_DIAG + j] = pr[:, j * SCAN_COLS:(j + 1) * SCAN_COLS]
            bui[kb * SCAN_PER_DIAG + j] = pi[:, j * SCAN_COLS:(j + 1) * SCAN_COLS]

    def scan_block(idx, carry):
        cb = idx // n_sub
        s = idx % n_sub
        ar = jnp.broadcast_to(ar_ref[cb], (SUBLANES, SCAN_COLS))
        ai = jnp.broadcast_to(ai_ref[cb], (SUBLANES, SCAN_COLS))
        row0 = pl.multiple_of(s * SUBLANES, SUBLANES)
        h_r = st_r[cb, pl.ds(row0, SUBLANES), :]
        h_i = st_i[cb, pl.ds(row0, SUBLANES), :]

        def step(t, h):
            hr, hi = h
            r = pl.multiple_of(t * batch + row0, SUBLANES)
            nr = ar * hr - ai * hi + bur[cb, pl.ds(r, SUBLANES), :]
            ni = ar * hi + ai * hr + bui[cb, pl.ds(r, SUBLANES), :]
            bur[cb, pl.ds(r, SUBLANES), :] = nr
            bui[cb, pl.ds(r, SUBLANES), :] = ni
            return nr, ni

        h_r, h_i = lax.fori_loop(0, steps, step, (h_r, h_i), unroll=min(steps, 8))
        st_r[cb, pl.ds(row0, SUBLANES), :] = h_r
        st_i[cb, pl.ds(row0, SUBLANES), :] = h_i
        return carry

    lax.fori_loop(0, N_SCAN_BLOCKS * n_sub, scan_block, 0)

    for cb in range(N_SCAN_BLOCKS):
        hr_ref[:, cb * SCAN_COLS:(cb + 1) * SCAN_COLS] = st_r[cb]
        hi_ref[:, cb * SCAN_COLS:(cb + 1) * SCAN_COLS] = st_i[cb]

    for nb in range(N_DIAG_BLOCKS):
        acc = None
        for j in range(SCAN_PER_DIAG):
            cb = nb * SCAN_PER_DIAG + j
            part = (_dot(bur[cb].astype(BF16), ccr_ref[nb, j * SCAN_COLS:(j + 1) * SCAN_COLS, :])
                    + _dot(bui[cb].astype(BF16), cci_ref[nb, j * SCAN_COLS:(j + 1) * SCAN_COLS, :]))
            acc = part if acc is None else acc + part
        cols = slice(nb * MXU_DIM, (nb + 1) * MXU_DIM)
        y = acc + d_ref[:, cols] * u[:, cols]
        g_ref[:, cols] = _gelu(y)


def _s5_scan(u_tm, h0_re, h0_im, disc, *, batch, steps):
    rows, d = u_tm.shape
    t = rows // batch
    n_chunks = t // steps
    blk = steps * batch
    ar, ai, bbr, bbi, ccr, cci, dsk = disc
    const2 = lambda c: (0, 0)
    const3 = lambda c: (0, 0, 0)
    kern = functools.partial(_s5_scan_kernel, steps=steps, batch=batch)
    return pl.pallas_call(
        kern,
        grid=(n_chunks,),
        in_specs=[
            pl.BlockSpec((blk, d), lambda c: (c, 0)),
            pl.BlockSpec((batch, SSM_COLS), const2),
            pl.BlockSpec((batch, SSM_COLS), const2),
            pl.BlockSpec((N_SCAN_BLOCKS, 1, SCAN_COLS), const3),
            pl.BlockSpec((N_SCAN_BLOCKS, 1, SCAN_COLS), const3),
            pl.BlockSpec((N_DIAG_BLOCKS, MXU_DIM, DIAG_COLS), const3),
            pl.BlockSpec((N_DIAG_BLOCKS, MXU_DIM, DIAG_COLS), const3),
            pl.BlockSpec((N_DIAG_BLOCKS, DIAG_COLS, MXU_DIM), const3),
            pl.BlockSpec((N_DIAG_BLOCKS, DIAG_COLS, MXU_DIM), const3),
            pl.BlockSpec((1, d), const2),
        ],
        out_specs=[
            pl.BlockSpec((blk, d), lambda c: (c, 0)),
            pl.BlockSpec((batch, SSM_COLS), const2),
            pl.BlockSpec((batch, SSM_COLS), const2),
        ],
        out_shape=[
            jax.ShapeDtypeStruct((rows, d), F32),
            jax.ShapeDtypeStruct((batch, SSM_COLS), F32),
            jax.ShapeDtypeStruct((batch, SSM_COLS), F32),
        ],
        scratch_shapes=[
            pltpu.VMEM((N_SCAN_BLOCKS, blk, SCAN_COLS), F32),
            pltpu.VMEM((N_SCAN_BLOCKS, blk, SCAN_COLS), F32),
            pltpu.VMEM((N_SCAN_BLOCKS, batch, SCAN_COLS), F32),
            pltpu.VMEM((N_SCAN_BLOCKS, batch, SCAN_COLS), F32),
        ],
        compiler_params=_params(("arbitrary",)),
        name="s5_scan",
    )(u_tm, h0_re, h0_im, ar, ai, bbr, bbi, ccr, cci, dsk)


def _glu_out_kernel(g_ref, x_ref, pt_ref, wg_ref, bg_ref, wo_ref, o_ref, *, batch):
    tile = x_ref.shape[1]
    g = g_ref[...]
    z = _dot(g.astype(BF16), wg_ref[...]) + bg_ref[...]
    o = (g * jax.nn.sigmoid(z)).astype(BF16)
    if batch == 1:
        o_ref[0] = x_ref[0] + _dot(o, wo_ref[...])
    else:
        n_groups = tile // PERM_STEPS
        groups = [_dot(pt_ref[...], o[tg * MXU_DIM:(tg + 1) * MXU_DIM]).astype(BF16) for tg in range(n_groups)]
        mix = _dot(jnp.concatenate(groups, axis=0), wo_ref[...])
        for tg in range(n_groups):
            for b in range(batch):
                ts = slice(tg * PERM_STEPS, (tg + 1) * PERM_STEPS)
                r0 = tg * MXU_DIM + b * PERM_STEPS
                o_ref[b, ts, :] = x_ref[b, ts, :] + mix[r0:r0 + PERM_STEPS]


def _glu_out(g_tm, x3d, perm_t, w_glu_bf, b_glu, w_out_bf, *, tile):
    bsz, t, d = x3d.shape
    assert bsz == 1 or (bsz == SUBLANES and tile % PERM_STEPS == 0)
    const2 = lambda i: (0, 0)
    return pl.pallas_call(
        functools.partial(_glu_out_kernel, batch=bsz),
        grid=(t // tile,),
        in_specs=[
            pl.BlockSpec((tile * bsz, d), lambda i: (i, 0)),
            pl.BlockSpec((bsz, tile, d), lambda i: (0, i, 0)),
            pl.BlockSpec((MXU_DIM, MXU_DIM), const2),
            pl.BlockSpec((d, d), const2),
            pl.BlockSpec((1, d), const2),
            pl.BlockSpec((d, d), const2),
        ],
        out_specs=pl.BlockSpec((bsz, tile, d), lambda i: (0, i, 0)),
        out_shape=jax.ShapeDtypeStruct((bsz, t, d), F32),
        compiler_params=_params(("parallel",)),
        name="s5_glu_out",
    )(g_tm, x3d, perm_t, w_glu_bf, b_glu, w_out_bf)


def _s5_discretize(a_re, a_im, log_dt, b_re, b_im, c_re, c_im, d_skip):
    dt = jnp.exp(log_dt)[:, None]
    mag = jnp.exp(a_re * dt)
    abar_re = mag * jnp.cos(a_im * dt)
    abar_im = mag * jnp.sin(a_im * dt)
    num_re = abar_re - 1.0
    num_im = abar_im
    den = a_re * a_re + a_im * a_im
    f_re = (num_re * a_re + num_im * a_im) / den
    f_im = (num_im * a_re - num_re * a_im) / den
    bb_re = f_re[..., None] * b_re - f_im[..., None] * b_im
    bb_im = f_re[..., None] * b_im + f_im[..., None] * b_re
    gpb = MXU_DIM // SSM_GROUP
    eye = jnp.eye(gpb, dtype=F32)

    def in_blocks(bb):
        bt = jnp.transpose(bb, (0, 2, 1)).reshape(N_DIAG_BLOCKS, gpb, SSM_GROUP, SSM_STATE)
        full = bt[:, :, :, None, :] * eye[None, :, None, :, None]
        return full.reshape(N_DIAG_BLOCKS, MXU_DIM, DIAG_COLS).astype(BF16)

    def out_blocks(cc):
        ct = jnp.transpose(cc, (0, 2, 1)).reshape(N_DIAG_BLOCKS, gpb, SSM_STATE, SSM_GROUP)
        full = ct[:, :, :, None, :] * eye[None, :, None, :, None]
        return full.reshape(N_DIAG_BLOCKS, DIAG_COLS, MXU_DIM).astype(BF16)

    ar = abar_re.reshape(N_SCAN_BLOCKS, 1, SCAN_COLS)
    ai = abar_im.reshape(N_SCAN_BLOCKS, 1, SCAN_COLS)
    return (ar, ai, in_blocks(bb_re), in_blocks(bb_im), out_blocks(c_re), out_blocks(-c_im),
            d_skip.reshape(1, D_MODEL))


def _gmlp_kernel(x_ref, g_ref, win_ref, bin_ref, lng_ref, lnb_ref, mix_ref, mixb_ref, wout_ref,
                 o_ref, v_ref, s_scr, *, rows):
    x = x_ref[...]
    hn = _rms(x, g_ref[...]).astype(BF16)
    z = _gelu(_dot(hn, win_ref[...]) + bin_ref[...])
    u = z[:, :GMLP_HALF]
    v = z[:, GMLP_HALF:]
    mu = jnp.mean(v, axis=-1, keepdims=True)
    vc = v - mu
    vn = vc * lax.rsqrt(jnp.mean(vc * vc, axis=-1, keepdims=True) + EPS) * lng_ref[...] + lnb_ref[...]
    v_ref[...] = vn
    vb = vn.astype(BF16)
    for r in range(rows // CHUNK):
        rs = slice(r * CHUNK, (r + 1) * CHUNK)
        for h in range(GMLP_HEADS):
            cs = slice(h * GMLP_HEAD_DIM, (h + 1) * GMLP_HEAD_DIM)
            mixed = _dot(mix_ref[h], vb[rs, cs]) + mixb_ref[h]
            s_scr[rs, cs] = (u[rs, cs] * mixed).astype(BF16)
    o_ref[...] = x + _dot(s_scr[...], wout_ref[...])


def _gmlp(x2d, g, w_in_bf, b_in, ln_g, ln_b, mix_bf, mix_bias, w_out_bf, *, tile):
    rows, d = x2d.shape
    const2 = lambda i: (0, 0)
    const3 = lambda i: (0, 0, 0)
    kern = functools.partial(_gmlp_kernel, rows=tile)
    return pl.pallas_call(
        kern,
        grid=(rows // tile,),
        in_specs=[
            pl.BlockSpec((tile, d), lambda i: (i, 0)),
            pl.BlockSpec((1, d), const2),
            pl.BlockSpec((d, 2 * GMLP_HALF), const2),
            pl.BlockSpec((1, 2 * GMLP_HALF), const2),
            pl.BlockSpec((1, GMLP_HALF), const2),
            pl.BlockSpec((1, GMLP_HALF), const2),
            pl.BlockSpec((GMLP_HEADS, CHUNK, CHUNK), const3),
            pl.BlockSpec((GMLP_HEADS, CHUNK, GMLP_HEAD_DIM), const3),
            pl.BlockSpec((GMLP_HALF, d), const2),
        ],
        out_specs=[
            pl.BlockSpec((tile, d), lambda i: (i, 0)),
            pl.BlockSpec((tile, GMLP_HALF), lambda i: (i, 0)),
        ],
        out_shape=[
            jax.ShapeDtypeStruct((rows, d), F32),
            jax.ShapeDtypeStruct((rows, GMLP_HALF), F32),
        ],
        scratch_shapes=[pltpu.VMEM((tile, GMLP_HALF), BF16)],
        compiler_params=_params(("parallel",)),
        name="gmlp",
    )(x2d, g, w_in_bf, b_in, ln_g, ln_b, mix_bf, mix_bias, w_out_bf)


KEY_PITCH = N_KEYS + SUBLANES


def _sorting_network(n):
    pairs = []
    p = 1
    while p < n:
        k = p
        while k >= 1:
            for j in range(k % p, n - k, 2 * k):
                for i in range(min(k, n - j - k)):
                    if (i + j) // (2 * p) == (i + j + k) // (2 * p):
                        pairs.append((i + j, i + j + k))
            k //= 2
        p *= 2
    return pairs


SORT16 = _sorting_network(PEER_TOPK)


def _compare_exchange(v, i, j):
    v[i], v[j] = jnp.maximum(v[i], v[j]), jnp.minimum(v[i], v[j])


def _sort16_desc(v):
    v = list(v)
    for i, j in SORT16:
        _compare_exchange(v, i, j)
    return v


def _merge_top16(a, b):
    c = [jnp.maximum(a[i], b[PEER_TOPK - 1 - i]) for i in range(PEER_TOPK)]
    d = PEER_TOPK // 2
    while d >= 1:
        for i in range(PEER_TOPK):
            if i & d == 0:
                _compare_exchange(c, i, i + d)
        d //= 2
    return c


def _top16_of(values):
    lists = [_sort16_desc(values[i:i + PEER_TOPK]) for i in range(0, len(values), PEER_TOPK)]
    while len(lists) > 1:
        merged = [_merge_top16(lists[i], lists[i + 1]) for i in range(0, len(lists) - 1, 2)]
        if len(lists) % 2:
            merged.append(lists[-1])
        lists = merged
    return lists[0]


def _peer_thresholds(s1_t, s2_t, e2_t, s1_v, s2_v, th_v, c1_v, grp):
    base = grp * (SUBLANES * KEY_PITCH)

    def key_rows(k):
        return pl.ds(base + k, SUBLANES, stride=KEY_PITCH)

    for k in range(N_KEYS):
        s1_v[k] = s1_t[key_rows(k), :]
        s2_v[k] = s2_t[key_rows(k), :]
    v1 = _top16_of([s1_v[k] for k in range(N_KEYS)])
    v2 = _top16_of([s2_v[k] for k in range(N_KEYS)])

    pairs = [(a, b) for a in range(PEER_TOPK) for b in range(PEER_TOPK // (a + 1))]
    cand = {ab: v1[ab[0]] + v2[ab[1]] for ab in pairs}
    first_row = [cand[(0, b)] for b in range(PEER_TOPK)]
    rest = [cand[ab] for ab in pairs if ab[0] > 0]
    rest += [jnp.full_like(v1[0], NEG_INF)] * (-len(rest) % PEER_TOPK)
    tau = _merge_top16(first_row, _top16_of(rest))[PEER_TOPK - 1]

    cmax = cand[(0, 0)]
    z = jnp.zeros_like(tau)
    th_rank = [jnp.full_like(tau, POS_INF) for _ in range(PEER_TOPK)]
    for a, b in pairs:
        sel = cand[(a, b)] >= tau
        z = z + jnp.where(sel, jnp.exp(cand[(a, b)] - cmax), 0.0)
        th_rank[a] = jnp.where(sel, v2[b], th_rank[a])
    c_scale = 0.5 / z

    def per_key(k, carry):
        s1k = s1_v[k]
        th = jnp.full_like(s1k, POS_INF)
        for a in reversed(range(PEER_TOPK)):
            th = jnp.where(s1k >= v1[a], th_rank[a], th)
        th_v[grp, k] = th
        c1_v[grp, k] = jnp.exp(s1k - v1[0]) * c_scale
        e2_t[key_rows(k), :] = jnp.exp(s2_v[k] - v2[0])
        return carry

    lax.fori_loop(0, N_KEYS, per_key, 0, unroll=4)


def _peer_kernel(x_ref, g_ref, gf_ref, wq_ref, k1_ref, k2_ref, u_ref, vt_ref, o_ref,
                 xn_scr, s1_t, s2_t, e2_t, s1_v, s2_v, th_v, c1_v, h_scr, w_scr, acc_scr,
                 *, tm, eb, final_norm):
    e = pl.program_id(1)
    n_e = pl.num_programs(1)
    n_tc = tm // LANES
    keys_per_block = eb // N_KEYS
    n_groups = PEER_HEADS * n_tc // SUBLANES

    def tile_row(h, tc):
        return (h * n_tc + tc) * KEY_PITCH

    @pl.when(e == 0)
    def prepare():
        xn_scr[...] = _rms(x_ref[...], g_ref[...]).T.astype(BF16)
        for h in range(PEER_HEADS):
            qt = _dot(wq_ref[h * PEER_QUERY:(h + 1) * PEER_QUERY, :], xn_scr[...])
            s1 = _dot(k1_ref[...], qt[:PEER_HALF].astype(BF16))
            s2 = _dot(k2_ref[...], qt[PEER_HALF:].astype(BF16))
            for tc in range(n_tc):
                rows = pl.ds(tile_row(h, tc), N_KEYS)
                s1_t[rows, :] = s1[:, tc * LANES:(tc + 1) * LANES]
                s2_t[rows, :] = s2[:, tc * LANES:(tc + 1) * LANES]

        def thresholds(grp, carry):
            _peer_thresholds(s1_t, s2_t, e2_t, s1_v, s2_v, th_v, c1_v, grp)
            return carry

        lax.fori_loop(0, n_groups, thresholds, 0)
        acc_scr[...] = jnp.zeros_like(acc_scr)

    h_scr[...] = _dot(u_ref[...], xn_scr[...])

    def key_row(kl, carry):
        i1 = e * keys_per_block + kl
        rows = pl.ds(pl.multiple_of(kl * N_KEYS, N_KEYS), N_KEYS)
        for tc in range(n_tc):
            lanes = slice(tc * LANES, (tc + 1) * LANES)
            gate = jnp.zeros((N_KEYS, LANES), F32)
            for h in range(PEER_HEADS):
                grp, j = divmod(h * n_tc + tc, SUBLANES)
                tile = pl.ds(tile_row(h, tc), N_KEYS)
                th = th_v[grp, i1, j:j + 1, :]
                c1 = c1_v[grp, i1, j:j + 1, :]
                gate = gate + jnp.where(s2_t[tile, :] >= th, c1 * e2_t[tile, :], 0.0)
            hv = h_scr[rows, lanes]
            w_scr[rows, lanes] = (gate * (hv * (1.0 + lax.erf(hv * SQRT_HALF)))).astype(BF16)
        return carry

    lax.fori_loop(0, keys_per_block, key_row, 0)
    acc_scr[...] += _dot(vt_ref[...], w_scr[...])

    @pl.when(e == n_e - 1)
    def _():
        y = x_ref[...] + acc_scr[...].T
        if final_norm:
            y = _rms(y, gf_ref[...])
        o_ref[...] = y


def _peer(x2d, g, g_final, wq_t_bf, k1_bf, k2_bf, u_bf, v_t_bf, *, layer, final_norm, tm=512, eb=2048):
    n, d = x2d.shape
    n_tc = tm // LANES
    n_e = N_EXPERTS // eb
    const2 = lambda i, e: (0, 0)
    kern = functools.partial(_peer_kernel, tm=tm, eb=eb, final_norm=final_norm)
    n_tiles = PEER_HEADS * n_tc
    assert n_tiles % SUBLANES == 0
    tiles = (n_tiles * KEY_PITCH, LANES)
    per_key = (N_KEYS, SUBLANES, LANES)
    per_key_groups = (n_tiles // SUBLANES,) + per_key
    return pl.pallas_call(
        kern,
        grid=(n // tm, n_e),
        in_specs=[
            pl.BlockSpec((tm, d), lambda i, e: (i, 0)),
            pl.BlockSpec((1, d), const2),
            pl.BlockSpec((1, d), const2),
            pl.BlockSpec((PEER_HEADS * PEER_QUERY, d), const2),
            pl.BlockSpec((N_KEYS, PEER_HALF), const2),
            pl.BlockSpec((N_KEYS, PEER_HALF), const2),
            pl.BlockSpec((None, eb, d), lambda i, e: (layer, e, 0)),
            pl.BlockSpec((None, d, eb), lambda i, e: (layer, 0, e)),
        ],
        out_specs=pl.BlockSpec((tm, d), lambda i, e: (i, 0)),
        out_shape=jax.ShapeDtypeStruct((n, d), F32),
        scratch_shapes=[
            pltpu.VMEM((d, tm), BF16),
            pltpu.VMEM(tiles, F32),
            pltpu.VMEM(tiles, F32),
            pltpu.VMEM(tiles, F32),
            pltpu.VMEM(per_key, F32),
            pltpu.VMEM(per_key, F32),
            pltpu.VMEM(per_key_groups, F32),
            pltpu.VMEM(per_key_groups, F32),
            pltpu.VMEM((eb, tm), F32),
            pltpu.VMEM((eb, tm), BF16),
            pltpu.VMEM((d, tm), F32),
        ],
        compiler_params=_params(("parallel", "arbitrary")),
        name="peer",
    )(x2d, g, g_final, wq_t_bf, k1_bf, k2_bf, u_bf, v_t_bf)


def _peer_tables_kernel(u_ref, v_ref, ub_ref, vt_ref):
    ub_ref[...] = u_ref[...].astype(BF16)
    vt_ref[0] = v_ref[0].T.astype(BF16)


def _peer_tables(peer_u, peer_v, *, tile=1024):
    n_layers, n_exp, d = peer_u.shape
    return pl.pallas_call(
        _peer_tables_kernel,
        grid=(n_layers, n_exp // tile),
        in_specs=[
            pl.BlockSpec((1, tile, d), lambda l, i: (l, i, 0)),
            pl.BlockSpec((1, tile, d), lambda l, i: (l, i, 0)),
        ],
        out_specs=[
            pl.BlockSpec((1, tile, d), lambda l, i: (l, i, 0)),
            pl.BlockSpec((1, d, tile), lambda l, i: (l, 0, i)),
        ],
        out_shape=[
            jax.ShapeDtypeStruct((n_layers, n_exp, d), BF16),
            jax.ShapeDtypeStruct((n_layers, d, n_exp), BF16),
        ],
        compiler_params=_params(("parallel", "parallel")),
        name="peer_tables",
    )(peer_u, peer_v)


def _trunk(x, h0_re, h0_im, w, *, scan_steps):
    bsz, t, d = x.shape
    n = bsz * t
    x2d = x.reshape(n, d)

    if bsz == SUBLANES and t % S5_ROW_TILE == 0:
        u_tm = _norm_matmul(x, w["norm_mix_g"][0], w["perm"], w["ssm_w_in"], tile=S5_ROW_TILE)
        g_tm, hr, hi = _s5_scan(u_tm, h0_re, h0_im, w["disc"], batch=bsz, steps=scan_steps)
        x2d = _glu_out(g_tm, x, w["perm"].T, w["ssm_w_glu"], w["ssm_b_glu"], w["ssm_w_out"],
                       tile=S5_ROW_TILE).reshape(n, d)
    else:
        u_bm = _norm_matmul(x2d[None], w["norm_mix_g"][0], w["perm"], w["ssm_w_in"], tile=n)
        u_tm = u_bm.reshape(bsz, t, d).transpose(1, 0, 2).reshape(n, d)
        g_tm, hr, hi = _s5_scan(u_tm, h0_re, h0_im, w["disc"], batch=bsz, steps=scan_steps)
        g_bm = g_tm.reshape(t, bsz, d).transpose(1, 0, 2).reshape(n, d)
        x2d = _glu_out(g_bm, x2d[None], w["perm"].T, w["ssm_w_glu"], w["ssm_b_glu"], w["ssm_w_out"],
                       tile=n).reshape(n, d)
    x2d = _peer(x2d, w["norm_ffn_g"][0], w["norm_final_g"], *w["peer"][0], *w["peer_tables"], layer=0,
                final_norm=False)

    if t % CHUNK == 0:
        mix, mix_bias = w["mix_full"]
    else:
        mix, mix_bias = w["mix_short"]
    x2d, v = _gmlp(x2d, w["norm_mix_g"][1], w["gmlp_w_in"], w["gmlp_b_in"], w["gmlp_ln_g"], w["gmlp_ln_b"],
                   mix, mix_bias, w["gmlp_w_out"], tile=512)
    y2d = _peer(x2d, w["norm_ffn_g"][1], w["norm_final_g"], *w["peer"][1], *w["peer_tables"], layer=1,
                final_norm=True)
    return y2d.reshape(bsz, t, d), hr, hi, v


def kernel(x_prompt, x_sample, state_ssm_re, state_ssm_im, norm_mix_g, norm_ffn_g, norm_final_g, ssm_w_in, ssm_a_re, ssm_a_im, ssm_log_dt, ssm_b_re, ssm_b_im, ssm_c_re, ssm_c_im, ssm_d, ssm_w_glu, ssm_b_glu, ssm_w_out, gmlp_w_in, gmlp_b_in, gmlp_ln_g, gmlp_ln_b, gmlp_w_s, gmlp_b_s, gmlp_w_out, peer_w_q, peer_k1, peer_k2, peer_u, peer_v):
    bp, tp, d = x_prompt.shape
    bs, ts, _ = x_sample.shape
    assert tp % CHUNK == 0 and CHUNK % ts == 0 and d == D_MODEL

    tril = jnp.tril(jnp.ones((CHUNK, CHUNK), F32))
    ws = gmlp_w_s[0] * tril
    bias_full = jnp.broadcast_to(gmlp_b_s[0][:, :, None], (GMLP_HEADS, CHUNK, GMLP_HEAD_DIM))
    reps = CHUNK // ts
    ws_short = jnp.einsum("ab,hij->haibj", jnp.eye(reps, dtype=F32), ws[:, :ts, :ts]).reshape(GMLP_HEADS, CHUNK, CHUNK)
    bias_short = jnp.broadcast_to(jnp.tile(gmlp_b_s[0][:, :ts], (1, reps))[:, :, None],
                                  (GMLP_HEADS, CHUNK, GMLP_HEAD_DIM))

    w = {
        "norm_mix_g": norm_mix_g[:, None, :],
        "norm_ffn_g": norm_ffn_g[:, None, :],
        "norm_final_g": norm_final_g[None, :],
        "perm": _time_major_perm(),
        "ssm_w_in": ssm_w_in[0].astype(BF16),
        "disc": _s5_discretize(ssm_a_re[0], ssm_a_im[0], ssm_log_dt[0], ssm_b_re[0], ssm_b_im[0],
                               ssm_c_re[0], ssm_c_im[0], ssm_d[0]),
        "ssm_w_glu": ssm_w_glu[0].astype(BF16),
        "ssm_b_glu": ssm_b_glu[0][None, :],
        "ssm_w_out": ssm_w_out[0].astype(BF16),
        "gmlp_w_in": gmlp_w_in[0].astype(BF16),
        "gmlp_b_in": gmlp_b_in[0][None, :],
        "gmlp_ln_g": gmlp_ln_g[0][None, :],
        "gmlp_ln_b": gmlp_ln_b[0][None, :],
        "mix_full": (ws.astype(BF16), bias_full),
        "mix_short": (ws_short.astype(BF16), bias_short),
        "gmlp_w_out": gmlp_w_out[0].astype(BF16),
        "peer": [
            (peer_w_q[i].T.astype(BF16), peer_k1[i].astype(BF16), peer_k2[i].astype(BF16)) for i in range(2)
        ],
        "peer_tables": _peer_tables(peer_u, peer_v),
    }

    zeros = jnp.zeros((bp, SSM_COLS), F32)
    y_p, hr_p, hi_p, _ = _trunk(x_prompt, zeros, zeros, w, scan_steps=64)
    y_s, hr_s, hi_s, v_s = _trunk(x_sample, state_ssm_re[0].reshape(bs, SSM_COLS),
                                  state_ssm_im[0].reshape(bs, SSM_COLS), w, scan_steps=ts)
    st = lambda a, b: a.reshape(1, b, SSM_GROUPS, SSM_STATE)
    return (y_p, y_s, st(hr_p, bp), st(hi_p, bp), st(hr_s, bs), st(hi_s, bs),
            v_s.reshape(1, bs, ts, GMLP_HALF))
```

```python
import functools
import math

import jax
import jax.numpy as jnp
from jax import lax
from jax.experimental import pallas as pl
from jax.experimental.pallas import tpu as pltpu

F32 = jnp.float32
BF16 = jnp.bfloat16

EPS = 1e-6
D_MODEL = 1024
SSM_GROUP = 16
SSM_GROUPS = 64
SSM_STATE = 64
SSM_COLS = SSM_GROUPS * SSM_STATE
GMLP_HALF = 2 * D_MODEL
GMLP_HEADS = 8
GMLP_HEAD_DIM = GMLP_HALF // GMLP_HEADS
CHUNK = 128
PEER_HEADS = 8
N_KEYS = 128
N_EXPERTS = N_KEYS * N_KEYS
PEER_QUERY = 256
PEER_HALF = 128
PEER_TOPK = 16

LANES = 128
SUBLANES = 8
MXU_DIM = 256
VMEM_LIMIT = 56 * 1024 * 1024

NEG_INF = float("-inf")
POS_INF = float("inf")
SQRT_HALF = math.sqrt(0.5)


def _dot(a, b):
    return jnp.dot(a, b, preferred_element_type=F32)


def _rms(x, g):
    ms = jnp.mean(x * x, axis=-1, keepdims=True)
    return x * lax.rsqrt(ms + EPS) * g


def _gelu(x):
    return 0.5 * x * (1.0 + lax.erf(x * SQRT_HALF))


def _params(sem, flags=None):
    return pltpu.CompilerParams(dimension_semantics=sem, vmem_limit_bytes=VMEM_LIMIT, flags=flags)


S5_ROW_TILE = 128
PERM_STEPS = MXU_DIM // SUBLANES


def _time_major_perm():
    r = jnp.arange(MXU_DIM)
    src = (r % SUBLANES) * PERM_STEPS + r // SUBLANES
    return (src[:, None] == r[None, :]).astype(BF16)


def _norm_matmul_kernel(x_ref, g_ref, p_ref, w_ref, o_ref, *, batch):
    tile = x_ref.shape[1]
    xn = [_rms(x_ref[b], g_ref[...]).astype(BF16) for b in range(batch)]
    if batch == 1:
        xp = xn[0]
    else:
        groups = []
        for tg in range(tile // PERM_STEPS):
            ts = slice(tg * PERM_STEPS, (tg + 1) * PERM_STEPS)
            rows_bt = jnp.concatenate([xn[b][ts] for b in range(batch)], axis=0)
            groups.append(_dot(p_ref[...], rows_bt).astype(BF16))
        xp = jnp.concatenate(groups, axis=0)
    o_ref[...] = _dot(xp, w_ref[...])


def _norm_matmul(x3d, g, perm, w_bf, *, tile):
    bsz, t, d = x3d.shape
    n = w_bf.shape[1]
    assert bsz == 1 or (bsz == SUBLANES and tile % PERM_STEPS == 0)
    return pl.pallas_call(
        functools.partial(_norm_matmul_kernel, batch=bsz),
        grid=(t // tile,),
        in_specs=[
            pl.BlockSpec((bsz, tile, d), lambda i: (0, i, 0)),
            pl.BlockSpec((1, d), lambda i: (0, 0)),
            pl.BlockSpec((MXU_DIM, MXU_DIM), lambda i: (0, 0)),
            pl.BlockSpec((d, n), lambda i: (0, 0)),
        ],
        out_specs=pl.BlockSpec((tile * bsz, n), lambda i: (i, 0)),
        out_shape=jax.ShapeDtypeStruct((t * bsz, n), F32),
        compiler_params=_params(("parallel",)),
        name="s5_in_proj",
    )(x3d, g, perm, w_bf)
SCAN_COLS = 512
N_SCAN_BLOCKS = SSM_COLS // SCAN_COLS
N_DIAG_BLOCKS = D_MODEL // MXU_DIM
DIAG_COLS = SSM_COLS // N_DIAG_BLOCKS
SCAN_PER_DIAG = DIAG_COLS // SCAN_COLS


def _s5_scan_kernel(u_ref, h0r_ref, h0i_ref, ar_ref, ai_ref, bbr_ref, bbi_ref, ccr_ref, cci_ref, d_ref,
                    g_ref, hr_ref, hi_ref, bur, bui, st_r, st_i, *, steps, batch):
    c = pl.program_id(0)
    n_sub = batch // SUBLANES

    @pl.when(c == 0)
    def _():
        for cb in range(N_SCAN_BLOCKS):
            st_r[cb] = h0r_ref[:, cb * SCAN_COLS:(cb + 1) * SCAN_COLS]
            st_i[cb] = h0i_ref[:, cb * SCAN_COLS:(cb + 1) * SCAN_COLS]

    u = u_ref[...]
    ub = u.astype(BF16)
    for kb in range(N_DIAG_BLOCKS):
        lhs = ub[:, kb * MXU_DIM:(kb + 1) * MXU_DIM]
        pr = _dot(lhs, bbr_ref[kb])
        pi = _dot(lhs, bbi_ref[kb])
        for j in range(SCAN_PER_DIAG):
            bur[kb * SCAN_PER_DIAG + j] = pr[:, j * SCAN_COLS:(j + 1) * SCAN_COLS]
            bui[kb * SCAN_PER_DIAG + j] = pi[:, j * SCAN_COLS:(j + 1) * SCAN_COLS]

    def scan_block(idx, carry):
        cb = idx // n_sub
        s = idx % n_sub
        ar = jnp.broadcast_to(ar_ref[cb], (SUBLANES, SCAN_COLS))
        ai = jnp.broadcast_to(ai_ref[cb], (SUBLANES, SCAN_COLS))
        row0 = pl.multiple_of(s * SUBLANES, SUBLANES)
        h_r = st_r[cb, pl.ds(row0, SUBLANES), :]
        h_i = st_i[cb, pl.ds(row0, SUBLANES), :]

        def step(t, h):
            hr, hi = h
            r = pl.multiple_of(t * batch + row0, SUBLANES)
            nr = ar * hr - ai * hi + bur[cb, pl.ds(r, SUBLANES), :]
            ni = ar * hi + ai * hr + bui[cb, pl.ds(r, SUBLANES), :]
            bur[cb, pl.ds(r, SUBLANES), :] = nr
            bui[cb, pl.ds(r, SUBLANES), :] = ni
            return nr, ni

        h_r, h_i = lax.fori_loop(0, steps, step, (h_r, h_i), unroll=min(steps, 8))
        st_r[cb, pl.ds(row0, SUBLANES), :] = h_r
        st_i[cb, pl.ds(row0, SUBLANES), :] = h_i
        return carry

    lax.fori_loop(0, N_SCAN_BLOCKS * n_sub, scan_block, 0)

    for cb in range(N_SCAN_BLOCKS):
        hr_ref[:, cb * SCAN_COLS:(cb + 1) * SCAN_COLS] = st_r[cb]
        hi_ref[:, cb * SCAN_COLS:(cb + 1) * SCAN_COLS] = st_i[cb]

    for nb in range(N_DIAG_BLOCKS):
        acc = None
        for j in range(SCAN_PER_DIAG):
            cb = nb * SCAN_PER_DIAG + j
            part = (_dot(bur[cb].astype(BF16), ccr_ref[nb, j * SCAN_COLS:(j + 1) * SCAN_COLS, :])
                    + _dot(bui[cb].astype(BF16), cci_ref[nb, j * SCAN_COLS:(j + 1) * SCAN_COLS, :]))
            acc = part if acc is None else acc + part
        cols = slice(nb * MXU_DIM, (nb + 1) * MXU_DIM)
        y = acc + d_ref[:, cols] * u[:, cols]
        g_ref[:, cols] = _gelu(y)


def _s5_scan(u_tm, h0_re, h0_im, disc, *, batch, steps):
    rows, d = u_tm.shape
    t = rows // batch
    n_chunks = t // steps
    blk = steps * batch
    ar, ai, bbr, bbi, ccr, cci, dsk = disc
    const2 = lambda c: (0, 0)
    const3 = lambda c: (0, 0, 0)
    kern = functools.partial(_s5_scan_kernel, steps=steps, batch=batch)
    return pl.pallas_call(
        kern,
        grid=(n_chunks,),
        in_specs=[
            pl.BlockSpec((blk, d), lambda c: (c, 0)),
            pl.BlockSpec((batch, SSM_COLS), const2),
            pl.BlockSpec((batch, SSM_COLS), const2),
            pl.BlockSpec((N_SCAN_BLOCKS, 1, SCAN_COLS), const3),
            pl.BlockSpec((N_SCAN_BLOCKS, 1, SCAN_COLS), const3),
            pl.BlockSpec((N_DIAG_BLOCKS, MXU_DIM, DIAG_COLS), const3),
            pl.BlockSpec((N_DIAG_BLOCKS, MXU_DIM, DIAG_COLS), const3),
            pl.BlockSpec((N_DIAG_BLOCKS, DIAG_COLS, MXU_DIM), const3),
            pl.BlockSpec((N_DIAG_BLOCKS, DIAG_COLS, MXU_DIM), const3),
            pl.BlockSpec((1, d), const2),
        ],
        out_specs=[
            pl.BlockSpec((blk, d), lambda c: (c, 0)),
            pl.BlockSpec((batch, SSM_COLS), const2),
            pl.BlockSpec((batch, SSM_COLS), const2),
        ],
        out_shape=[
            jax.ShapeDtypeStruct((rows, d), F32),
            jax.ShapeDtypeStruct((batch, SSM_COLS), F32),
            jax.ShapeDtypeStruct((batch, SSM_COLS), F32),
        ],
        scratch_shapes=[
            pltpu.VMEM((N_SCAN_BLOCKS, blk, SCAN_COLS), F32),
            pltpu.VMEM((N_SCAN_BLOCKS, blk, SCAN_COLS), F32),
            pltpu.VMEM((N_SCAN_BLOCKS, batch, SCAN_COLS), F32),
            pltpu.VMEM((N_SCAN_BLOCKS, batch, SCAN_COLS), F32),
        ],
        compiler_params=_params(("arbitrary",)),
        name="s5_scan",
    )(u_tm, h0_re, h0_im, ar, ai, bbr, bbi, ccr, cci, dsk)


def _glu_out_kernel(g_ref, x_ref, pt_ref, wg_ref, bg_ref, wo_ref, o_ref, *, batch):
    tile = x_ref.shape[1]
    g = g_ref[...]
    z = _dot(g.astype(BF16), wg_ref[...]) + bg_ref[...]
    o = (g * jax.nn.sigmoid(z)).astype(BF16)
    if batch == 1:
        o_ref[0] = x_ref[0] + _dot(o, wo_ref[...])
    else:
        n_groups = tile // PERM_STEPS
        groups = [_dot(pt_ref[...], o[tg * MXU_DIM:(tg + 1) * MXU_DIM]).astype(BF16) for tg in range(n_groups)]
        mix = _dot(jnp.concatenate(groups, axis=0), wo_ref[...])
        for tg in range(n_groups):
            for b in range(batch):
                ts = slice(tg * PERM_STEPS, (tg + 1) * PERM_STEPS)
                r0 = tg * MXU_DIM + b * PERM_STEPS
                o_ref[b, ts, :] = x_ref[b, ts, :] + mix[r0:r0 + PERM_STEPS]


def _glu_out(g_tm, x3d, perm_t, w_glu_bf, b_glu, w_out_bf, *, tile):
    bsz, t, d = x3d.shape
    assert bsz == 1 or (bsz == SUBLANES and tile % PERM_STEPS == 0)
    const2 = lambda i: (0, 0)
    return pl.pallas_call(
        functools.partial(_glu_out_kernel, batch=bsz),
        grid=(t // tile,),
        in_specs=[
            pl.BlockSpec((tile * bsz, d), lambda i: (i, 0)),
            pl.BlockSpec((bsz, tile, d), lambda i: (0, i, 0)),
            pl.BlockSpec((MXU_DIM, MXU_DIM), const2),
            pl.BlockSpec((d, d), const2),
            pl.BlockSpec((1, d), const2),
            pl.BlockSpec((d, d), const2),
        ],
        out_specs=pl.BlockSpec((bsz, tile, d), lambda i: (0, i, 0)),
        out_shape=jax.ShapeDtypeStruct((bsz, t, d), F32),
        compiler_params=_params(("parallel",)),
        name="s5_glu_out",
    )(g_tm, x3d, perm_t, w_glu_bf, b_glu, w_out_bf)


def _s5_discretize(a_re, a_im, log_dt, b_re, b_im, c_re, c_im, d_skip):
    dt = jnp.exp(log_dt)[:, None]
    mag = jnp.exp(a_re * dt)
    abar_re = mag * jnp.cos(a_im * dt)
    abar_im = mag * jnp.sin(a_im * dt)
    num_re = abar_re - 1.0
    num_im = abar_im
    den = a_re * a_re + a_im * a_im
    f_re = (num_re * a_re + num_im * a_im) / den
    f_im = (num_im * a_re - num_re * a_im) / den
    bb_re = f_re[..., None] * b_re - f_im[..., None] * b_im
    bb_im = f_re[..., None] * b_im + f_im[..., None] * b_re
    gpb = MXU_DIM // SSM_GROUP
    eye = jnp.eye(gpb, dtype=F32)

    def in_blocks(bb):
        bt = jnp.transpose(bb, (0, 2, 1)).reshape(N_DIAG_BLOCKS, gpb, SSM_GROUP, SSM_STATE)
        full = bt[:, :, :, None, :] * eye[None, :, None, :, None]
        return full.reshape(N_DIAG_BLOCKS, MXU_DIM, DIAG_COLS).astype(BF16)

    def out_blocks(cc):
        ct = jnp.transpose(cc, (0, 2, 1)).reshape(N_DIAG_BLOCKS, gpb, SSM_STATE, SSM_GROUP)
        full = ct[:, :, :, None, :] * eye[None, :, None, :, None]
        return full.reshape(N_DIAG_BLOCKS, DIAG_COLS, MXU_DIM).astype(BF16)

    ar = abar_re.reshape(N_SCAN_BLOCKS, 1, SCAN_COLS)
    ai = abar_im.reshape(N_SCAN_BLOCKS, 1, SCAN_COLS)
    return (ar, ai, in_blocks(bb_re), in_blocks(bb_im), out_blocks(c_re), out_blocks(-c_im),
            d_skip.reshape(1, D_MODEL))


def _gmlp_kernel(x_ref, g_ref, win_ref, bin_ref, lng_ref, lnb_ref, mix_ref, mixb_ref, wout_ref,
                 o_ref, v_ref, s_scr, *, rows):
    x = x_ref[...]
    hn = _rms(x, g_ref[...]).astype(BF16)
    z = _gelu(_dot(hn, win_ref[...]) + bin_ref[...])
    u = z[:, :GMLP_HALF]
    v = z[:, GMLP_HALF:]
    mu = jnp.mean(v, axis=-1, keepdims=True)
    vc = v - mu
    vn = vc * lax.rsqrt(jnp.mean(vc * vc, axis=-1, keepdims=True) + EPS) * lng_ref[...] + lnb_ref[...]
    v_ref[...] = vn
    vb = vn.astype(BF16)
    for r in range(rows // CHUNK):
        rs = slice(r * CHUNK, (r + 1) * CHUNK)
        for h in range(GMLP_HEADS):
            cs = slice(h * GMLP_HEAD_DIM, (h + 1) * GMLP_HEAD_DIM)
            mixed = _dot(mix_ref[h], vb[rs, cs]) + mixb_ref[h]
            s_scr[rs, cs] = (u[rs, cs] * mixed).astype(BF16)
    o_ref[...] = x + _dot(s_scr[...], wout_ref[...])


def _gmlp(x2d, g, w_in_bf, b_in, ln_g, ln_b, mix_bf, mix_bias, w_out_bf, *, tile):
    rows, d = x2d.shape
    const2 = lambda i: (0, 0)
    const3 = lambda i: (0, 0, 0)
    kern = functools.partial(_gmlp_kernel, rows=tile)
    return pl.pallas_call(
        kern,
        grid=(rows // tile,),
        in_specs=[
            pl.BlockSpec((tile, d), lambda i: (i, 0)),
            pl.BlockSpec((1, d), const2),
            pl.BlockSpec((d, 2 * GMLP_HALF), const2),
            pl.BlockSpec((1, 2 * GMLP_HALF), const2),
            pl.BlockSpec((1, GMLP_HALF), const2),
            pl.BlockSpec((1, GMLP_HALF), const2),
            pl.BlockSpec((GMLP_HEADS, CHUNK, CHUNK), const3),
            pl.BlockSpec((GMLP_HEADS, CHUNK, GMLP_HEAD_DIM), const3),
            pl.BlockSpec((GMLP_HALF, d), const2),
        ],
        out_specs=[
            pl.BlockSpec((tile, d), lambda i: (i, 0)),
            pl.BlockSpec((tile, GMLP_HALF), lambda i: (i, 0)),
        ],
        out_shape=[
            jax.ShapeDtypeStruct((rows, d), F32),
            jax.ShapeDtypeStruct((rows, GMLP_HALF), F32),
        ],
        scratch_shapes=[pltpu.VMEM((tile, GMLP_HALF), BF16)],
        compiler_params=_params(("parallel",)),
        name="gmlp",
    )(x2d, g, w_in_bf, b_in, ln_g, ln_b, mix_bf, mix_bias, w_out_bf)


KEY_PITCH = N_KEYS + SUBLANES


def _sorting_network(n):
    pairs = []
    p = 1
    while p < n:
        k = p
        while k >= 1:
            for j in range(k % p, n - k, 2 * k):
                for i in range(min(k, n - j - k)):
                    if (i + j) // (2 * p) == (i + j + k) // (2 * p):
                        pairs.append((i + j, i + j + k))
            k //= 2
        p *= 2
    return pairs


SORT16 = _sorting_network(PEER_TOPK)


def _compare_exchange(v, i, j):
    v[i], v[j] = jnp.maximum(v[i], v[j]), jnp.minimum(v[i], v[j])


def _sort16_desc(v):
    v = list(v)
    for i, j in SORT16:
        _compare_exchange(v, i, j)
    return v


def _merge_top16(a, b):
    c = [jnp.maximum(a[i], b[PEER_TOPK - 1 - i]) for i in range(PEER_TOPK)]
    d = PEER_TOPK // 2
    while d >= 1:
        for i in range(PEER_TOPK):
            if i & d == 0:
                _compare_exchange(c, i, i + d)
        d //= 2
    return c


def _top16_of(values):
    lists = [_sort16_desc(values[i:i + PEER_TOPK]) for i in range(0, len(values), PEER_TOPK)]
    while len(lists) > 1:
        merged = [_merge_top16(lists[i], lists[i + 1]) for i in range(0, len(lists) - 1, 2)]
        if len(lists) % 2:
            merged.append(lists[-1])
        lists = merged
    return lists[0]


def _peer_thresholds(s1_t, s2_t, e2_t, s1_v, s2_v, th_v, c1_v, grp):
    base = grp * (SUBLANES * KEY_PITCH)

    def key_rows(k):
        return pl.ds(base + k, SUBLANES, stride=KEY_PITCH)

    for k in range(N_KEYS):
        s1_v[k] = s1_t[key_rows(k), :]
        s2_v[k] = s2_t[key_rows(k), :]
    v1 = _top16_of([s1_v[k] for k in range(N_KEYS)])
    v2 = _top16_of([s2_v[k] for k in range(N_KEYS)])

    pairs = [(a, b) for a in range(PEER_TOPK) for b in range(PEER_TOPK // (a + 1))]
    cand = {ab: v1[ab[0]] + v2[ab[1]] for ab in pairs}
    first_row = [cand[(0, b)] for b in range(PEER_TOPK)]
    rest = [cand[ab] for ab in pairs if ab[0] > 0]
    rest += [jnp.full_like(v1[0], NEG_INF)] * (-len(rest) % PEER_TOPK)
    tau = _merge_top16(first_row, _top16_of(rest))[PEER_TOPK - 1]

    cmax = cand[(0, 0)]
    z = jnp.zeros_like(tau)
    th_rank = [jnp.full_like(tau, POS_INF) for _ in range(PEER_TOPK)]
    for a, b in pairs:
        sel = cand[(a, b)] >= tau
        z = z + jnp.where(sel, jnp.exp(cand[(a, b)] - cmax), 0.0)
        th_rank[a] = jnp.where(sel, v2[b], th_rank[a])
    c_scale = SQRT_HALF / z

    def per_key(k, carry):
        s1k = s1_v[k]
        th = jnp.full_like(s1k, POS_INF)
        for a in reversed(range(PEER_TOPK)):
            th = jnp.where(s1k >= v1[a], th_rank[a], th)
        th_v[grp, k] = th
        c1_v[grp, k] = jnp.exp(s1k - v1[0]) * c_scale
        e2_t[key_rows(k), :] = jnp.exp(s2_v[k] - v2[0])
        return carry

    lax.fori_loop(0, N_KEYS, per_key, 0, unroll=4)


def _peer_kernel(x_ref, g_ref, gf_ref, wq_ref, k1_ref, k2_ref, u_ref, vt_ref, o_ref,
                 xn_scr, xs_scr, s1_t, s2_t, e2_t, s1_v, s2_v, th_v, c1_v, h_scr, w_scr, acc_scr,
                 *, tm, eb, final_norm):
    e = pl.program_id(1)
    n_e = pl.num_programs(1)
    n_tc = tm // LANES
    keys_per_block = eb // N_KEYS
    n_groups = PEER_HEADS * n_tc // SUBLANES

    def tile_row(h, tc):
        return (h * n_tc + tc) * KEY_PITCH

    @pl.when(e == 0)
    def prepare():
        xn_t = _rms(x_ref[...], g_ref[...]).T
        xn_scr[...] = xn_t.astype(BF16)
        xs_scr[...] = (xn_t * SQRT_HALF).astype(BF16)
        for h in range(PEER_HEADS):
            qt = _dot(wq_ref[h * PEER_QUERY:(h + 1) * PEER_QUERY, :], xn_scr[...])
            s1 = _dot(k1_ref[...], qt[:PEER_HALF].astype(BF16))
            s2 = _dot(k2_ref[...], qt[PEER_HALF:].astype(BF16))
            for tc in range(n_tc):
                rows = pl.ds(tile_row(h, tc), N_KEYS)
                s1_t[rows, :] = s1[:, tc * LANES:(tc + 1) * LANES]
                s2_t[rows, :] = s2[:, tc * LANES:(tc + 1) * LANES]

        def thresholds(grp, carry):
            _peer_thresholds(s1_t, s2_t, e2_t, s1_v, s2_v, th_v, c1_v, grp)
            return carry

        lax.fori_loop(0, n_groups, thresholds, 0)
        acc_scr[...] = jnp.zeros_like(acc_scr)

    h_scr[...] = _dot(u_ref[...], xs_scr[...])

    def key_row(kl, carry):
        i1 = e * keys_per_block + kl
        rows = pl.ds(pl.multiple_of(kl * N_KEYS, N_KEYS), N_KEYS)
        for tc in range(n_tc):
            lanes = slice(tc * LANES, (tc + 1) * LANES)
            gate = jnp.zeros((N_KEYS, LANES), F32)
            for h in range(PEER_HEADS):
                grp, j = divmod(h * n_tc + tc, SUBLANES)
                tile = pl.ds(tile_row(h, tc), N_KEYS)
                th = th_v[grp, i1, j:j + 1, :]
                c1 = c1_v[grp, i1, j:j + 1, :]
                gate = gate + jnp.where(s2_t[tile, :] >= th, c1 * e2_t[tile, :], 0.0)
            hv = h_scr[rows, lanes]
            w_scr[rows, lanes] = (gate * (hv * (1.0 + lax.erf(hv)))).astype(BF16)
        return carry

    lax.fori_loop(0, keys_per_block, key_row, 0)
    acc_scr[...] += _dot(vt_ref[...], w_scr[...])

    @pl.when(e == n_e - 1)
    def _():
        y = x_ref[...] + acc_scr[...].T
        if final_norm:
            y = _rms(y, gf_ref[...])
        o_ref[...] = y


def _peer(x2d, g, g_final, wq_t_bf, k1_bf, k2_bf, u_bf, v_t_bf, *, layer, final_norm, tm=512, eb=2048):
    n, d = x2d.shape
    n_tc = tm // LANES
    n_e = N_EXPERTS // eb
    const2 = lambda i, e: (0, 0)
    kern = functools.partial(_peer_kernel, tm=tm, eb=eb, final_norm=final_norm)
    n_tiles = PEER_HEADS * n_tc
    assert n_tiles % SUBLANES == 0
    tiles = (n_tiles * KEY_PITCH, LANES)
    per_key = (N_KEYS, SUBLANES, LANES)
    per_key_groups = (n_tiles // SUBLANES,) + per_key
    return pl.pallas_call(
        kern,
        grid=(n // tm, n_e),
        in_specs=[
            pl.BlockSpec((tm, d), lambda i, e: (i, 0)),
            pl.BlockSpec((1, d), const2),
            pl.BlockSpec((1, d), const2),
            pl.BlockSpec((PEER_HEADS * PEER_QUERY, d), const2),
            pl.BlockSpec((N_KEYS, PEER_HALF), const2),
            pl.BlockSpec((N_KEYS, PEER_HALF), const2),
            pl.BlockSpec((None, eb, d), lambda i, e: (layer, e, 0)),
            pl.BlockSpec((None, d, eb), lambda i, e: (layer, 0, e)),
        ],
        out_specs=pl.BlockSpec((tm, d), lambda i, e: (i, 0)),
        out_shape=jax.ShapeDtypeStruct((n, d), F32),
        scratch_shapes=[
            pltpu.VMEM((d, tm), BF16),
            pltpu.VMEM((d, tm), BF16),
            pltpu.VMEM(tiles, F32),
            pltpu.VMEM(tiles, F32),
            pltpu.VMEM(tiles, F32),
            pltpu.VMEM(per_key, F32),
            pltpu.VMEM(per_key, F32),
            pltpu.VMEM(per_key_groups, F32),
            pltpu.VMEM(per_key_groups, F32),
            pltpu.VMEM((eb, tm), F32),
            pltpu.VMEM((eb, tm), BF16),
            pltpu.VMEM((d, tm), F32),
        ],
        compiler_params=_params(("parallel", "arbitrary")),
        name="peer",
    )(x2d, g, g_final, wq_t_bf, k1_bf, k2_bf, u_bf, v_t_bf)


def _peer_tables_kernel(u_ref, v_ref, ub_ref, vt_ref):
    ub_ref[...] = u_ref[...].astype(BF16)
    vt_ref[0] = v_ref[0].T.astype(BF16)


def _peer_tables(peer_u, peer_v, *, tile=1024):
    n_layers, n_exp, d = peer_u.shape
    return pl.pallas_call(
        _peer_tables_kernel,
        grid=(n_layers, n_exp // tile),
        in_specs=[
            pl.BlockSpec((1, tile, d), lambda l, i: (l, i, 0)),
            pl.BlockSpec((1, tile, d), lambda l, i: (l, i, 0)),
        ],
        out_specs=[
            pl.BlockSpec((1, tile, d), lambda l, i: (l, i, 0)),
            pl.BlockSpec((1, d, tile), lambda l, i: (l, 0, i)),
        ],
        out_shape=[
            jax.ShapeDtypeStruct((n_layers, n_exp, d), BF16),
            jax.ShapeDtypeStruct((n_layers, d, n_exp), BF16),
        ],
        compiler_params=_params(("parallel", "parallel")),
        name="peer_tables",
    )(peer_u, peer_v)


def _trunk(x, h0_re, h0_im, w, *, scan_steps):
    bsz, t, d = x.shape
    n = bsz * t
    x2d = x.reshape(n, d)

    if bsz == SUBLANES and t % S5_ROW_TILE == 0:
        u_tm = _norm_matmul(x, w["norm_mix_g"][0], w["perm"], w["ssm_w_in"], tile=S5_ROW_TILE)
        g_tm, hr, hi = _s5_scan(u_tm, h0_re, h0_im, w["disc"], batch=bsz, steps=scan_steps)
        x2d = _glu_out(g_tm, x, w["perm"].T, w["ssm_w_glu"], w["ssm_b_glu"], w["ssm_w_out"],
                       tile=S5_ROW_TILE).reshape(n, d)
    else:
        u_bm = _norm_matmul(x2d[None], w["norm_mix_g"][0], w["perm"], w["ssm_w_in"], tile=n)
        u_tm = u_bm.reshape(bsz, t, d).transpose(1, 0, 2).reshape(n, d)
        g_tm, hr, hi = _s5_scan(u_tm, h0_re, h0_im, w["disc"], batch=bsz, steps=scan_steps)
        g_bm = g_tm.reshape(t, bsz, d).transpose(1, 0, 2).reshape(n, d)
        x2d = _glu_out(g_bm, x2d[None], w["perm"].T, w["ssm_w_glu"], w["ssm_b_glu"], w["ssm_w_out"],
                       tile=n).reshape(n, d)
    x2d = _peer(x2d, w["norm_ffn_g"][0], w["norm_final_g"], *w["peer"][0], *w["peer_tables"], layer=0,
                final_norm=False)

    if t % CHUNK == 0:
        mix, mix_bias = w["mix_full"]
    else:
        mix, mix_bias = w["mix_short"]
    x2d, v = _gmlp(x2d, w["norm_mix_g"][1], w["gmlp_w_in"], w["gmlp_b_in"], w["gmlp_ln_g"], w["gmlp_ln_b"],
                   mix, mix_bias, w["gmlp_w_out"], tile=512)
    y2d = _peer(x2d, w["norm_ffn_g"][1], w["norm_final_g"], *w["peer"][1], *w["peer_tables"], layer=1,
                final_norm=True)
    return y2d.reshape(bsz, t, d), hr, hi, v


def kernel(x_prompt, x_sample, state_ssm_re, state_ssm_im, norm_mix_g, norm_ffn_g, norm_final_g, ssm_w_in, ssm_a_re, ssm_a_im, ssm_log_dt, ssm_b_re, ssm_b_im, ssm_c_re, ssm_c_im, ssm_d, ssm_w_glu, ssm_b_glu, ssm_w_out, gmlp_w_in, gmlp_b_in, gmlp_ln_g, gmlp_ln_b, gmlp_w_s, gmlp_b_s, gmlp_w_out, peer_w_q, peer_k1, peer_k2, peer_u, peer_v):
    bp, tp, d = x_prompt.shape
    bs, ts, _ = x_sample.shape
    assert tp % CHUNK == 0 and CHUNK % ts == 0 and d == D_MODEL

    tril = jnp.tril(jnp.ones((CHUNK, CHUNK), F32))
    ws = gmlp_w_s[0] * tril
    bias_full = jnp.broadcast_to(gmlp_b_s[0][:, :, None], (GMLP_HEADS, CHUNK, GMLP_HEAD_DIM))
    reps = CHUNK // ts
    ws_short = jnp.einsum("ab,hij->haibj", jnp.eye(reps, dtype=F32), ws[:, :ts, :ts]).reshape(GMLP_HEADS, CHUNK, CHUNK)
    bias_short = jnp.broadcast_to(jnp.tile(gmlp_b_s[0][:, :ts], (1, reps))[:, :, None],
                                  (GMLP_HEADS, CHUNK, GMLP_HEAD_DIM))

    w = {
        "norm_mix_g": norm_mix_g[:, None, :],
        "norm_ffn_g": norm_ffn_g[:, None, :],
        "norm_final_g": norm_final_g[None, :],
        "perm": _time_major_perm(),
        "ssm_w_in": ssm_w_in[0].astype(BF16),
        "disc": _s5_discretize(ssm_a_re[0], ssm_a_im[0], ssm_log_dt[0], ssm_b_re[0], ssm_b_im[0],
                               ssm_c_re[0], ssm_c_im[0], ssm_d[0]),
        "ssm_w_glu": ssm_w_glu[0].astype(BF16),
        "ssm_b_glu": ssm_b_glu[0][None, :],
        "ssm_w_out": ssm_w_out[0].astype(BF16),
        "gmlp_w_in": gmlp_w_in[0].astype(BF16),
        "gmlp_b_in": gmlp_b_in[0][None, :],
        "gmlp_ln_g": gmlp_ln_g[0][None, :],
        "gmlp_ln_b": gmlp_ln_b[0][None, :],
        "mix_full": (ws.astype(BF16), bias_full),
        "mix_short": (ws_short.astype(BF16), bias_short),
        "gmlp_w_out": gmlp_w_out[0].astype(BF16),
        "peer": [
            (peer_w_q[i].T.astype(BF16), peer_k1[i].astype(BF16), peer_k2[i].astype(BF16)) for i in range(2)
        ],
        "peer_tables": _peer_tables(peer_u, peer_v),
    }

    zeros = jnp.zeros((bp, SSM_COLS), F32)
    y_p, hr_p, hi_p, _ = _trunk(x_prompt, zeros, zeros, w, scan_steps=64)
    y_s, hr_s, hi_s, v_s = _trunk(x_sample, state_ssm_re[0].reshape(bs, SSM_COLS),
                                  state_ssm_im[0].reshape(bs, SSM_COLS), w, scan_steps=ts)
    st = lambda a, b: a.reshape(1, b, SSM_GROUPS, SSM_STATE)
    return (y_p, y_s, st(hr_p, bp), st(hi_p, bp), st(hr_s, bs), st(hi_s, bs),
            v_s.reshape(1, bs, ts, GMLP_HALF))
```

```python
import functools
import math

import jax
import jax.numpy as jnp
from jax import lax
from jax.experimental import pallas as pl
from jax.experimental.pallas import tpu as pltpu

F32 = jnp.float32
BF16 = jnp.bfloat16

EPS = 1e-6
D_MODEL = 1024
SSM_GROUP = 16
SSM_GROUPS = 64
SSM_STATE = 64
SSM_COLS = SSM_GROUPS * SSM_STATE
GMLP_HALF = 2 * D_MODEL
GMLP_HEADS = 8
GMLP_HEAD_DIM = GMLP_HALF // GMLP_HEADS
CHUNK = 128
PEER_HEADS = 8
N_KEYS = 128
N_EXPERTS = N_KEYS * N_KEYS
PEER_QUERY = 256
PEER_HALF = 128
PEER_TOPK = 16

LANES = 128
SUBLANES = 8
MXU_DIM = 256
VMEM_LIMIT = 56 * 1024 * 1024

NEG_INF = float("-inf")
POS_INF = float("inf")
SQRT_HALF = math.sqrt(0.5)


def _dot(a, b):
    return jnp.dot(a, b, preferred_element_type=F32)


def _rms(x, g):
    ms = jnp.mean(x * x, axis=-1, keepdims=True)
    return x * lax.rsqrt(ms + EPS) * g


def _gelu(x):
    return 0.5 * x * (1.0 + lax.erf(x * SQRT_HALF))


def _params(sem, flags=None):
    return pltpu.CompilerParams(dimension_semantics=sem, vmem_limit_bytes=VMEM_LIMIT, flags=flags)


S5_ROW_TILE = 128
PERM_STEPS = MXU_DIM // SUBLANES


def _time_major_perm():
    r = jnp.arange(MXU_DIM)
    src = (r % SUBLANES) * PERM_STEPS + r // SUBLANES
    return (src[:, None] == r[None, :]).astype(BF16)


def _norm_matmul_kernel(x_ref, g_ref, p_ref, w_ref, o_ref, *, batch):
    tile = x_ref.shape[1]
    xn = [_rms(x_ref[b], g_ref[...]).astype(BF16) for b in range(batch)]
    if batch == 1:
        xp = xn[0]
    else:
        groups = []
        for tg in range(tile // PERM_STEPS):
            ts = slice(tg * PERM_STEPS, (tg + 1) * PERM_STEPS)
            rows_bt = jnp.concatenate([xn[b][ts] for b in range(batch)], axis=0)
            groups.append(_dot(p_ref[...], rows_bt).astype(BF16))
        xp = jnp.concatenate(groups, axis=0)
    o_ref[...] = _dot(xp, w_ref[...])


def _norm_matmul(x3d, g, perm, w_bf, *, tile):
    bsz, t, d = x3d.shape
    n = w_bf.shape[1]
    assert bsz == 1 or (bsz == SUBLANES and tile % PERM_STEPS == 0)
    return pl.pallas_call(
        functools.partial(_norm_matmul_kernel, batch=bsz),
        grid=(t // tile,),
        in_specs=[
            pl.BlockSpec((bsz, tile, d), lambda i: (0, i, 0)),
            pl.BlockSpec((1, d), lambda i: (0, 0)),
            pl.BlockSpec((MXU_DIM, MXU_DIM), lambda i: (0, 0)),
            pl.BlockSpec((d, n), lambda i: (0, 0)),
        ],
        out_specs=pl.BlockSpec((tile * bsz, n), lambda i: (i, 0)),
        out_shape=jax.ShapeDtypeStruct((t * bsz, n), F32),
        compiler_params=_params(("parallel",)),
        name="s5_in_proj",
    )(x3d, g, perm, w_bf)
SCAN_COLS = 512
N_SCAN_BLOCKS = SSM_COLS // SCAN_COLS
N_DIAG_BLOCKS = D_MODEL // MXU_DIM
DIAG_COLS = SSM_COLS // N_DIAG_BLOCKS
SCAN_PER_DIAG = DIAG_COLS // SCAN_COLS


def _s5_scan_kernel(u_ref, h0r_ref, h0i_ref, ar_ref, ai_ref, bbr_ref, bbi_ref, ccr_ref, cci_ref, d_ref,
                    g_ref, hr_ref, hi_ref, bur, bui, st_r, st_i, *, steps, batch):
    c = pl.program_id(0)
    n_sub = batch // SUBLANES

    @pl.when(c == 0)
    def _():
        for cb in range(N_SCAN_BLOCKS):
            st_r[cb] = h0r_ref[:, cb * SCAN_COLS:(cb + 1) * SCAN_COLS]
            st_i[cb] = h0i_ref[:, cb * SCAN_COLS:(cb + 1) * SCAN_COLS]

    u = u_ref[...]
    ub = u.astype(BF16)
    for kb in range(N_DIAG_BLOCKS):
        lhs = ub[:, kb * MXU_DIM:(kb + 1) * MXU_DIM]
        pr = _dot(lhs, bbr_ref[kb])
        pi = _dot(lhs, bbi_ref[kb])
        for j in range(SCAN_PER_DIAG):
            bur[kb * SCAN_PER_DIAG + j] = pr[:, j * SCAN_COLS:(j + 1) * SCAN_COLS]
            bui[kb * SCAN_PER_DIAG + j] = pi[:, j * SCAN_COLS:(j + 1) * SCAN_COLS]

    def scan_block(idx, carry):
        cb = idx // n_sub
        s = idx % n_sub
        ar = jnp.broadcast_to(ar_ref[cb], (SUBLANES, SCAN_COLS))
        ai = jnp.broadcast_to(ai_ref[cb], (SUBLANES, SCAN_COLS))
        row0 = pl.multiple_of(s * SUBLANES, SUBLANES)
        h_r = st_r[cb, pl.ds(row0, SUBLANES), :]
        h_i = st_i[cb, pl.ds(row0, SUBLANES), :]

        def step(t, h):
            hr, hi = h
            r = pl.multiple_of(t * batch + row0, SUBLANES)
            nr = ar * hr - ai * hi + bur[cb, pl.ds(r, SUBLANES), :]
            ni = ar * hi + ai * hr + bui[cb, pl.ds(r, SUBLANES), :]
            bur[cb, pl.ds(r, SUBLANES), :] = nr
            bui[cb, pl.ds(r, SUBLANES), :] = ni
            return nr, ni

        h_r, h_i = lax.fori_loop(0, steps, step, (h_r, h_i), unroll=min(steps, 8))
        st_r[cb, pl.ds(row0, SUBLANES), :] = h_r
        st_i[cb, pl.ds(row0, SUBLANES), :] = h_i
        return carry

    lax.fori_loop(0, N_SCAN_BLOCKS * n_sub, scan_block, 0)

    for cb in range(N_SCAN_BLOCKS):
        hr_ref[:, cb * SCAN_COLS:(cb + 1) * SCAN_COLS] = st_r[cb]
        hi_ref[:, cb * SCAN_COLS:(cb + 1) * SCAN_COLS] = st_i[cb]

    for nb in range(N_DIAG_BLOCKS):
        acc = None
        for j in range(SCAN_PER_DIAG):
            cb = nb * SCAN_PER_DIAG + j
            part = (_dot(bur[cb].astype(BF16), ccr_ref[nb, j * SCAN_COLS:(j + 1) * SCAN_COLS, :])
                    + _dot(bui[cb].astype(BF16), cci_ref[nb, j * SCAN_COLS:(j + 1) * SCAN_COLS, :]))
            acc = part if acc is None else acc + part
        cols = slice(nb * MXU_DIM, (nb + 1) * MXU_DIM)
        y = acc + d_ref[:, cols] * u[:, cols]
        g_ref[:, cols] = _gelu(y)


def _s5_scan(u_tm, h0_re, h0_im, disc, *, batch, steps):
    rows, d = u_tm.shape
    t = rows // batch
    n_chunks = t // steps
    blk = steps * batch
    ar, ai, bbr, bbi, ccr, cci, dsk = disc
    const2 = lambda c: (0, 0)
    const3 = lambda c: (0, 0, 0)
    kern = functools.partial(_s5_scan_kernel, steps=steps, batch=batch)
    return pl.pallas_call(
        kern,
        grid=(n_chunks,),
        in_specs=[
            pl.BlockSpec((blk, d), lambda c: (c, 0)),
            pl.BlockSpec((batch, SSM_COLS), const2),
            pl.BlockSpec((batch, SSM_COLS), const2),
            pl.BlockSpec((N_SCAN_BLOCKS, 1, SCAN_COLS), const3),
            pl.BlockSpec((N_SCAN_BLOCKS, 1, SCAN_COLS), const3),
            pl.BlockSpec((N_DIAG_BLOCKS, MXU_DIM, DIAG_COLS), const3),
            pl.BlockSpec((N_DIAG_BLOCKS, MXU_DIM, DIAG_COLS), const3),
            pl.BlockSpec((N_DIAG_BLOCKS, DIAG_COLS, MXU_DIM), const3),
            pl.BlockSpec((N_DIAG_BLOCKS, DIAG_COLS, MXU_DIM), const3),
            pl.BlockSpec((1, d), const2),
        ],
        out_specs=[
            pl.BlockSpec((blk, d), lambda c: (c, 0)),
            pl.BlockSpec((batch, SSM_COLS), const2),
            pl.BlockSpec((batch, SSM_COLS), const2),
        ],
        out_shape=[
            jax.ShapeDtypeStruct((rows, d), F32),
            jax.ShapeDtypeStruct((batch, SSM_COLS), F32),
            jax.ShapeDtypeStruct((batch, SSM_COLS), F32),
        ],
        scratch_shapes=[
            pltpu.VMEM((N_SCAN_BLOCKS, blk, SCAN_COLS), F32),
            pltpu.VMEM((N_SCAN_BLOCKS, blk, SCAN_COLS), F32),
            pltpu.VMEM((N_SCAN_BLOCKS, batch, SCAN_COLS), F32),
            pltpu.VMEM((N_SCAN_BLOCKS, batch, SCAN_COLS), F32),
        ],
        compiler_params=_params(("arbitrary",)),
        name="s5_scan",
    )(u_tm, h0_re, h0_im, ar, ai, bbr, bbi, ccr, cci, dsk)


def _glu_out_kernel(g_ref, x_ref, pt_ref, wg_ref, bg_ref, wo_ref, o_ref, *, batch):
    tile = x_ref.shape[1]
    g = g_ref[...]
    z = _dot(g.astype(BF16), wg_ref[...]) + bg_ref[...]
    o = (g * jax.nn.sigmoid(z)).astype(BF16)
    if batch == 1:
        o_ref[0] = x_ref[0] + _dot(o, wo_ref[...])
    else:
        n_groups = tile // PERM_STEPS
        groups = [_dot(pt_ref[...], o[tg * MXU_DIM:(tg + 1) * MXU_DIM]).astype(BF16) for tg in range(n_groups)]
        mix = _dot(jnp.concatenate(groups, axis=0), wo_ref[...])
        for tg in range(n_groups):
            for b in range(batch):
                ts = slice(tg * PERM_STEPS, (tg + 1) * PERM_STEPS)
                r0 = tg * MXU_DIM + b * PERM_STEPS
                o_ref[b, ts, :] = x_ref[b, ts, :] + mix[r0:r0 + PERM_STEPS]


def _glu_out(g_tm, x3d, perm_t, w_glu_bf, b_glu, w_out_bf, *, tile):
    bsz, t, d = x3d.shape
    assert bsz == 1 or (bsz == SUBLANES and tile % PERM_STEPS == 0)
    const2 = lambda i: (0, 0)
    return pl.pallas_call(
        functools.partial(_glu_out_kernel, batch=bsz),
        grid=(t // tile,),
        in_specs=[
            pl.BlockSpec((tile * bsz, d), lambda i: (i, 0)),
            pl.BlockSpec((bsz, tile, d), lambda i: (0, i, 0)),
            pl.BlockSpec((MXU_DIM, MXU_DIM), const2),
            pl.BlockSpec((d, d), const2),
            pl.BlockSpec((1, d), const2),
            pl.BlockSpec((d, d), const2),
        ],
        out_specs=pl.BlockSpec((bsz, tile, d), lambda i: (0, i, 0)),
        out_shape=jax.ShapeDtypeStruct((bsz, t, d), F32),
        compiler_params=_params(("parallel",)),
        name="s5_glu_out",
    )(g_tm, x3d, perm_t, w_glu_bf, b_glu, w_out_bf)


def _s5_discretize(a_re, a_im, log_dt, b_re, b_im, c_re, c_im, d_skip):
    dt = jnp.exp(log_dt)[:, None]
    mag = jnp.exp(a_re * dt)
    abar_re = mag * jnp.cos(a_im * dt)
    abar_im = mag * jnp.sin(a_im * dt)
    num_re = abar_re - 1.0
    num_im = abar_im
    den = a_re * a_re + a_im * a_im
    f_re = (num_re * a_re + num_im * a_im) / den
    f_im = (num_im * a_re - num_re * a_im) / den
    bb_re = f_re[..., None] * b_re - f_im[..., None] * b_im
    bb_im = f_re[..., None] * b_im + f_im[..., None] * b_re
    gpb = MXU_DIM // SSM_GROUP
    eye = jnp.eye(gpb, dtype=F32)

    def in_blocks(bb):
        bt = jnp.transpose(bb, (0, 2, 1)).reshape(N_DIAG_BLOCKS, gpb, SSM_GROUP, SSM_STATE)
        full = bt[:, :, :, None, :] * eye[None, :, None, :, None]
        return full.reshape(N_DIAG_BLOCKS, MXU_DIM, DIAG_COLS).astype(BF16)

    def out_blocks(cc):
        ct = jnp.transpose(cc, (0, 2, 1)).reshape(N_DIAG_BLOCKS, gpb, SSM_STATE, SSM_GROUP)
        full = ct[:, :, :, None, :] * eye[None, :, None, :, None]
        return full.reshape(N_DIAG_BLOCKS, DIAG_COLS, MXU_DIM).astype(BF16)

    ar = abar_re.reshape(N_SCAN_BLOCKS, 1, SCAN_COLS)
    ai = abar_im.reshape(N_SCAN_BLOCKS, 1, SCAN_COLS)
    return (ar, ai, in_blocks(bb_re), in_blocks(bb_im), out_blocks(c_re), out_blocks(-c_im),
            d_skip.reshape(1, D_MODEL))


def _gmlp_kernel(x_ref, g_ref, win_ref, bin_ref, lng_ref, lnb_ref, mix_ref, mixb_ref, wout_ref,
                 o_ref, v_ref, s_scr, *, rows):
    x = x_ref[...]
    hn = _rms(x, g_ref[...]).astype(BF16)
    z = _gelu(_dot(hn, win_ref[...]) + bin_ref[...])
    u = z[:, :GMLP_HALF]
    v = z[:, GMLP_HALF:]
    mu = jnp.mean(v, axis=-1, keepdims=True)
    vc = v - mu
    vn = vc * lax.rsqrt(jnp.mean(vc * vc, axis=-1, keepdims=True) + EPS) * lng_ref[...] + lnb_ref[...]
    v_ref[...] = vn
    vb = vn.astype(BF16)
    for r in range(rows // CHUNK):
        rs = slice(r * CHUNK, (r + 1) * CHUNK)
        for h in range(GMLP_HEADS):
            cs = slice(h * GMLP_HEAD_DIM, (h + 1) * GMLP_HEAD_DIM)
            mixed = _dot(mix_ref[h], vb[rs, cs]) + mixb_ref[h]
            s_scr[rs, cs] = (u[rs, cs] * mixed).astype(BF16)
    o_ref[...] = x + _dot(s_scr[...], wout_ref[...])


def _gmlp(x2d, g, w_in_bf, b_in, ln_g, ln_b, mix_bf, mix_bias, w_out_bf, *, tile):
    rows, d = x2d.shape
    const2 = lambda i: (0, 0)
    const3 = lambda i: (0, 0, 0)
    kern = functools.partial(_gmlp_kernel, rows=tile)
    return pl.pallas_call(
        kern,
        grid=(rows // tile,),
        in_specs=[
            pl.BlockSpec((tile, d), lambda i: (i, 0)),
            pl.BlockSpec((1, d), const2),
            pl.BlockSpec((d, 2 * GMLP_HALF), const2),
            pl.BlockSpec((1, 2 * GMLP_HALF), const2),
            pl.BlockSpec((1, GMLP_HALF), const2),
            pl.BlockSpec((1, GMLP_HALF), const2),
            pl.BlockSpec((GMLP_HEADS, CHUNK, CHUNK), const3),
            pl.BlockSpec((GMLP_HEADS, CHUNK, GMLP_HEAD_DIM), const3),
            pl.BlockSpec((GMLP_HALF, d), const2),
        ],
        out_specs=[
            pl.BlockSpec((tile, d), lambda i: (i, 0)),
            pl.BlockSpec((tile, GMLP_HALF), lambda i: (i, 0)),
        ],
        out_shape=[
            jax.ShapeDtypeStruct((rows, d), F32),
            jax.ShapeDtypeStruct((rows, GMLP_HALF), F32),
        ],
        scratch_shapes=[pltpu.VMEM((tile, GMLP_HALF), BF16)],
        compiler_params=_params(("parallel",)),
        name="gmlp",
    )(x2d, g, w_in_bf, b_in, ln_g, ln_b, mix_bf, mix_bias, w_out_bf)


KEY_PITCH = N_KEYS + SUBLANES


def _sorting_network(n):
    pairs = []
    p = 1
    while p < n:
        k = p
        while k >= 1:
            for j in range(k % p, n - k, 2 * k):
                for i in range(min(k, n - j - k)):
                    if (i + j) // (2 * p) == (i + j + k) // (2 * p):
                        pairs.append((i + j, i + j + k))
            k //= 2
        p *= 2
    return pairs


SORT16 = _sorting_network(PEER_TOPK)


def _compare_exchange(v, i, j):
    v[i], v[j] = jnp.maximum(v[i], v[j]), jnp.minimum(v[i], v[j])


def _sort16_desc(v):
    v = list(v)
    for i, j in SORT16:
        _compare_exchange(v, i, j)
    return v


def _merge_top16(a, b):
    c = [jnp.maximum(a[i], b[PEER_TOPK - 1 - i]) for i in range(PEER_TOPK)]
    d = PEER_TOPK // 2
    while d >= 1:
        for i in range(PEER_TOPK):
            if i & d == 0:
                _compare_exchange(c, i, i + d)
        d //= 2
    return c


def _top16_of(values):
    lists = [_sort16_desc(values[i:i + PEER_TOPK]) for i in range(0, len(values), PEER_TOPK)]
    while len(lists) > 1:
        merged = [_merge_top16(lists[i], lists[i + 1]) for i in range(0, len(lists) - 1, 2)]
        if len(lists) % 2:
            merged.append(lists[-1])
        lists = merged
    return lists[0]


def _peer_thresholds(s1_t, s2_t, e2_t, s1_v, s2_v, th_v, c1_v, grp):
    base = grp * (SUBLANES * KEY_PITCH)

    def key_rows(k):
        return pl.ds(base + k, SUBLANES, stride=KEY_PITCH)

    for k in range(N_KEYS):
        s1_v[k] = s1_t[key_rows(k), :]
        s2_v[k] = s2_t[key_rows(k), :]
    v1 = _top16_of([s1_v[k] for k in range(N_KEYS)])
    v2 = _top16_of([s2_v[k] for k in range(N_KEYS)])

    pairs = [(a, b) for a in range(PEER_TOPK) for b in range(PEER_TOPK // (a + 1))]
    cand = {ab: v1[ab[0]] + v2[ab[1]] for ab in pairs}
    first_row = [cand[(0, b)] for b in range(PEER_TOPK)]
    rest = [cand[ab] for ab in pairs if ab[0] > 0]
    rest += [jnp.full_like(v1[0], NEG_INF)] * (-len(rest) % PEER_TOPK)
    tau = _merge_top16(first_row, _top16_of(rest))[PEER_TOPK - 1]

    cmax = cand[(0, 0)]
    z = jnp.zeros_like(tau)
    th_rank = [jnp.full_like(tau, POS_INF) for _ in range(PEER_TOPK)]
    for a, b in pairs:
        sel = cand[(a, b)] >= tau
        z = z + jnp.where(sel, jnp.exp(cand[(a, b)] - cmax), 0.0)
        th_rank[a] = jnp.where(sel, v2[b], th_rank[a])
    c_scale = SQRT_HALF / z

    def per_key(k, carry):
        s1k = s1_v[k]
        th = jnp.full_like(s1k, POS_INF)
        for a in reversed(range(PEER_TOPK)):
            th = jnp.where(s1k >= v1[a], th_rank[a], th)
        th_v[grp, k] = th
        c1_v[grp, k] = jnp.exp(s1k - v1[0]) * c_scale
        e2_t[key_rows(k), :] = jnp.exp(s2_v[k] - v2[0])
        return carry

    lax.fori_loop(0, N_KEYS, per_key, 0, unroll=4)


def _peer_kernel(x_ref, g_ref, gf_ref, wq_ref, k1_ref, k2_ref, u_ref, vt_ref, o_ref,
                 xn_scr, xs_scr, s1_t, s2_t, e2_t, s1_v, s2_v, th_v, c1_v, h_scr, w_scr, acc_scr,
                 *, tm, eb, final_norm):
    e = pl.program_id(1)
    n_e = pl.num_programs(1)
    n_tc = tm // LANES
    keys_per_block = eb // N_KEYS
    n_groups = PEER_HEADS * n_tc // SUBLANES

    def tile_row(h, tc):
        return (h * n_tc + tc) * KEY_PITCH

    @pl.when(e == 0)
    def prepare():
        xn_t = _rms(x_ref[...], g_ref[...]).T
        xn_scr[...] = xn_t.astype(BF16)
        xs_scr[...] = (xn_t * SQRT_HALF).astype(BF16)
        for h in range(PEER_HEADS):
            qt = _dot(wq_ref[h * PEER_QUERY:(h + 1) * PEER_QUERY, :], xn_scr[...])
            s1 = _dot(k1_ref[...], qt[:PEER_HALF].astype(BF16))
            s2 = _dot(k2_ref[...], qt[PEER_HALF:].astype(BF16))
            for tc in range(n_tc):
                rows = pl.ds(tile_row(h, tc), N_KEYS)
                s1_t[rows, :] = s1[:, tc * LANES:(tc + 1) * LANES]
                s2_t[rows, :] = s2[:, tc * LANES:(tc + 1) * LANES]

        def thresholds(grp, carry):
            _peer_thresholds(s1_t, s2_t, e2_t, s1_v, s2_v, th_v, c1_v, grp)
            return carry

        lax.fori_loop(0, n_groups, thresholds, 0)
        acc_scr[...] = jnp.zeros_like(acc_scr)

    hs = _dot(u_ref[...], xs_scr[...])
    gel = hs * (1.0 + lax.erf(hs))
    for kl in range(keys_per_block):
        i1 = e * keys_per_block + kl
        rows = slice(kl * N_KEYS, (kl + 1) * N_KEYS)
        for tc in range(n_tc):
            lanes = slice(tc * LANES, (tc + 1) * LANES)
            gate = jnp.zeros((N_KEYS, LANES), F32)
            for h in range(PEER_HEADS):
                grp, j = divmod(h * n_tc + tc, SUBLANES)
                tile = pl.ds(tile_row(h, tc), N_KEYS)
                th = th_v[grp, i1, j:j + 1, :]
                c1 = c1_v[grp, i1, j:j + 1, :]
                gate = gate + jnp.where(s2_t[tile, :] >= th, c1 * e2_t[tile, :], 0.0)
            w_scr[rows, lanes] = (gate * gel[rows, lanes]).astype(BF16)
    acc_scr[...] += _dot(vt_ref[...], w_scr[...])

    @pl.when(e == n_e - 1)
    def _():
        y = x_ref[...] + acc_scr[...].T
        if final_norm:
            y = _rms(y, gf_ref[...])
        o_ref[...] = y


def _peer(x2d, g, g_final, wq_t_bf, k1_bf, k2_bf, u_bf, v_t_bf, *, layer, final_norm, tm=512, eb=2048):
    n, d = x2d.shape
    n_tc = tm // LANES
    n_e = N_EXPERTS // eb
    const2 = lambda i, e: (0, 0)
    kern = functools.partial(_peer_kernel, tm=tm, eb=eb, final_norm=final_norm)
    n_tiles = PEER_HEADS * n_tc
    assert n_tiles % SUBLANES == 0
    tiles = (n_tiles * KEY_PITCH, LANES)
    per_key = (N_KEYS, SUBLANES, LANES)
    per_key_groups = (n_tiles // SUBLANES,) + per_key
    return pl.pallas_call(
        kern,
        grid=(n // tm, n_e),
        in_specs=[
            pl.BlockSpec((tm, d), lambda i, e: (i, 0)),
            pl.BlockSpec((1, d), const2),
            pl.BlockSpec((1, d), const2),
            pl.BlockSpec((PEER_HEADS * PEER_QUERY, d), const2),
            pl.BlockSpec((N_KEYS, PEER_HALF), const2),
            pl.BlockSpec((N_KEYS, PEER_HALF), const2),
            pl.BlockSpec((None, eb, d), lambda i, e: (layer, e, 0)),
            pl.BlockSpec((None, d, eb), lambda i, e: (layer, 0, e)),
        ],
        out_specs=pl.BlockSpec((tm, d), lambda i, e: (i, 0)),
        out_shape=jax.ShapeDtypeStruct((n, d), F32),
        scratch_shapes=[
            pltpu.VMEM((d, tm), BF16),
            pltpu.VMEM((d, tm), BF16),
            pltpu.VMEM(tiles, F32),
            pltpu.VMEM(tiles, F32),
            pltpu.VMEM(tiles, F32),
            pltpu.VMEM(per_key, F32),
            pltpu.VMEM(per_key, F32),
            pltpu.VMEM(per_key_groups, F32),
            pltpu.VMEM(per_key_groups, F32),
            pltpu.VMEM((eb, tm), F32),
            pltpu.VMEM((eb, tm), BF16),
            pltpu.VMEM((d, tm), F32),
        ],
        compiler_params=_params(("parallel", "arbitrary")),
        name="peer",
    )(x2d, g, g_final, wq_t_bf, k1_bf, k2_bf, u_bf, v_t_bf)


def _peer_tables_kernel(u_ref, v_ref, ub_ref, vt_ref):
    ub_ref[...] = u_ref[...].astype(BF16)
    vt_ref[0] = v_ref[0].T.astype(BF16)


def _peer_tables(peer_u, peer_v, *, tile=1024):
    n_layers, n_exp, d = peer_u.shape
    return pl.pallas_call(
        _peer_tables_kernel,
        grid=(n_layers, n_exp // tile),
        in_specs=[
            pl.BlockSpec((1, tile, d), lambda l, i: (l, i, 0)),
            pl.BlockSpec((1, tile, d), lambda l, i: (l, i, 0)),
        ],
        out_specs=[
            pl.BlockSpec((1, tile, d), lambda l, i: (l, i, 0)),
            pl.BlockSpec((1, d, tile), lambda l, i: (l, 0, i)),
        ],
        out_shape=[
            jax.ShapeDtypeStruct((n_layers, n_exp, d), BF16),
            jax.ShapeDtypeStruct((n_layers, d, n_exp), BF16),
        ],
        compiler_params=_params(("parallel", "parallel")),
        name="peer_tables",
    )(peer_u, peer_v)


def _trunk(x, h0_re, h0_im, w, *, scan_steps):
    bsz, t, d = x.shape
    n = bsz * t
    x2d = x.reshape(n, d)

    if bsz == SUBLANES and t % S5_ROW_TILE == 0:
        u_tm = _norm_matmul(x, w["norm_mix_g"][0], w["perm"], w["ssm_w_in"], tile=S5_ROW_TILE)
        g_tm, hr, hi = _s5_scan(u_tm, h0_re, h0_im, w["disc"], batch=bsz, steps=scan_steps)
        x2d = _glu_out(g_tm, x, w["perm"].T, w["ssm_w_glu"], w["ssm_b_glu"], w["ssm_w_out"],
                       tile=S5_ROW_TILE).reshape(n, d)
    else:
        u_bm = _norm_matmul(x2d[None], w["norm_mix_g"][0], w["perm"], w["ssm_w_in"], tile=n)
        u_tm = u_bm.reshape(bsz, t, d).transpose(1, 0, 2).reshape(n, d)
        g_tm, hr, hi = _s5_scan(u_tm, h0_re, h0_im, w["disc"], batch=bsz, steps=scan_steps)
        g_bm = g_tm.reshape(t, bsz, d).transpose(1, 0, 2).reshape(n, d)
        x2d = _glu_out(g_bm, x2d[None], w["perm"].T, w["ssm_w_glu"], w["ssm_b_glu"], w["ssm_w_out"],
                       tile=n).reshape(n, d)
    x2d = _peer(x2d, w["norm_ffn_g"][0], w["norm_final_g"], *w["peer"][0], *w["peer_tables"], layer=0,
                final_norm=False)

    if t % CHUNK == 0:
        mix, mix_bias = w["mix_full"]
    else:
        mix, mix_bias = w["mix_short"]
    x2d, v = _gmlp(x2d, w["norm_mix_g"][1], w["gmlp_w_in"], w["gmlp_b_in"], w["gmlp_ln_g"], w["gmlp_ln_b"],
                   mix, mix_bias, w["gmlp_w_out"], tile=512)
    y2d = _peer(x2d, w["norm_ffn_g"][1], w["norm_final_g"], *w["peer"][1], *w["peer_tables"], layer=1,
                final_norm=True)
    return y2d.reshape(bsz, t, d), hr, hi, v


def kernel(x_prompt, x_sample, state_ssm_re, state_ssm_im, norm_mix_g, norm_ffn_g, norm_final_g, ssm_w_in, ssm_a_re, ssm_a_im, ssm_log_dt, ssm_b_re, ssm_b_im, ssm_c_re, ssm_c_im, ssm_d, ssm_w_glu, ssm_b_glu, ssm_w_out, gmlp_w_in, gmlp_b_in, gmlp_ln_g, gmlp_ln_b, gmlp_w_s, gmlp_b_s, gmlp_w_out, peer_w_q, peer_k1, peer_k2, peer_u, peer_v):
    bp, tp, d = x_prompt.shape
    bs, ts, _ = x_sample.shape
    assert tp % CHUNK == 0 and CHUNK % ts == 0 and d == D_MODEL

    tril = jnp.tril(jnp.ones((CHUNK, CHUNK), F32))
    ws = gmlp_w_s[0] * tril
    bias_full = jnp.broadcast_to(gmlp_b_s[0][:, :, None], (GMLP_HEADS, CHUNK, GMLP_HEAD_DIM))
    reps = CHUNK // ts
    ws_short = jnp.einsum("ab,hij->haibj", jnp.eye(reps, dtype=F32), ws[:, :ts, :ts]).reshape(GMLP_HEADS, CHUNK, CHUNK)
    bias_short = jnp.broadcast_to(jnp.tile(gmlp_b_s[0][:, :ts], (1, reps))[:, :, None],
                                  (GMLP_HEADS, CHUNK, GMLP_HEAD_DIM))

    w = {
        "norm_mix_g": norm_mix_g[:, None, :],
        "norm_ffn_g": norm_ffn_g[:, None, :],
        "norm_final_g": norm_final_g[None, :],
        "perm": _time_major_perm(),
        "ssm_w_in": ssm_w_in[0].astype(BF16),
        "disc": _s5_discretize(ssm_a_re[0], ssm_a_im[0], ssm_log_dt[0], ssm_b_re[0], ssm_b_im[0],
                               ssm_c_re[0], ssm_c_im[0], ssm_d[0]),
        "ssm_w_glu": ssm_w_glu[0].astype(BF16),
        "ssm_b_glu": ssm_b_glu[0][None, :],
        "ssm_w_out": ssm_w_out[0].astype(BF16),
        "gmlp_w_in": gmlp_w_in[0].astype(BF16),
        "gmlp_b_in": gmlp_b_in[0][None, :],
        "gmlp_ln_g": gmlp_ln_g[0][None, :],
        "gmlp_ln_b": gmlp_ln_b[0][None, :],
        "mix_full": (ws.astype(BF16), bias_full),
        "mix_short": (ws_short.astype(BF16), bias_short),
        "gmlp_w_out": gmlp_w_out[0].astype(BF16),
        "peer": [
            (peer_w_q[i].T.astype(BF16), peer_k1[i].astype(BF16), peer_k2[i].astype(BF16)) for i in range(2)
        ],
        "peer_tables": _peer_tables(peer_u, peer_v),
    }

    zeros = jnp.zeros((bp, SSM_COLS), F32)
    y_p, hr_p, hi_p, _ = _trunk(x_prompt, zeros, zeros, w, scan_steps=64)
    y_s, hr_s, hi_s, v_s = _trunk(x_sample, state_ssm_re[0].reshape(bs, SSM_COLS),
                                  state_ssm_im[0].reshape(bs, SSM_COLS), w, scan_steps=ts)
    st = lambda a, b: a.reshape(1, b, SSM_GROUPS, SSM_STATE)
    return (y_p, y_s, st(hr_p, bp), st(hi_p, bp), st(hr_s, bs), st(hi_s, bs),
            v_s.reshape(1, bs, ts, GMLP_HALF))
```

```python
import functools
import math

import jax
import jax.numpy as jnp
from jax import lax
from jax.experimental import pallas as pl
from jax.experimental.pallas import tpu as pltpu

F32 = jnp.float32
BF16 = jnp.bfloat16

EPS = 1e-6
D_MODEL = 1024
SSM_GROUP = 16
SSM_GROUPS = 64
SSM_STATE = 64
SSM_COLS = SSM_GROUPS * SSM_STATE
GMLP_HALF = 2 * D_MODEL
GMLP_HEADS = 8
GMLP_HEAD_DIM = GMLP_HALF // GMLP_HEADS
CHUNK = 128
PEER_HEADS = 8
N_KEYS = 128
N_EXPERTS = N_KEYS * N_KEYS
PEER_QUERY = 256
PEER_HALF = 128
PEER_TOPK = 16

LANES = 128
SUBLANES = 8
MXU_DIM = 256
VMEM_LIMIT = 56 * 1024 * 1024

NEG_INF = float("-inf")
POS_INF = float("inf")
SQRT_HALF = math.sqrt(0.5)


def _dot(a, b):
    return jnp.dot(a, b, preferred_element_type=F32)


def _rms(x, g):
    ms = jnp.mean(x * x, axis=-1, keepdims=True)
    return x * lax.rsqrt(ms + EPS) * g


def _gelu(x):
    return 0.5 * x * (1.0 + lax.erf(x * SQRT_HALF))


def _params(sem, flags=None):
    return pltpu.CompilerParams(dimension_semantics=sem, vmem_limit_bytes=VMEM_LIMIT, flags=flags)


S5_ROW_TILE = 128
PERM_STEPS = MXU_DIM // SUBLANES


def _time_major_perm():
    r = jnp.arange(MXU_DIM)
    src = (r % SUBLANES) * PERM_STEPS + r // SUBLANES
    return (src[:, None] == r[None, :]).astype(BF16)


def _norm_matmul_kernel(x_ref, g_ref, p_ref, w_ref, o_ref, *, batch):
    tile = x_ref.shape[1]
    xn = [_rms(x_ref[b], g_ref[...]).astype(BF16) for b in range(batch)]
    if batch == 1:
        xp = xn[0]
    else:
        groups = []
        for tg in range(tile // PERM_STEPS):
            ts = slice(tg * PERM_STEPS, (tg + 1) * PERM_STEPS)
            rows_bt = jnp.concatenate([xn[b][ts] for b in range(batch)], axis=0)
            groups.append(_dot(p_ref[...], rows_bt).astype(BF16))
        xp = jnp.concatenate(groups, axis=0)
    o_ref[...] = _dot(xp, w_ref[...])


def _norm_matmul(x3d, g, perm, w_bf, *, tile):
    bsz, t, d = x3d.shape
    n = w_bf.shape[1]
    assert bsz == 1 or (bsz == SUBLANES and tile % PERM_STEPS == 0)
    return pl.pallas_call(
        functools.partial(_norm_matmul_kernel, batch=bsz),
        grid=(t // tile,),
        in_specs=[
            pl.BlockSpec((bsz, tile, d), lambda i: (0, i, 0)),
            pl.BlockSpec((1, d), lambda i: (0, 0)),
            pl.BlockSpec((MXU_DIM, MXU_DIM), lambda i: (0, 0)),
            pl.BlockSpec((d, n), lambda i: (0, 0)),
        ],
        out_specs=pl.BlockSpec((tile * bsz, n), lambda i: (i, 0)),
        out_shape=jax.ShapeDtypeStruct((t * bsz, n), F32),
        compiler_params=_params(("parallel",)),
        name="s5_in_proj",
    )(x3d, g, perm, w_bf)
SCAN_COLS = 512
N_SCAN_BLOCKS = SSM_COLS // SCAN_COLS
N_DIAG_BLOCKS = D_MODEL // MXU_DIM
DIAG_COLS = SSM_COLS // N_DIAG_BLOCKS
SCAN_PER_DIAG = DIAG_COLS // SCAN_COLS


def _s5_scan_kernel(u_ref, h0r_ref, h0i_ref, ar_ref, ai_ref, bbr_ref, bbi_ref, ccr_ref, cci_ref, d_ref,
                    g_ref, hr_ref, hi_ref, bur, bui, st_r, st_i, *, steps, batch):
    c = pl.program_id(0)
    n_sub = batch // SUBLANES

    @pl.when(c == 0)
    def _():
        for cb in range(N_SCAN_BLOCKS):
            st_r[cb] = h0r_ref[:, cb * SCAN_COLS:(cb + 1) * SCAN_COLS]
            st_i[cb] = h0i_ref[:, cb * SCAN_COLS:(cb + 1) * SCAN_COLS]

    u = u_ref[...]
    ub = u.astype(BF16)
    for kb in range(N_DIAG_BLOCKS):
        lhs = ub[:, kb * MXU_DIM:(kb + 1) * MXU_DIM]
        pr = _dot(lhs, bbr_ref[kb])
        pi = _dot(lhs, bbi_ref[kb])
        for j in range(SCAN_PER_DIAG):
            bur[kb * SCAN_PER_DIAG + j] = pr[:, j * SCAN_COLS:(j + 1) * SCAN_COLS]
            bui[kb * SCAN_PER_DIAG + j] = pi[:, j * SCAN_COLS:(j + 1) * SCAN_COLS]

    def scan_block(idx, carry):
        cb = idx // n_sub
        s = idx % n_sub
        ar = jnp.broadcast_to(ar_ref[cb], (SUBLANES, SCAN_COLS))
        ai = jnp.broadcast_to(ai_ref[cb], (SUBLANES, SCAN_COLS))
        row0 = pl.multiple_of(s * SUBLANES, SUBLANES)
        h_r = st_r[cb, pl.ds(row0, SUBLANES), :]
        h_i = st_i[cb, pl.ds(row0, SUBLANES), :]

        def step(t, h):
            hr, hi = h
            r = pl.multiple_of(t * batch + row0, SUBLANES)
            nr = ar * hr - ai * hi + bur[cb, pl.ds(r, SUBLANES), :]
            ni = ar * hi + ai * hr + bui[cb, pl.ds(r, SUBLANES), :]
            bur[cb, pl.ds(r, SUBLANES), :] = nr
            bui[cb, pl.ds(r, SUBLANES), :] = ni
            return nr, ni

        h_r, h_i = lax.fori_loop(0, steps, step, (h_r, h_i), unroll=min(steps, 8))
        st_r[cb, pl.ds(row0, SUBLANES), :] = h_r
        st_i[cb, pl.ds(row0, SUBLANES), :] = h_i
        return carry

    lax.fori_loop(0, N_SCAN_BLOCKS * n_sub, scan_block, 0)

    for cb in range(N_SCAN_BLOCKS):
        hr_ref[:, cb * SCAN_COLS:(cb + 1) * SCAN_COLS] = st_r[cb]
        hi_ref[:, cb * SCAN_COLS:(cb + 1) * SCAN_COLS] = st_i[cb]

    for nb in range(N_DIAG_BLOCKS):
        acc = None
        for j in range(SCAN_PER_DIAG):
            cb = nb * SCAN_PER_DIAG + j
            part = (_dot(bur[cb].astype(BF16), ccr_ref[nb, j * SCAN_COLS:(j + 1) * SCAN_COLS, :])
                    + _dot(bui[cb].astype(BF16), cci_ref[nb, j * SCAN_COLS:(j + 1) * SCAN_COLS, :]))
            acc = part if acc is None else acc + part
        cols = slice(nb * MXU_DIM, (nb + 1) * MXU_DIM)
        y = acc + d_ref[:, cols] * u[:, cols]
        g_ref[:, cols] = _gelu(y)


def _s5_scan(u_tm, h0_re, h0_im, disc, *, batch, steps):
    rows, d = u_tm.shape
    t = rows // batch
    n_chunks = t // steps
    blk = steps * batch
    ar, ai, bbr, bbi, ccr, cci, dsk = disc
    const2 = lambda c: (0, 0)
    const3 = lambda c: (0, 0, 0)
    kern = functools.partial(_s5_scan_kernel, steps=steps, batch=batch)
    return pl.pallas_call(
        kern,
        grid=(n_chunks,),
        in_specs=[
            pl.BlockSpec((blk, d), lambda c: (c, 0)),
            pl.BlockSpec((batch, SSM_COLS), const2),
            pl.BlockSpec((batch, SSM_COLS), const2),
            pl.BlockSpec((N_SCAN_BLOCKS, 1, SCAN_COLS), const3),
            pl.BlockSpec((N_SCAN_BLOCKS, 1, SCAN_COLS), const3),
            pl.BlockSpec((N_DIAG_BLOCKS, MXU_DIM, DIAG_COLS), const3),
            pl.BlockSpec((N_DIAG_BLOCKS, MXU_DIM, DIAG_COLS), const3),
            pl.BlockSpec((N_DIAG_BLOCKS, DIAG_COLS, MXU_DIM), const3),
            pl.BlockSpec((N_DIAG_BLOCKS, DIAG_COLS, MXU_DIM), const3),
            pl.BlockSpec((1, d), const2),
        ],
        out_specs=[
            pl.BlockSpec((blk, d), lambda c: (c, 0)),
            pl.BlockSpec((batch, SSM_COLS), const2),
            pl.BlockSpec((batch, SSM_COLS), const2),
        ],
        out_shape=[
            jax.ShapeDtypeStruct((rows, d), F32),
            jax.ShapeDtypeStruct((batch, SSM_COLS), F32),
            jax.ShapeDtypeStruct((batch, SSM_COLS), F32),
        ],
        scratch_shapes=[
            pltpu.VMEM((N_SCAN_BLOCKS, blk, SCAN_COLS), F32),
            pltpu.VMEM((N_SCAN_BLOCKS, blk, SCAN_COLS), F32),
            pltpu.VMEM((N_SCAN_BLOCKS, batch, SCAN_COLS), F32),
            pltpu.VMEM((N_SCAN_BLOCKS, batch, SCAN_COLS), F32),
        ],
        compiler_params=_params(("arbitrary",)),
        name="s5_scan",
    )(u_tm, h0_re, h0_im, ar, ai, bbr, bbi, ccr, cci, dsk)


def _glu_out_kernel(g_ref, x_ref, pt_ref, wg_ref, bg_ref, wo_ref, o_ref, *, batch):
    tile = x_ref.shape[1]
    g = g_ref[...]
    z = _dot(g.astype(BF16), wg_ref[...]) + bg_ref[...]
    o = (g * jax.nn.sigmoid(z)).astype(BF16)
    if batch == 1:
        o_ref[0] = x_ref[0] + _dot(o, wo_ref[...])
    else:
        n_groups = tile // PERM_STEPS
        groups = [_dot(pt_ref[...], o[tg * MXU_DIM:(tg + 1) * MXU_DIM]).astype(BF16) for tg in range(n_groups)]
        mix = _dot(jnp.concatenate(groups, axis=0), wo_ref[...])
        for tg in range(n_groups):
            for b in range(batch):
                ts = slice(tg * PERM_STEPS, (tg + 1) * PERM_STEPS)
                r0 = tg * MXU_DIM + b * PERM_STEPS
                o_ref[b, ts, :] = x_ref[b, ts, :] + mix[r0:r0 + PERM_STEPS]


def _glu_out(g_tm, x3d, perm_t, w_glu_bf, b_glu, w_out_bf, *, tile):
    bsz, t, d = x3d.shape
    assert bsz == 1 or (bsz == SUBLANES and tile % PERM_STEPS == 0)
    const2 = lambda i: (0, 0)
    return pl.pallas_call(
        functools.partial(_glu_out_kernel, batch=bsz),
        grid=(t // tile,),
        in_specs=[
            pl.BlockSpec((tile * bsz, d), lambda i: (i, 0)),
            pl.BlockSpec((bsz, tile, d), lambda i: (0, i, 0)),
            pl.BlockSpec((MXU_DIM, MXU_DIM), const2),
            pl.BlockSpec((d, d), const2),
            pl.BlockSpec((1, d), const2),
            pl.BlockSpec((d, d), const2),
        ],
        out_specs=pl.BlockSpec((bsz, tile, d), lambda i: (0, i, 0)),
        out_shape=jax.ShapeDtypeStruct((bsz, t, d), F32),
        compiler_params=_params(("parallel",)),
        name="s5_glu_out",
    )(g_tm, x3d, perm_t, w_glu_bf, b_glu, w_out_bf)


def _s5_discretize(a_re, a_im, log_dt, b_re, b_im, c_re, c_im, d_skip):
    dt = jnp.exp(log_dt)[:, None]
    mag = jnp.exp(a_re * dt)
    abar_re = mag * jnp.cos(a_im * dt)
    abar_im = mag * jnp.sin(a_im * dt)
    num_re = abar_re - 1.0
    num_im = abar_im
    den = a_re * a_re + a_im * a_im
    f_re = (num_re * a_re + num_im * a_im) / den
    f_im = (num_im * a_re - num_re * a_im) / den
    bb_re = f_re[..., None] * b_re - f_im[..., None] * b_im
    bb_im = f_re[..., None] * b_im + f_im[..., None] * b_re
    gpb = MXU_DIM // SSM_GROUP
    eye = jnp.eye(gpb, dtype=F32)

    def in_blocks(bb):
        bt = jnp.transpose(bb, (0, 2, 1)).reshape(N_DIAG_BLOCKS, gpb, SSM_GROUP, SSM_STATE)
        full = bt[:, :, :, None, :] * eye[None, :, None, :, None]
        return full.reshape(N_DIAG_BLOCKS, MXU_DIM, DIAG_COLS).astype(BF16)

    def out_blocks(cc):
        ct = jnp.transpose(cc, (0, 2, 1)).reshape(N_DIAG_BLOCKS, gpb, SSM_STATE, SSM_GROUP)
        full = ct[:, :, :, None, :] * eye[None, :, None, :, None]
        return full.reshape(N_DIAG_BLOCKS, DIAG_COLS, MXU_DIM).astype(BF16)

    ar = abar_re.reshape(N_SCAN_BLOCKS, 1, SCAN_COLS)
    ai = abar_im.reshape(N_SCAN_BLOCKS, 1, SCAN_COLS)
    return (ar, ai, in_blocks(bb_re), in_blocks(bb_im), out_blocks(c_re), out_blocks(-c_im),
            d_skip.reshape(1, D_MODEL))


def _gmlp_kernel(x_ref, g_ref, win_ref, bin_ref, lng_ref, lnb_ref, mix_ref, mixb_ref, wout_ref,
                 o_ref, v_ref, s_scr, *, rows):
    x = x_ref[...]
    hn = _rms(x, g_ref[...]).astype(BF16)
    z = _gelu(_dot(hn, win_ref[...]) + bin_ref[...])
    u = z[:, :GMLP_HALF]
    v = z[:, GMLP_HALF:]
    mu = jnp.mean(v, axis=-1, keepdims=True)
    vc = v - mu
    vn = vc * lax.rsqrt(jnp.mean(vc * vc, axis=-1, keepdims=True) + EPS) * lng_ref[...] + lnb_ref[...]
    v_ref[...] = vn
    vb = vn.astype(BF16)
    for r in range(rows // CHUNK):
        rs = slice(r * CHUNK, (r + 1) * CHUNK)
        for h in range(GMLP_HEADS):
            cs = slice(h * GMLP_HEAD_DIM, (h + 1) * GMLP_HEAD_DIM)
            mixed = _dot(mix_ref[h], vb[rs, cs]) + mixb_ref[h]
            s_scr[rs, cs] = (u[rs, cs] * mixed).astype(BF16)
    o_ref[...] = x + _dot(s_scr[...], wout_ref[...])


def _gmlp(x2d, g, w_in_bf, b_in, ln_g, ln_b, mix_bf, mix_bias, w_out_bf, *, tile):
    rows, d = x2d.shape
    const2 = lambda i: (0, 0)
    const3 = lambda i: (0, 0, 0)
    kern = functools.partial(_gmlp_kernel, rows=tile)
    return pl.pallas_call(
        kern,
        grid=(rows // tile,),
        in_specs=[
            pl.BlockSpec((tile, d), lambda i: (i, 0)),
            pl.BlockSpec((1, d), const2),
            pl.BlockSpec((d, 2 * GMLP_HALF), const2),
            pl.BlockSpec((1, 2 * GMLP_HALF), const2),
            pl.BlockSpec((1, GMLP_HALF), const2),
            pl.BlockSpec((1, GMLP_HALF), const2),
            pl.BlockSpec((GMLP_HEADS, CHUNK, CHUNK), const3),
            pl.BlockSpec((GMLP_HEADS, CHUNK, GMLP_HEAD_DIM), const3),
            pl.BlockSpec((GMLP_HALF, d), const2),
        ],
        out_specs=[
            pl.BlockSpec((tile, d), lambda i: (i, 0)),
            pl.BlockSpec((tile, GMLP_HALF), lambda i: (i, 0)),
        ],
        out_shape=[
            jax.ShapeDtypeStruct((rows, d), F32),
            jax.ShapeDtypeStruct((rows, GMLP_HALF), F32),
        ],
        scratch_shapes=[pltpu.VMEM((tile, GMLP_HALF), BF16)],
        compiler_params=_params(("parallel",)),
        name="gmlp",
    )(x2d, g, w_in_bf, b_in, ln_g, ln_b, mix_bf, mix_bias, w_out_bf)


KEY_PITCH = N_KEYS + SUBLANES


def _sorting_network(n):
    pairs = []
    p = 1
    while p < n:
        k = p
        while k >= 1:
            for j in range(k % p, n - k, 2 * k):
                for i in range(min(k, n - j - k)):
                    if (i + j) // (2 * p) == (i + j + k) // (2 * p):
                        pairs.append((i + j, i + j + k))
            k //= 2
        p *= 2
    return pairs


SORT16 = _sorting_network(PEER_TOPK)


def _compare_exchange(v, i, j):
    v[i], v[j] = jnp.maximum(v[i], v[j]), jnp.minimum(v[i], v[j])


def _sort16_desc(v):
    v = list(v)
    for i, j in SORT16:
        _compare_exchange(v, i, j)
    return v


def _merge_top16(a, b):
    c = [jnp.maximum(a[i], b[PEER_TOPK - 1 - i]) for i in range(PEER_TOPK)]
    d = PEER_TOPK // 2
    while d >= 1:
        for i in range(PEER_TOPK):
            if i & d == 0:
                _compare_exchange(c, i, i + d)
        d //= 2
    return c


def _top16_of(values):
    lists = [_sort16_desc(values[i:i + PEER_TOPK]) for i in range(0, len(values), PEER_TOPK)]
    while len(lists) > 1:
        merged = [_merge_top16(lists[i], lists[i + 1]) for i in range(0, len(lists) - 1, 2)]
        if len(lists) % 2:
            merged.append(lists[-1])
        lists = merged
    return lists[0]


def _peer_thresholds(s1_t, s2_t, e2_t, s1_v, s2_v, th_v, c1_v, grp):
    base = grp * (SUBLANES * KEY_PITCH)

    def key_rows(k):
        return pl.ds(base + k, SUBLANES, stride=KEY_PITCH)

    for k in range(N_KEYS):
        s1_v[k] = s1_t[key_rows(k), :]
        s2_v[k] = s2_t[key_rows(k), :]
    v1 = _top16_of([s1_v[k] for k in range(N_KEYS)])
    v2 = _top16_of([s2_v[k] for k in range(N_KEYS)])

    pairs = [(a, b) for a in range(PEER_TOPK) for b in range(PEER_TOPK // (a + 1))]
    cand = {ab: v1[ab[0]] + v2[ab[1]] for ab in pairs}
    first_row = [cand[(0, b)] for b in range(PEER_TOPK)]
    rest = [cand[ab] for ab in pairs if ab[0] > 0]
    rest += [jnp.full_like(v1[0], NEG_INF)] * (-len(rest) % PEER_TOPK)
    tau = _merge_top16(first_row, _top16_of(rest))[PEER_TOPK - 1]

    cmax = cand[(0, 0)]
    z = jnp.zeros_like(tau)
    th_rank = [jnp.full_like(tau, POS_INF) for _ in range(PEER_TOPK)]
    for a, b in pairs:
        sel = cand[(a, b)] >= tau
        z = z + jnp.where(sel, jnp.exp(cand[(a, b)] - cmax), 0.0)
        th_rank[a] = jnp.where(sel, v2[b], th_rank[a])
    c_scale = SQRT_HALF / z

    def per_key(k, carry):
        s1k = s1_v[k]
        th = jnp.full_like(s1k, POS_INF)
        for a in reversed(range(PEER_TOPK)):
            th = jnp.where(s1k >= v1[a], th_rank[a], th)
        th_v[grp, k] = th
        c1_v[grp, k] = jnp.exp(s1k - v1[0]) * c_scale
        e2_t[key_rows(k), :] = jnp.exp(s2_v[k] - v2[0])
        return carry

    lax.fori_loop(0, N_KEYS, per_key, 0, unroll=4)


def _peer_kernel(x_ref, g_ref, gf_ref, wq_ref, k1_ref, k2_ref, u_ref, vt_ref, o_ref,
                 xn_scr, xs_scr, s1_t, s2_t, e2_t, s1_v, s2_v, th_v, c1_v, h_scr, w_scr, acc_scr,
                 *, tm, eb, final_norm):
    e = pl.program_id(1)
    n_e = pl.num_programs(1)
    n_tc = tm // LANES
    keys_per_block = eb // N_KEYS
    n_groups = PEER_HEADS * n_tc // SUBLANES

    def tile_row(h, tc):
        return (h * n_tc + tc) * KEY_PITCH

    @pl.when(e == 0)
    def prepare():
        xn_t = _rms(x_ref[...], g_ref[...]).T
        xn_scr[...] = xn_t.astype(BF16)
        xs_scr[...] = (xn_t * SQRT_HALF).astype(BF16)
        for h in range(PEER_HEADS):
            qt = _dot(wq_ref[h * PEER_QUERY:(h + 1) * PEER_QUERY, :], xn_scr[...])
            s1 = _dot(k1_ref[...], qt[:PEER_HALF].astype(BF16))
            s2 = _dot(k2_ref[...], qt[PEER_HALF:].astype(BF16))
            for tc in range(n_tc):
                rows = pl.ds(tile_row(h, tc), N_KEYS)
                s1_t[rows, :] = s1[:, tc * LANES:(tc + 1) * LANES]
                s2_t[rows, :] = s2[:, tc * LANES:(tc + 1) * LANES]

        def thresholds(grp, carry):
            _peer_thresholds(s1_t, s2_t, e2_t, s1_v, s2_v, th_v, c1_v, grp)
            return carry

        lax.fori_loop(0, n_groups, thresholds, 0)
        acc_scr[...] = jnp.zeros_like(acc_scr)

    hs = _dot(u_ref[...], xs_scr[...])
    h_scr[...] = hs * (1.0 + lax.erf(hs))

    def key_row(kl, carry):
        i1 = e * keys_per_block + kl
        rows = pl.ds(pl.multiple_of(kl * N_KEYS, N_KEYS), N_KEYS)
        for tc in range(n_tc):
            lanes = slice(tc * LANES, (tc + 1) * LANES)
            gate = jnp.zeros((N_KEYS, LANES), F32)
            for h in range(PEER_HEADS):
                grp, j = divmod(h * n_tc + tc, SUBLANES)
                tile = pl.ds(tile_row(h, tc), N_KEYS)
                th = th_v[grp, i1, j:j + 1, :]
                c1 = c1_v[grp, i1, j:j + 1, :]
                gate = gate + jnp.where(s2_t[tile, :] >= th, c1 * e2_t[tile, :], 0.0)
            w_scr[rows, lanes] = (gate * h_scr[rows, lanes]).astype(BF16)
        return carry

    lax.fori_loop(0, keys_per_block, key_row, 0)
    acc_scr[...] += _dot(vt_ref[...], w_scr[...])

    @pl.when(e == n_e - 1)
    def _():
        y = x_ref[...] + acc_scr[...].T
        if final_norm:
            y = _rms(y, gf_ref[...])
        o_ref[...] = y


def _peer(x2d, g, g_final, wq_t_bf, k1_bf, k2_bf, u_bf, v_t_bf, *, layer, final_norm, tm=512, eb=2048):
    n, d = x2d.shape
    n_tc = tm // LANES
    n_e = N_EXPERTS // eb
    const2 = lambda i, e: (0, 0)
    kern = functools.partial(_peer_kernel, tm=tm, eb=eb, final_norm=final_norm)
    n_tiles = PEER_HEADS * n_tc
    assert n_tiles % SUBLANES == 0
    tiles = (n_tiles * KEY_PITCH, LANES)
    per_key = (N_KEYS, SUBLANES, LANES)
    per_key_groups = (n_tiles // SUBLANES,) + per_key
    return pl.pallas_call(
        kern,
        grid=(n // tm, n_e),
        in_specs=[
            pl.BlockSpec((tm, d), lambda i, e: (i, 0)),
            pl.BlockSpec((1, d), const2),
            pl.BlockSpec((1, d), const2),
            pl.BlockSpec((PEER_HEADS * PEER_QUERY, d), const2),
            pl.BlockSpec((N_KEYS, PEER_HALF), const2),
            pl.BlockSpec((N_KEYS, PEER_HALF), const2),
            pl.BlockSpec((None, eb, d), lambda i, e: (layer, e, 0)),
            pl.BlockSpec((None, d, eb), lambda i, e: (layer, 0, e)),
        ],
        out_specs=pl.BlockSpec((tm, d), lambda i, e: (i, 0)),
        out_shape=jax.ShapeDtypeStruct((n, d), F32),
        scratch_shapes=[
            pltpu.VMEM((d, tm), BF16),
            pltpu.VMEM((d, tm), BF16),
            pltpu.VMEM(tiles, F32),
            pltpu.VMEM(tiles, F32),
            pltpu.VMEM(tiles, F32),
            pltpu.VMEM(per_key, F32),
            pltpu.VMEM(per_key, F32),
            pltpu.VMEM(per_key_groups, F32),
            pltpu.VMEM(per_key_groups, F32),
            pltpu.VMEM((eb, tm), F32),
            pltpu.VMEM((eb, tm), BF16),
            pltpu.VMEM((d, tm), F32),
        ],
        compiler_params=_params(("parallel", "arbitrary")),
        name="peer",
    )(x2d, g, g_final, wq_t_bf, k1_bf, k2_bf, u_bf, v_t_bf)


def _peer_tables_kernel(u_ref, v_ref, ub_ref, vt_ref):
    ub_ref[...] = u_ref[...].astype(BF16)
    vt_ref[0] = v_ref[0].T.astype(BF16)


def _peer_tables(peer_u, peer_v, *, tile=1024):
    n_layers, n_exp, d = peer_u.shape
    return pl.pallas_call(
        _peer_tables_kernel,
        grid=(n_layers, n_exp // tile),
        in_specs=[
            pl.BlockSpec((1, tile, d), lambda l, i: (l, i, 0)),
            pl.BlockSpec((1, tile, d), lambda l, i: (l, i, 0)),
        ],
        out_specs=[
            pl.BlockSpec((1, tile, d), lambda l, i: (l, i, 0)),
            pl.BlockSpec((1, d, tile), lambda l, i: (l, 0, i)),
        ],
        out_shape=[
            jax.ShapeDtypeStruct((n_layers, n_exp, d), BF16),
            jax.ShapeDtypeStruct((n_layers, d, n_exp), BF16),
        ],
        compiler_params=_params(("parallel", "parallel")),
        name="peer_tables",
    )(peer_u, peer_v)


def _trunk(x, h0_re, h0_im, w, *, scan_steps):
    bsz, t, d = x.shape
    n = bsz * t
    x2d = x.reshape(n, d)

    if bsz == SUBLANES and t % S5_ROW_TILE == 0:
        u_tm = _norm_matmul(x, w["norm_mix_g"][0], w["perm"], w["ssm_w_in"], tile=S5_ROW_TILE)
        g_tm, hr, hi = _s5_scan(u_tm, h0_re, h0_im, w["disc"], batch=bsz, steps=scan_steps)
        x2d = _glu_out(g_tm, x, w["perm"].T, w["ssm_w_glu"], w["ssm_b_glu"], w["ssm_w_out"],
                       tile=S5_ROW_TILE).reshape(n, d)
    else:
        u_bm = _norm_matmul(x2d[None], w["norm_mix_g"][0], w["perm"], w["ssm_w_in"], tile=n)
        u_tm = u_bm.reshape(bsz, t, d).transpose(1, 0, 2).reshape(n, d)
        g_tm, hr, hi = _s5_scan(u_tm, h0_re, h0_im, w["disc"], batch=bsz, steps=scan_steps)
        g_bm = g_tm.reshape(t, bsz, d).transpose(1, 0, 2).reshape(n, d)
        x2d = _glu_out(g_bm, x2d[None], w["perm"].T, w["ssm_w_glu"], w["ssm_b_glu"], w["ssm_w_out"],
                       tile=n).reshape(n, d)
    x2d = _peer(x2d, w["norm_ffn_g"][0], w["norm_final_g"], *w["peer"][0], *w["peer_tables"], layer=0,
                final_norm=False)

    if t % CHUNK == 0:
        mix, mix_bias = w["mix_full"]
    else:
        mix, mix_bias = w["mix_short"]
    x2d, v = _gmlp(x2d, w["norm_mix_g"][1], w["gmlp_w_in"], w["gmlp_b_in"], w["gmlp_ln_g"], w["gmlp_ln_b"],
                   mix, mix_bias, w["gmlp_w_out"], tile=512)
    y2d = _peer(x2d, w["norm_ffn_g"][1], w["norm_final_g"], *w["peer"][1], *w["peer_tables"], layer=1,
                final_norm=True)
    return y2d.reshape(bsz, t, d), hr, hi, v


def kernel(x_prompt, x_sample, state_ssm_re, state_ssm_im, norm_mix_g, norm_ffn_g, norm_final_g, ssm_w_in, ssm_a_re, ssm_a_im, ssm_log_dt, ssm_b_re, ssm_b_im, ssm_c_re, ssm_c_im, ssm_d, ssm_w_glu, ssm_b_glu, ssm_w_out, gmlp_w_in, gmlp_b_in, gmlp_ln_g, gmlp_ln_b, gmlp_w_s, gmlp_b_s, gmlp_w_out, peer_w_q, peer_k1, peer_k2, peer_u, peer_v):
    bp, tp, d = x_prompt.shape
    bs, ts, _ = x_sample.shape
    assert tp % CHUNK == 0 and CHUNK % ts == 0 and d == D_MODEL

    tril = jnp.tril(jnp.ones((CHUNK, CHUNK), F32))
    ws = gmlp_w_s[0] * tril
    bias_full = jnp.broadcast_to(gmlp_b_s[0][:, :, None], (GMLP_HEADS, CHUNK, GMLP_HEAD_DIM))
    reps = CHUNK // ts
    ws_short = jnp.einsum("ab,hij->haibj", jnp.eye(reps, dtype=F32), ws[:, :ts, :ts]).reshape(GMLP_HEADS, CHUNK, CHUNK)
    bias_short = jnp.broadcast_to(jnp.tile(gmlp_b_s[0][:, :ts], (1, reps))[:, :, None],
                                  (GMLP_HEADS, CHUNK, GMLP_HEAD_DIM))

    w = {
        "norm_mix_g": norm_mix_g[:, None, :],
        "norm_ffn_g": norm_ffn_g[:, None, :],
        "norm_final_g": norm_final_g[None, :],
        "perm": _time_major_perm(),
        "ssm_w_in": ssm_w_in[0].astype(BF16),
        "disc": _s5_discretize(ssm_a_re[0], ssm_a_im[0], ssm_log_dt[0], ssm_b_re[0], ssm_b_im[0],
                               ssm_c_re[0], ssm_c_im[0], ssm_d[0]),
        "ssm_w_glu": ssm_w_glu[0].astype(BF16),
        "ssm_b_glu": ssm_b_glu[0][None, :],
        "ssm_w_out": ssm_w_out[0].astype(BF16),
        "gmlp_w_in": gmlp_w_in[0].astype(BF16),
        "gmlp_b_in": gmlp_b_in[0][None, :],
        "gmlp_ln_g": gmlp_ln_g[0][None, :],
        "gmlp_ln_b": gmlp_ln_b[0][None, :],
        "mix_full": (ws.astype(BF16), bias_full),
        "mix_short": (ws_short.astype(BF16), bias_short),
        "gmlp_w_out": gmlp_w_out[0].astype(BF16),
        "peer": [
            (peer_w_q[i].T.astype(BF16), peer_k1[i].astype(BF16), peer_k2[i].astype(BF16)) for i in range(2)
        ],
        "peer_tables": _peer_tables(peer_u, peer_v),
    }

    zeros = jnp.zeros((bp, SSM_COLS), F32)
    y_p, hr_p, hi_p, _ = _trunk(x_prompt, zeros, zeros, w, scan_steps=64)
    y_s, hr_s, hi_s, v_s = _trunk(x_sample, state_ssm_re[0].reshape(bs, SSM_COLS),
                                  state_ssm_im[0].reshape(bs, SSM_COLS), w, scan_steps=ts)
    st = lambda a, b: a.reshape(1, b, SSM_GROUPS, SSM_STATE)
    return (y_p, y_s, st(hr_p, bp), st(hi_p, bp), st(hr_s, bs), st(hi_s, bs),
            v_s.reshape(1, bs, ts, GMLP_HALF))
```

```python
import functools
import math

import jax
import jax.numpy as jnp
from jax import lax
from jax.experimental import pallas as pl
from jax.experimental.pallas import tpu as pltpu

F32 = jnp.float32
BF16 = jnp.bfloat16

EPS = 1e-6
D_MODEL = 1024
SSM_GROUP = 16
SSM_GROUPS = 64
SSM_STATE = 64
SSM_COLS = SSM_GROUPS * SSM_STATE
GMLP_HALF = 2 * D_MODEL
GMLP_HEADS = 8
GMLP_HEAD_DIM = GMLP_HALF // GMLP_HEADS
CHUNK = 128
PEER_HEADS = 8
N_KEYS = 128
N_EXPERTS = N_KEYS * N_KEYS
PEER_QUERY = 256
PEER_HALF = 128
PEER_TOPK = 16

LANES = 128
SUBLANES = 8
MXU_DIM = 256
VMEM_LIMIT = 56 * 1024 * 1024

NEG_INF = float("-inf")
POS_INF = float("inf")
SQRT_HALF = math.sqrt(0.5)


def _dot(a, b):
    return jnp.dot(a, b, preferred_element_type=F32)


def _rms(x, g):
    ms = jnp.mean(x * x, axis=-1, keepdims=True)
    return x * lax.rsqrt(ms + EPS) * g


def _gelu(x):
    return 0.5 * x * (1.0 + lax.erf(x * SQRT_HALF))


def _params(sem):
    return pltpu.CompilerParams(dimension_semantics=sem, vmem_limit_bytes=VMEM_LIMIT)


S5_ROW_TILE = 128
S5_SCAN_STEPS = 64
GMLP_ROW_TILE = 512
PEER_TOKEN_TILE = 512
PEER_EXPERT_BLOCK = 2048


PERM_STEPS = MXU_DIM // SUBLANES


def _time_major_perm():
    r = jnp.arange(MXU_DIM)
    src = (r % SUBLANES) * PERM_STEPS + r // SUBLANES
    return (src[:, None] == r[None, :]).astype(BF16)


def _norm_matmul_kernel(x_ref, g_ref, p_ref, w_ref, o_ref, *, batch):
    tile = x_ref.shape[1]
    xn = [_rms(x_ref[b], g_ref[...]).astype(BF16) for b in range(batch)]
    if batch == 1:
        xp = xn[0]
    else:
        groups = []
        for tg in range(tile // PERM_STEPS):
            ts = slice(tg * PERM_STEPS, (tg + 1) * PERM_STEPS)
            rows_bt = jnp.concatenate([xn[b][ts] for b in range(batch)], axis=0)
            groups.append(_dot(p_ref[...], rows_bt).astype(BF16))
        xp = jnp.concatenate(groups, axis=0)
    o_ref[...] = _dot(xp, w_ref[...])


def _norm_matmul(x3d, g, perm, w_bf, *, tile):
    bsz, t, d = x3d.shape
    n = w_bf.shape[1]
    assert bsz == 1 or (bsz == SUBLANES and tile % PERM_STEPS == 0)
    return pl.pallas_call(
        functools.partial(_norm_matmul_kernel, batch=bsz),
        grid=(t // tile,),
        in_specs=[
            pl.BlockSpec((bsz, tile, d), lambda i: (0, i, 0)),
            pl.BlockSpec((1, d), lambda i: (0, 0)),
            pl.BlockSpec((MXU_DIM, MXU_DIM), lambda i: (0, 0)),
            pl.BlockSpec((d, n), lambda i: (0, 0)),
        ],
        out_specs=pl.BlockSpec((tile * bsz, n), lambda i: (i, 0)),
        out_shape=jax.ShapeDtypeStruct((t * bsz, n), F32),
        compiler_params=_params(("parallel",)),
        name="s5_in_proj",
    )(x3d, g, perm, w_bf)
SCAN_COLS = 512
N_SCAN_BLOCKS = SSM_COLS // SCAN_COLS
N_DIAG_BLOCKS = D_MODEL // MXU_DIM
DIAG_COLS = SSM_COLS // N_DIAG_BLOCKS
SCAN_PER_DIAG = DIAG_COLS // SCAN_COLS


def _s5_scan_kernel(u_ref, h0r_ref, h0i_ref, ar_ref, ai_ref, bbr_ref, bbi_ref, ccr_ref, cci_ref, d_ref,
                    g_ref, hr_ref, hi_ref, bur, bui, st_r, st_i, *, steps, batch):
    c = pl.program_id(0)
    n_sub = batch // SUBLANES

    @pl.when(c == 0)
    def _():
        for cb in range(N_SCAN_BLOCKS):
            st_r[cb] = h0r_ref[:, cb * SCAN_COLS:(cb + 1) * SCAN_COLS]
            st_i[cb] = h0i_ref[:, cb * SCAN_COLS:(cb + 1) * SCAN_COLS]

    u = u_ref[...]
    ub = u.astype(BF16)
    for kb in range(N_DIAG_BLOCKS):
        lhs = ub[:, kb * MXU_DIM:(kb + 1) * MXU_DIM]
        pr = _dot(lhs, bbr_ref[kb])
        pi = _dot(lhs, bbi_ref[kb])
        for j in range(SCAN_PER_DIAG):
            bur[kb * SCAN_PER_DIAG + j] = pr[:, j * SCAN_COLS:(j + 1) * SCAN_COLS]
            bui[kb * SCAN_PER_DIAG + j] = pi[:, j * SCAN_COLS:(j + 1) * SCAN_COLS]

    def scan_block(idx, carry):
        cb = idx // n_sub
        s = idx % n_sub
        ar = jnp.broadcast_to(ar_ref[cb], (SUBLANES, SCAN_COLS))
        ai = jnp.broadcast_to(ai_ref[cb], (SUBLANES, SCAN_COLS))
        row0 = pl.multiple_of(s * SUBLANES, SUBLANES)
        h_r = st_r[cb, pl.ds(row0, SUBLANES), :]
        h_i = st_i[cb, pl.ds(row0, SUBLANES), :]

        def step(t, h):
            hr, hi = h
            r = pl.multiple_of(t * batch + row0, SUBLANES)
            nr = ar * hr - ai * hi + bur[cb, pl.ds(r, SUBLANES), :]
            ni = ar * hi + ai * hr + bui[cb, pl.ds(r, SUBLANES), :]
            bur[cb, pl.ds(r, SUBLANES), :] = nr
            bui[cb, pl.ds(r, SUBLANES), :] = ni
            return nr, ni

        h_r, h_i = lax.fori_loop(0, steps, step, (h_r, h_i), unroll=min(steps, 8))
        st_r[cb, pl.ds(row0, SUBLANES), :] = h_r
        st_i[cb, pl.ds(row0, SUBLANES), :] = h_i
        return carry

    lax.fori_loop(0, N_SCAN_BLOCKS * n_sub, scan_block, 0)

    for cb in range(N_SCAN_BLOCKS):
        hr_ref[:, cb * SCAN_COLS:(cb + 1) * SCAN_COLS] = st_r[cb]
        hi_ref[:, cb * SCAN_COLS:(cb + 1) * SCAN_COLS] = st_i[cb]

    for nb in range(N_DIAG_BLOCKS):
        acc = None
        for j in range(SCAN_PER_DIAG):
            cb = nb * SCAN_PER_DIAG + j
            part = (_dot(bur[cb].astype(BF16), ccr_ref[nb, j * SCAN_COLS:(j + 1) * SCAN_COLS, :])
                    + _dot(bui[cb].astype(BF16), cci_ref[nb, j * SCAN_COLS:(j + 1) * SCAN_COLS, :]))
            acc = part if acc is None else acc + part
        cols = slice(nb * MXU_DIM, (nb + 1) * MXU_DIM)
        y = acc + d_ref[:, cols] * u[:, cols]
        g_ref[:, cols] = _gelu(y)


def _s5_scan(u_tm, h0_re, h0_im, disc, *, batch, steps):
    rows, d = u_tm.shape
    t = rows // batch
    n_chunks = t // steps
    blk = steps * batch
    ar, ai, bbr, bbi, ccr, cci, dsk = disc
    const2 = lambda c: (0, 0)
    const3 = lambda c: (0, 0, 0)
    kern = functools.partial(_s5_scan_kernel, steps=steps, batch=batch)
    return pl.pallas_call(
        kern,
        grid=(n_chunks,),
        in_specs=[
            pl.BlockSpec((blk, d), lambda c: (c, 0)),
            pl.BlockSpec((batch, SSM_COLS), const2),
            pl.BlockSpec((batch, SSM_COLS), const2),
            pl.BlockSpec((N_SCAN_BLOCKS, 1, SCAN_COLS), const3),
            pl.BlockSpec((N_SCAN_BLOCKS, 1, SCAN_COLS), const3),
            pl.BlockSpec((N_DIAG_BLOCKS, MXU_DIM, DIAG_COLS), const3),
            pl.BlockSpec((N_DIAG_BLOCKS, MXU_DIM, DIAG_COLS), const3),
            pl.BlockSpec((N_DIAG_BLOCKS, DIAG_COLS, MXU_DIM), const3),
            pl.BlockSpec((N_DIAG_BLOCKS, DIAG_COLS, MXU_DIM), const3),
            pl.BlockSpec((1, d), const2),
        ],
        out_specs=[
            pl.BlockSpec((blk, d), lambda c: (c, 0)),
            pl.BlockSpec((batch, SSM_COLS), const2),
            pl.BlockSpec((batch, SSM_COLS), const2),
        ],
        out_shape=[
            jax.ShapeDtypeStruct((rows, d), F32),
            jax.ShapeDtypeStruct((batch, SSM_COLS), F32),
            jax.ShapeDtypeStruct((batch, SSM_COLS), F32),
        ],
        scratch_shapes=[
            pltpu.VMEM((N_SCAN_BLOCKS, blk, SCAN_COLS), F32),
            pltpu.VMEM((N_SCAN_BLOCKS, blk, SCAN_COLS), F32),
            pltpu.VMEM((N_SCAN_BLOCKS, batch, SCAN_COLS), F32),
            pltpu.VMEM((N_SCAN_BLOCKS, batch, SCAN_COLS), F32),
        ],
        compiler_params=_params(("arbitrary",)),
        name="s5_scan",
    )(u_tm, h0_re, h0_im, ar, ai, bbr, bbi, ccr, cci, dsk)


def _glu_out_kernel(g_ref, x_ref, pt_ref, wg_ref, bg_ref, wo_ref, o_ref, *, batch):
    tile = x_ref.shape[1]
    g = g_ref[...]
    z = _dot(g.astype(BF16), wg_ref[...]) + bg_ref[...]
    o = (g * jax.nn.sigmoid(z)).astype(BF16)
    if batch == 1:
        o_ref[0] = x_ref[0] + _dot(o, wo_ref[...])
    else:
        n_groups = tile // PERM_STEPS
        groups = [_dot(pt_ref[...], o[tg * MXU_DIM:(tg + 1) * MXU_DIM]).astype(BF16) for tg in range(n_groups)]
        mix = _dot(jnp.concatenate(groups, axis=0), wo_ref[...])
        for tg in range(n_groups):
            for b in range(batch):
                ts = slice(tg * PERM_STEPS, (tg + 1) * PERM_STEPS)
                r0 = tg * MXU_DIM + b * PERM_STEPS
                o_ref[b, ts, :] = x_ref[b, ts, :] + mix[r0:r0 + PERM_STEPS]


def _glu_out(g_tm, x3d, perm_t, w_glu_bf, b_glu, w_out_bf, *, tile):
    bsz, t, d = x3d.shape
    assert bsz == 1 or (bsz == SUBLANES and tile % PERM_STEPS == 0)
    const2 = lambda i: (0, 0)
    return pl.pallas_call(
        functools.partial(_glu_out_kernel, batch=bsz),
        grid=(t // tile,),
        in_specs=[
            pl.BlockSpec((tile * bsz, d), lambda i: (i, 0)),
            pl.BlockSpec((bsz, tile, d), lambda i: (0, i, 0)),
            pl.BlockSpec((MXU_DIM, MXU_DIM), const2),
            pl.BlockSpec((d, d), const2),
            pl.BlockSpec((1, d), const2),
            pl.BlockSpec((d, d), const2),
        ],
        out_specs=pl.BlockSpec((bsz, tile, d), lambda i: (0, i, 0)),
        out_shape=jax.ShapeDtypeStruct((bsz, t, d), F32),
        compiler_params=_params(("parallel",)),
        name="s5_glu_out",
    )(g_tm, x3d, perm_t, w_glu_bf, b_glu, w_out_bf)


def _s5_discretize(a_re, a_im, log_dt, b_re, b_im, c_re, c_im, d_skip):
    dt = jnp.exp(log_dt)[:, None]
    mag = jnp.exp(a_re * dt)
    abar_re = mag * jnp.cos(a_im * dt)
    abar_im = mag * jnp.sin(a_im * dt)
    num_re = abar_re - 1.0
    num_im = abar_im
    den = a_re * a_re + a_im * a_im
    f_re = (num_re * a_re + num_im * a_im) / den
    f_im = (num_im * a_re - num_re * a_im) / den
    bb_re = f_re[..., None] * b_re - f_im[..., None] * b_im
    bb_im = f_re[..., None] * b_im + f_im[..., None] * b_re
    gpb = MXU_DIM // SSM_GROUP
    eye = jnp.eye(gpb, dtype=F32)

    def in_blocks(bb):
        bt = jnp.transpose(bb, (0, 2, 1)).reshape(N_DIAG_BLOCKS, gpb, SSM_GROUP, SSM_STATE)
        full = bt[:, :, :, None, :] * eye[None, :, None, :, None]
        return full.reshape(N_DIAG_BLOCKS, MXU_DIM, DIAG_COLS).astype(BF16)

    def out_blocks(cc):
        ct = jnp.transpose(cc, (0, 2, 1)).reshape(N_DIAG_BLOCKS, gpb, SSM_STATE, SSM_GROUP)
        full = ct[:, :, :, None, :] * eye[None, :, None, :, None]
        return full.reshape(N_DIAG_BLOCKS, DIAG_COLS, MXU_DIM).astype(BF16)

    ar = abar_re.reshape(N_SCAN_BLOCKS, 1, SCAN_COLS)
    ai = abar_im.reshape(N_SCAN_BLOCKS, 1, SCAN_COLS)
    return (ar, ai, in_blocks(bb_re), in_blocks(bb_im), out_blocks(c_re), out_blocks(-c_im),
            d_skip.reshape(1, D_MODEL))


def _gmlp_kernel(x_ref, g_ref, win_ref, bin_ref, lng_ref, lnb_ref, mix_ref, mixb_ref, wout_ref,
                 o_ref, v_ref, s_scr, *, rows):
    x = x_ref[...]
    hn = _rms(x, g_ref[...]).astype(BF16)
    z = _gelu(_dot(hn, win_ref[...]) + bin_ref[...])
    u = z[:, :GMLP_HALF]
    v = z[:, GMLP_HALF:]
    mu = jnp.mean(v, axis=-1, keepdims=True)
    vc = v - mu
    vn = vc * lax.rsqrt(jnp.mean(vc * vc, axis=-1, keepdims=True) + EPS) * lng_ref[...] + lnb_ref[...]
    v_ref[...] = vn
    vb = vn.astype(BF16)
    for r in range(rows // CHUNK):
        rs = slice(r * CHUNK, (r + 1) * CHUNK)
        for h in range(GMLP_HEADS):
            cs = slice(h * GMLP_HEAD_DIM, (h + 1) * GMLP_HEAD_DIM)
            mixed = _dot(mix_ref[h], vb[rs, cs]) + mixb_ref[h]
            s_scr[rs, cs] = (u[rs, cs] * mixed).astype(BF16)
    o_ref[...] = x + _dot(s_scr[...], wout_ref[...])


def _gmlp(x2d, g, w_in_bf, b_in, ln_g, ln_b, mix_bf, mix_bias, w_out_bf, *, tile):
    rows, d = x2d.shape
    const2 = lambda i: (0, 0)
    const3 = lambda i: (0, 0, 0)
    kern = functools.partial(_gmlp_kernel, rows=tile)
    return pl.pallas_call(
        kern,
        grid=(rows // tile,),
        in_specs=[
            pl.BlockSpec((tile, d), lambda i: (i, 0)),
            pl.BlockSpec((1, d), const2),
            pl.BlockSpec((d, 2 * GMLP_HALF), const2),
            pl.BlockSpec((1, 2 * GMLP_HALF), const2),
            pl.BlockSpec((1, GMLP_HALF), const2),
            pl.BlockSpec((1, GMLP_HALF), const2),
            pl.BlockSpec((GMLP_HEADS, CHUNK, CHUNK), const3),
            pl.BlockSpec((GMLP_HEADS, CHUNK, GMLP_HEAD_DIM), const3),
            pl.BlockSpec((GMLP_HALF, d), const2),
        ],
        out_specs=[
            pl.BlockSpec((tile, d), lambda i: (i, 0)),
            pl.BlockSpec((tile, GMLP_HALF), lambda i: (i, 0)),
        ],
        out_shape=[
            jax.ShapeDtypeStruct((rows, d), F32),
            jax.ShapeDtypeStruct((rows, GMLP_HALF), F32),
        ],
        scratch_shapes=[pltpu.VMEM((tile, GMLP_HALF), BF16)],
        compiler_params=_params(("parallel",)),
        name="gmlp",
    )(x2d, g, w_in_bf, b_in, ln_g, ln_b, mix_bf, mix_bias, w_out_bf)


KEY_PITCH = N_KEYS + SUBLANES


def _sorting_network(n):
    pairs = []
    p = 1
    while p < n:
        k = p
        while k >= 1:
            for j in range(k % p, n - k, 2 * k):
                for i in range(min(k, n - j - k)):
                    if (i + j) // (2 * p) == (i + j + k) // (2 * p):
                        pairs.append((i + j, i + j + k))
            k //= 2
        p *= 2
    return pairs


SORT16 = _sorting_network(PEER_TOPK)


def _compare_exchange(v, i, j):
    v[i], v[j] = jnp.maximum(v[i], v[j]), jnp.minimum(v[i], v[j])


def _sort16_desc(v):
    v = list(v)
    for i, j in SORT16:
        _compare_exchange(v, i, j)
    return v


def _merge_top16(a, b):
    c = [jnp.maximum(a[i], b[PEER_TOPK - 1 - i]) for i in range(PEER_TOPK)]
    d = PEER_TOPK // 2
    while d >= 1:
        for i in range(PEER_TOPK):
            if i & d == 0:
                _compare_exchange(c, i, i + d)
        d //= 2
    return c


def _top16_of(values):
    lists = [_sort16_desc(values[i:i + PEER_TOPK]) for i in range(0, len(values), PEER_TOPK)]
    while len(lists) > 1:
        merged = [_merge_top16(lists[i], lists[i + 1]) for i in range(0, len(lists) - 1, 2)]
        if len(lists) % 2:
            merged.append(lists[-1])
        lists = merged
    return lists[0]


def _peer_thresholds(s1_t, s2_t, e2_t, s1_v, s2_v, th_v, c1_v, grp):
    base = grp * (SUBLANES * KEY_PITCH)

    def key_rows(k):
        return pl.ds(base + k, SUBLANES, stride=KEY_PITCH)

    for k in range(N_KEYS):
        s1_v[k] = s1_t[key_rows(k), :]
        s2_v[k] = s2_t[key_rows(k), :]
    v1 = _top16_of([s1_v[k] for k in range(N_KEYS)])
    v2 = _top16_of([s2_v[k] for k in range(N_KEYS)])

    pairs = [(a, b) for a in range(PEER_TOPK) for b in range(PEER_TOPK // (a + 1))]
    cand = {ab: v1[ab[0]] + v2[ab[1]] for ab in pairs}
    first_row = [cand[(0, b)] for b in range(PEER_TOPK)]
    rest = [cand[ab] for ab in pairs if ab[0] > 0]
    rest += [jnp.full_like(v1[0], NEG_INF)] * (-len(rest) % PEER_TOPK)
    tau = _merge_top16(first_row, _top16_of(rest))[PEER_TOPK - 1]

    cmax = cand[(0, 0)]
    z = jnp.zeros_like(tau)
    th_rank = [jnp.full_like(tau, POS_INF) for _ in range(PEER_TOPK)]
    for a, b in pairs:
        sel = cand[(a, b)] >= tau
        z = z + jnp.where(sel, jnp.exp(cand[(a, b)] - cmax), 0.0)
        th_rank[a] = jnp.where(sel, v2[b], th_rank[a])
    c_scale = SQRT_HALF / z

    def per_key(k, carry):
        s1k = s1_v[k]
        th = jnp.full_like(s1k, POS_INF)
        for a in reversed(range(PEER_TOPK)):
            th = jnp.where(s1k >= v1[a], th_rank[a], th)
        th_v[grp, k] = th
        c1_v[grp, k] = jnp.exp(s1k - v1[0]) * c_scale
        e2_t[key_rows(k), :] = jnp.exp(s2_v[k] - v2[0])
        return carry

    lax.fori_loop(0, N_KEYS, per_key, 0, unroll=4)


def _peer_kernel(x_ref, g_ref, gf_ref, wq_ref, k1_ref, k2_ref, u_ref, vt_ref, o_ref,
                 xn_scr, xs_scr, s1_t, s2_t, e2_t, s1_v, s2_v, th_v, c1_v, h_scr, w_scr, acc_scr,
                 *, tm, eb, final_norm):
    e = pl.program_id(1)
    n_e = pl.num_programs(1)
    n_tc = tm // LANES
    keys_per_block = eb // N_KEYS
    n_groups = PEER_HEADS * n_tc // SUBLANES

    def tile_row(h, tc):
        return (h * n_tc + tc) * KEY_PITCH

    @pl.when(e == 0)
    def prepare():
        xn_t = _rms(x_ref[...], g_ref[...]).T
        xn_scr[...] = xn_t.astype(BF16)
        xs_scr[...] = (xn_t * SQRT_HALF).astype(BF16)
        for h in range(PEER_HEADS):
            qt = _dot(wq_ref[h * PEER_QUERY:(h + 1) * PEER_QUERY, :], xn_scr[...])
            s1 = _dot(k1_ref[...], qt[:PEER_HALF].astype(BF16))
            s2 = _dot(k2_ref[...], qt[PEER_HALF:].astype(BF16))
            for tc in range(n_tc):
                rows = pl.ds(tile_row(h, tc), N_KEYS)
                s1_t[rows, :] = s1[:, tc * LANES:(tc + 1) * LANES]
                s2_t[rows, :] = s2[:, tc * LANES:(tc + 1) * LANES]

        def thresholds(grp, carry):
            _peer_thresholds(s1_t, s2_t, e2_t, s1_v, s2_v, th_v, c1_v, grp)
            return carry

        lax.fori_loop(0, n_groups, thresholds, 0)
        acc_scr[...] = jnp.zeros_like(acc_scr)

    hs = _dot(u_ref[...], xs_scr[...])
    h_scr[...] = hs * (1.0 + lax.erf(hs))

    def key_row(kl, carry):
        i1 = e * keys_per_block + kl
        rows = pl.ds(pl.multiple_of(kl * N_KEYS, N_KEYS), N_KEYS)
        for tc in range(n_tc):
            lanes = slice(tc * LANES, (tc + 1) * LANES)
            gate = jnp.zeros((N_KEYS, LANES), F32)
            for h in range(PEER_HEADS):
                grp, j = divmod(h * n_tc + tc, SUBLANES)
                tile = pl.ds(tile_row(h, tc), N_KEYS)
                th = th_v[grp, i1, j:j + 1, :]
                c1 = c1_v[grp, i1, j:j + 1, :]
                gate = gate + jnp.where(s2_t[tile, :] >= th, c1 * e2_t[tile, :], 0.0)
            w_scr[rows, lanes] = (gate * h_scr[rows, lanes]).astype(BF16)
        return carry

    lax.fori_loop(0, keys_per_block, key_row, 0)
    acc_scr[...] += _dot(vt_ref[...], w_scr[...])

    @pl.when(e == n_e - 1)
    def _():
        y = x_ref[...] + acc_scr[...].T
        if final_norm:
            y = _rms(y, gf_ref[...])
        o_ref[...] = y


def _peer(x2d, g, g_final, wq_t_bf, k1_bf, k2_bf, u_bf, v_t_bf, *, layer, final_norm,
          tm=PEER_TOKEN_TILE, eb=PEER_EXPERT_BLOCK):
    n, d = x2d.shape
    n_tc = tm // LANES
    n_e = N_EXPERTS // eb
    const2 = lambda i, e: (0, 0)
    kern = functools.partial(_peer_kernel, tm=tm, eb=eb, final_norm=final_norm)
    n_tiles = PEER_HEADS * n_tc
    assert n_tiles % SUBLANES == 0
    tiles = (n_tiles * KEY_PITCH, LANES)
    per_key = (N_KEYS, SUBLANES, LANES)
    per_key_groups = (n_tiles // SUBLANES,) + per_key
    return pl.pallas_call(
        kern,
        grid=(n // tm, n_e),
        in_specs=[
            pl.BlockSpec((tm, d), lambda i, e: (i, 0)),
            pl.BlockSpec((1, d), const2),
            pl.BlockSpec((1, d), const2),
            pl.BlockSpec((PEER_HEADS * PEER_QUERY, d), const2),
            pl.BlockSpec((N_KEYS, PEER_HALF), const2),
            pl.BlockSpec((N_KEYS, PEER_HALF), const2),
            pl.BlockSpec((None, eb, d), lambda i, e: (layer, e, 0)),
            pl.BlockSpec((None, d, eb), lambda i, e: (layer, 0, e)),
        ],
        out_specs=pl.BlockSpec((tm, d), lambda i, e: (i, 0)),
        out_shape=jax.ShapeDtypeStruct((n, d), F32),
        scratch_shapes=[
            pltpu.VMEM((d, tm), BF16),
            pltpu.VMEM((d, tm), BF16),
            pltpu.VMEM(tiles, F32),
            pltpu.VMEM(tiles, F32),
            pltpu.VMEM(tiles, F32),
            pltpu.VMEM(per_key, F32),
            pltpu.VMEM(per_key, F32),
            pltpu.VMEM(per_key_groups, F32),
            pltpu.VMEM(per_key_groups, F32),
            pltpu.VMEM((eb, tm), F32),
            pltpu.VMEM((eb, tm), BF16),
            pltpu.VMEM((d, tm), F32),
        ],
        compiler_params=_params(("parallel", "arbitrary")),
        name="peer",
    )(x2d, g, g_final, wq_t_bf, k1_bf, k2_bf, u_bf, v_t_bf)


def _peer_tables_kernel(u_ref, v_ref, ub_ref, vt_ref):
    ub_ref[...] = u_ref[...].astype(BF16)
    vt_ref[0] = v_ref[0].T.astype(BF16)


def _peer_tables(peer_u, peer_v, *, tile=1024):
    n_layers, n_exp, d = peer_u.shape
    return pl.pallas_call(
        _peer_tables_kernel,
        grid=(n_layers, n_exp // tile),
        in_specs=[
            pl.BlockSpec((1, tile, d), lambda l, i: (l, i, 0)),
            pl.BlockSpec((1, tile, d), lambda l, i: (l, i, 0)),
        ],
        out_specs=[
            pl.BlockSpec((1, tile, d), lambda l, i: (l, i, 0)),
            pl.BlockSpec((1, d, tile), lambda l, i: (l, 0, i)),
        ],
        out_shape=[
            jax.ShapeDtypeStruct((n_layers, n_exp, d), BF16),
            jax.ShapeDtypeStruct((n_layers, d, n_exp), BF16),
        ],
        compiler_params=_params(("parallel", "parallel")),
        name="peer_tables",
    )(peer_u, peer_v)


def _trunk(x, h0_re, h0_im, w, *, scan_steps):
    bsz, t, d = x.shape
    n = bsz * t
    x2d = x.reshape(n, d)

    if bsz == SUBLANES and t % S5_ROW_TILE == 0:
        u_tm = _norm_matmul(x, w["norm_mix_g"][0], w["perm"], w["ssm_w_in"], tile=S5_ROW_TILE)
        g_tm, hr, hi = _s5_scan(u_tm, h0_re, h0_im, w["disc"], batch=bsz, steps=scan_steps)
        x2d = _glu_out(g_tm, x, w["perm"].T, w["ssm_w_glu"], w["ssm_b_glu"], w["ssm_w_out"],
                       tile=S5_ROW_TILE).reshape(n, d)
    else:
        u_bm = _norm_matmul(x2d[None], w["norm_mix_g"][0], w["perm"], w["ssm_w_in"], tile=n)
        u_tm = u_bm.reshape(bsz, t, d).transpose(1, 0, 2).reshape(n, d)
        g_tm, hr, hi = _s5_scan(u_tm, h0_re, h0_im, w["disc"], batch=bsz, steps=scan_steps)
        g_bm = g_tm.reshape(t, bsz, d).transpose(1, 0, 2).reshape(n, d)
        x2d = _glu_out(g_bm, x2d[None], w["perm"].T, w["ssm_w_glu"], w["ssm_b_glu"], w["ssm_w_out"],
                       tile=n).reshape(n, d)
    x2d = _peer(x2d, w["norm_ffn_g"][0], w["norm_final_g"], *w["peer"][0], *w["peer_tables"], layer=0,
                final_norm=False)

    if t % CHUNK == 0:
        mix, mix_bias = w["mix_full"]
    else:
        mix, mix_bias = w["mix_short"]
    x2d, v = _gmlp(x2d, w["norm_mix_g"][1], w["gmlp_w_in"], w["gmlp_b_in"], w["gmlp_ln_g"], w["gmlp_ln_b"],
                   mix, mix_bias, w["gmlp_w_out"], tile=GMLP_ROW_TILE)
    y2d = _peer(x2d, w["norm_ffn_g"][1], w["norm_final_g"], *w["peer"][1], *w["peer_tables"], layer=1,
                final_norm=True)
    return y2d.reshape(bsz, t, d), hr, hi, v


def kernel(x_prompt, x_sample, state_ssm_re, state_ssm_im, norm_mix_g, norm_ffn_g, norm_final_g, ssm_w_in, ssm_a_re, ssm_a_im, ssm_log_dt, ssm_b_re, ssm_b_im, ssm_c_re, ssm_c_im, ssm_d, ssm_w_glu, ssm_b_glu, ssm_w_out, gmlp_w_in, gmlp_b_in, gmlp_ln_g, gmlp_ln_b, gmlp_w_s, gmlp_b_s, gmlp_w_out, peer_w_q, peer_k1, peer_k2, peer_u, peer_v):
    bp, tp, d = x_prompt.shape
    bs, ts, _ = x_sample.shape
    assert tp % CHUNK == 0 and CHUNK % ts == 0 and d == D_MODEL

    tril = jnp.tril(jnp.ones((CHUNK, CHUNK), F32))
    ws = gmlp_w_s[0] * tril
    bias_full = jnp.broadcast_to(gmlp_b_s[0][:, :, None], (GMLP_HEADS, CHUNK, GMLP_HEAD_DIM))
    reps = CHUNK // ts
    ws_short = jnp.einsum("ab,hij->haibj", jnp.eye(reps, dtype=F32), ws[:, :ts, :ts]).reshape(GMLP_HEADS, CHUNK, CHUNK)
    bias_short = jnp.broadcast_to(jnp.tile(gmlp_b_s[0][:, :ts], (1, reps))[:, :, None],
                                  (GMLP_HEADS, CHUNK, GMLP_HEAD_DIM))

    w = {
        "norm_mix_g": norm_mix_g[:, None, :],
        "norm_ffn_g": norm_ffn_g[:, None, :],
        "norm_final_g": norm_final_g[None, :],
        "perm": _time_major_perm(),
        "ssm_w_in": ssm_w_in[0].astype(BF16),
        "disc": _s5_discretize(ssm_a_re[0], ssm_a_im[0], ssm_log_dt[0], ssm_b_re[0], ssm_b_im[0],
                               ssm_c_re[0], ssm_c_im[0], ssm_d[0]),
        "ssm_w_glu": ssm_w_glu[0].astype(BF16),
        "ssm_b_glu": ssm_b_glu[0][None, :],
        "ssm_w_out": ssm_w_out[0].astype(BF16),
        "gmlp_w_in": gmlp_w_in[0].astype(BF16),
        "gmlp_b_in": gmlp_b_in[0][None, :],
        "gmlp_ln_g": gmlp_ln_g[0][None, :],
        "gmlp_ln_b": gmlp_ln_b[0][None, :],
        "mix_full": (ws.astype(BF16), bias_full),
        "mix_short": (ws_short.astype(BF16), bias_short),
        "gmlp_w_out": gmlp_w_out[0].astype(BF16),
        "peer": [
            (peer_w_q[i].T.astype(BF16), peer_k1[i].astype(BF16), peer_k2[i].astype(BF16)) for i in range(2)
        ],
        "peer_tables": _peer_tables(peer_u, peer_v),
    }

    zeros = jnp.zeros((bp, SSM_COLS), F32)
    y_p, hr_p, hi_p, _ = _trunk(x_prompt, zeros, zeros, w, scan_steps=S5_SCAN_STEPS)
    y_s, hr_s, hi_s, v_s = _trunk(x_sample, state_ssm_re[0].reshape(bs, SSM_COLS),
                                  state_ssm_im[0].reshape(bs, SSM_COLS), w, scan_steps=ts)
    st = lambda a, b: a.reshape(1, b, SSM_GROUPS, SSM_STATE)
    return (y_p, y_s, st(hr_p, bp), st(hi_p, bp), st(hr_s, bs), st(hi_s, bs),
            v_s.reshape(1, bs, ts, GMLP_HALF))
```

```python
import functools
import math

import jax
import jax.numpy as jnp
from jax import lax
from jax.experimental import pallas as pl
from jax.experimental.pallas import tpu as pltpu

F32 = jnp.float32
BF16 = jnp.bfloat16

EPS = 1e-6
D_MODEL = 1024
SSM_GROUP = 16
SSM_GROUPS = 64
SSM_STATE = 64
SSM_COLS = SSM_GROUPS * SSM_STATE
GMLP_HALF = 2 * D_MODEL
GMLP_HEADS = 8
GMLP_HEAD_DIM = GMLP_HALF // GMLP_HEADS
CHUNK = 128
PEER_HEADS = 8
N_KEYS = 128
N_EXPERTS = N_KEYS * N_KEYS
PEER_QUERY = 256
PEER_HALF = 128
PEER_TOPK = 16

LANES = 128
SUBLANES = 8
MXU_DIM = 256
VMEM_LIMIT = 56 * 1024 * 1024

NEG_INF = float("-inf")
POS_INF = float("inf")
SQRT_HALF = math.sqrt(0.5)


def _dot(a, b):
    return jnp.dot(a, b, preferred_element_type=F32)


def _rms(x, g):
    ms = jnp.mean(x * x, axis=-1, keepdims=True)
    return x * lax.rsqrt(ms + EPS) * g


def _gelu(x):
    return 0.5 * x * (1.0 + lax.erf(x * SQRT_HALF))


def _params(sem):
    return pltpu.CompilerParams(dimension_semantics=sem, vmem_limit_bytes=VMEM_LIMIT)


S5_ROW_TILE = 128
S5_SCAN_STEPS = 64
GMLP_ROW_TILE = 512
PEER_TOKEN_TILE = 512
PEER_EXPERT_BLOCK = 2048


PERM_STEPS = MXU_DIM // SUBLANES


def _time_major_perm():
    r = jnp.arange(MXU_DIM)
    src = (r % SUBLANES) * PERM_STEPS + r // SUBLANES
    return (src[:, None] == r[None, :]).astype(BF16)


def _norm_matmul_kernel(x_ref, g_ref, p_ref, w_ref, o_ref, *, batch):
    tile = x_ref.shape[1]
    xn = [_rms(x_ref[b], g_ref[...]).astype(BF16) for b in range(batch)]
    if batch == 1:
        xp = xn[0]
    else:
        groups = []
        for tg in range(tile // PERM_STEPS):
            ts = slice(tg * PERM_STEPS, (tg + 1) * PERM_STEPS)
            rows_bt = jnp.concatenate([xn[b][ts] for b in range(batch)], axis=0)
            groups.append(_dot(p_ref[...], rows_bt).astype(BF16))
        xp = jnp.concatenate(groups, axis=0)
    o_ref[...] = _dot(xp, w_ref[...])


def _norm_matmul(x3d, g, perm, w_bf, *, tile):
    bsz, t, d = x3d.shape
    n = w_bf.shape[1]
    assert bsz == 1 or (bsz == SUBLANES and tile % PERM_STEPS == 0)
    return pl.pallas_call(
        functools.partial(_norm_matmul_kernel, batch=bsz),
        grid=(t // tile,),
        in_specs=[
            pl.BlockSpec((bsz, tile, d), lambda i: (0, i, 0)),
            pl.BlockSpec((1, d), lambda i: (0, 0)),
            pl.BlockSpec((MXU_DIM, MXU_DIM), lambda i: (0, 0)),
            pl.BlockSpec((d, n), lambda i: (0, 0)),
        ],
        out_specs=pl.BlockSpec((tile * bsz, n), lambda i: (i, 0)),
        out_shape=jax.ShapeDtypeStruct((t * bsz, n), F32),
        compiler_params=_params(("parallel",)),
        name="s5_in_proj",
    )(x3d, g, perm, w_bf)
SCAN_COLS = 512
N_SCAN_BLOCKS = SSM_COLS // SCAN_COLS
N_DIAG_BLOCKS = D_MODEL // MXU_DIM
DIAG_COLS = SSM_COLS // N_DIAG_BLOCKS
SCAN_PER_DIAG = DIAG_COLS // SCAN_COLS


def _s5_scan_kernel(u_ref, h0r_ref, h0i_ref, ar_ref, ai_ref, bbr_ref, bbi_ref, ccr_ref, cci_ref, d_ref,
                    g_ref, hr_ref, hi_ref, bur, bui, st_r, st_i, *, steps, batch):
    c = pl.program_id(0)
    n_sub = batch // SUBLANES

    @pl.when(c == 0)
    def _():
        for cb in range(N_SCAN_BLOCKS):
            st_r[cb] = h0r_ref[:, cb * SCAN_COLS:(cb + 1) * SCAN_COLS]
            st_i[cb] = h0i_ref[:, cb * SCAN_COLS:(cb + 1) * SCAN_COLS]

    u = u_ref[...]
    ub = u.astype(BF16)
    for kb in range(N_DIAG_BLOCKS):
        lhs = ub[:, kb * MXU_DIM:(kb + 1) * MXU_DIM]
        pr = _dot(lhs, bbr_ref[kb])
        pi = _dot(lhs, bbi_ref[kb])
        for j in range(SCAN_PER_DIAG):
            bur[kb * SCAN_PER_DIAG + j] = pr[:, j * SCAN_COLS:(j + 1) * SCAN_COLS]
            bui[kb * SCAN_PER_DIAG + j] = pi[:, j * SCAN_COLS:(j + 1) * SCAN_COLS]

    def scan_block(idx, carry):
        cb = idx // n_sub
        s = idx % n_sub
        ar = jnp.broadcast_to(ar_ref[cb], (SUBLANES, SCAN_COLS))
        ai = jnp.broadcast_to(ai_ref[cb], (SUBLANES, SCAN_COLS))
        row0 = pl.multiple_of(s * SUBLANES, SUBLANES)
        h_r = st_r[cb, pl.ds(row0, SUBLANES), :]
        h_i = st_i[cb, pl.ds(row0, SUBLANES), :]

        def step(t, h):
            hr, hi = h
            r = pl.multiple_of(t * batch + row0, SUBLANES)
            nr = ar * hr - ai * hi + bur[cb, pl.ds(r, SUBLANES), :]
            ni = ar * hi + ai * hr + bui[cb, pl.ds(r, SUBLANES), :]
            bur[cb, pl.ds(r, SUBLANES), :] = nr
            bui[cb, pl.ds(r, SUBLANES), :] = ni
            return nr, ni

        h_r, h_i = lax.fori_loop(0, steps, step, (h_r, h_i), unroll=min(steps, 8))
        st_r[cb, pl.ds(row0, SUBLANES), :] = h_r
        st_i[cb, pl.ds(row0, SUBLANES), :] = h_i
        return carry

    lax.fori_loop(0, N_SCAN_BLOCKS * n_sub, scan_block, 0)

    for cb in range(N_SCAN_BLOCKS):
        hr_ref[:, cb * SCAN_COLS:(cb + 1) * SCAN_COLS] = st_r[cb]
        hi_ref[:, cb * SCAN_COLS:(cb + 1) * SCAN_COLS] = st_i[cb]

    for nb in range(N_DIAG_BLOCKS):
        acc = None
        for j in range(SCAN_PER_DIAG):
            cb = nb * SCAN_PER_DIAG + j
            part = (_dot(bur[cb].astype(BF16), ccr_ref[nb, j * SCAN_COLS:(j + 1) * SCAN_COLS, :])
                    + _dot(bui[cb].astype(BF16), cci_ref[nb, j * SCAN_COLS:(j + 1) * SCAN_COLS, :]))
            acc = part if acc is None else acc + part
        cols = slice(nb * MXU_DIM, (nb + 1) * MXU_DIM)
        y = acc + d_ref[:, cols] * u[:, cols]
        g_ref[:, cols] = _gelu(y)


def _s5_scan(u_tm, h0_re, h0_im, disc, *, batch, steps):
    rows, d = u_tm.shape
    t = rows // batch
    n_chunks = t // steps
    blk = steps * batch
    ar, ai, bbr, bbi, ccr, cci, dsk = disc
    const2 = lambda c: (0, 0)
    const3 = lambda c: (0, 0, 0)
    kern = functools.partial(_s5_scan_kernel, steps=steps, batch=batch)
    return pl.pallas_call(
        kern,
        grid=(n_chunks,),
        in_specs=[
            pl.BlockSpec((blk, d), lambda c: (c, 0)),
            pl.BlockSpec((batch, SSM_COLS), const2),
            pl.BlockSpec((batch, SSM_COLS), const2),
            pl.BlockSpec((N_SCAN_BLOCKS, 1, SCAN_COLS), const3),
            pl.BlockSpec((N_SCAN_BLOCKS, 1, SCAN_COLS), const3),
            pl.BlockSpec((N_DIAG_BLOCKS, MXU_DIM, DIAG_COLS), const3),
            pl.BlockSpec((N_DIAG_BLOCKS, MXU_DIM, DIAG_COLS), const3),
            pl.BlockSpec((N_DIAG_BLOCKS, DIAG_COLS, MXU_DIM), const3),
            pl.BlockSpec((N_DIAG_BLOCKS, DIAG_COLS, MXU_DIM), const3),
            pl.BlockSpec((1, d), const2),
        ],
        out_specs=[
            pl.BlockSpec((blk, d), lambda c: (c, 0)),
            pl.BlockSpec((batch, SSM_COLS), const2),
            pl.BlockSpec((batch, SSM_COLS), const2),
        ],
        out_shape=[
            jax.ShapeDtypeStruct((rows, d), F32),
            jax.ShapeDtypeStruct((batch, SSM_COLS), F32),
            jax.ShapeDtypeStruct((batch, SSM_COLS), F32),
        ],
        scratch_shapes=[
            pltpu.VMEM((N_SCAN_BLOCKS, blk, SCAN_COLS), F32),
            pltpu.VMEM((N_SCAN_BLOCKS, blk, SCAN_COLS), F32),
            pltpu.VMEM((N_SCAN_BLOCKS, batch, SCAN_COLS), F32),
            pltpu.VMEM((N_SCAN_BLOCKS, batch, SCAN_COLS), F32),
        ],
        compiler_params=_params(("arbitrary",)),
        name="s5_scan",
    )(u_tm, h0_re, h0_im, ar, ai, bbr, bbi, ccr, cci, dsk)


def _glu_out_kernel(g_ref, x_ref, pt_ref, wg_ref, bg_ref, wo_ref, o_ref, *, batch):
    tile = x_ref.shape[1]
    g = g_ref[...]
    z = _dot(g.astype(BF16), wg_ref[...]) + bg_ref[...]
    o = (g * jax.nn.sigmoid(z)).astype(BF16)
    if batch == 1:
        o_ref[0] = x_ref[0] + _dot(o, wo_ref[...])
    else:
        n_groups = tile // PERM_STEPS
        groups = [_dot(pt_ref[...], o[tg * MXU_DIM:(tg + 1) * MXU_DIM]).astype(BF16) for tg in range(n_groups)]
        mix = _dot(jnp.concatenate(groups, axis=0), wo_ref[...])
        for tg in range(n_groups):
            for b in range(batch):
                ts = slice(tg * PERM_STEPS, (tg + 1) * PERM_STEPS)
                r0 = tg * MXU_DIM + b * PERM_STEPS
                o_ref[b, ts, :] = x_ref[b, ts, :] + mix[r0:r0 + PERM_STEPS]


def _glu_out(g_tm, x3d, perm_t, w_glu_bf, b_glu, w_out_bf, *, tile):
    bsz, t, d = x3d.shape
    assert bsz == 1 or (bsz == SUBLANES and tile % PERM_STEPS == 0)
    const2 = lambda i: (0, 0)
    return pl.pallas_call(
        functools.partial(_glu_out_kernel, batch=bsz),
        grid=(t // tile,),
        in_specs=[
            pl.BlockSpec((tile * bsz, d), lambda i: (i, 0)),
            pl.BlockSpec((bsz, tile, d), lambda i: (0, i, 0)),
            pl.BlockSpec((MXU_DIM, MXU_DIM), const2),
            pl.BlockSpec((d, d), const2),
            pl.BlockSpec((1, d), const2),
            pl.BlockSpec((d, d), const2),
        ],
        out_specs=pl.BlockSpec((bsz, tile, d), lambda i: (0, i, 0)),
        out_shape=jax.ShapeDtypeStruct((bsz, t, d), F32),
        compiler_params=_params(("parallel",)),
        name="s5_glu_out",
    )(g_tm, x3d, perm_t, w_glu_bf, b_glu, w_out_bf)


def _s5_discretize(a_re, a_im, log_dt, b_re, b_im, c_re, c_im, d_skip):
    dt = jnp.exp(log_dt)[:, None]
    mag = jnp.exp(a_re * dt)
    abar_re = mag * jnp.cos(a_im * dt)
    abar_im = mag * jnp.sin(a_im * dt)
    num_re = abar_re - 1.0
    num_im = abar_im
    den = a_re * a_re + a_im * a_im
    f_re = (num_re * a_re + num_im * a_im) / den
    f_im = (num_im * a_re - num_re * a_im) / den
    bb_re = f_re[..., None] * b_re - f_im[..., None] * b_im
    bb_im = f_re[..., None] * b_im + f_im[..., None] * b_re
    gpb = MXU_DIM // SSM_GROUP
    eye = jnp.eye(gpb, dtype=F32)

    def in_blocks(bb):
        bt = jnp.transpose(bb, (0, 2, 1)).reshape(N_DIAG_BLOCKS, gpb, SSM_GROUP, SSM_STATE)
        full = bt[:, :, :, None, :] * eye[None, :, None, :, None]
        return full.reshape(N_DIAG_BLOCKS, MXU_DIM, DIAG_COLS).astype(BF16)

    def out_blocks(cc):
        ct = jnp.transpose(cc, (0, 2, 1)).reshape(N_DIAG_BLOCKS, gpb, SSM_STATE, SSM_GROUP)
        full = ct[:, :, :, None, :] * eye[None, :, None, :, None]
        return full.reshape(N_DIAG_BLOCKS, DIAG_COLS, MXU_DIM).astype(BF16)

    ar = abar_re.reshape(N_SCAN_BLOCKS, 1, SCAN_COLS)
    ai = abar_im.reshape(N_SCAN_BLOCKS, 1, SCAN_COLS)
    return (ar, ai, in_blocks(bb_re), in_blocks(bb_im), out_blocks(c_re), out_blocks(-c_im),
            d_skip.reshape(1, D_MODEL))


def _gmlp_kernel(x_ref, g_ref, win_ref, bin_ref, lng_ref, lnb_ref, mix_ref, mixb_ref, wout_ref,
                 o_ref, v_ref, s_scr, *, rows):
    x = x_ref[...]
    hn = _rms(x, g_ref[...]).astype(BF16)
    z = _gelu(_dot(hn, win_ref[...]) + bin_ref[...])
    u = z[:, :GMLP_HALF]
    v = z[:, GMLP_HALF:]
    mu = jnp.mean(v, axis=-1, keepdims=True)
    vc = v - mu
    vn = vc * lax.rsqrt(jnp.mean(vc * vc, axis=-1, keepdims=True) + EPS) * lng_ref[...] + lnb_ref[...]
    v_ref[...] = vn
    vb = vn.astype(BF16)
    for r in range(rows // CHUNK):
        rs = slice(r * CHUNK, (r + 1) * CHUNK)
        for h in range(GMLP_HEADS):
            cs = slice(h * GMLP_HEAD_DIM, (h + 1) * GMLP_HEAD_DIM)
            mixed = _dot(mix_ref[h], vb[rs, cs]) + mixb_ref[h]
            s_scr[rs, cs] = (u[rs, cs] * mixed).astype(BF16)
    o_ref[...] = x + _dot(s_scr[...], wout_ref[...])


def _gmlp(x2d, g, w_in_bf, b_in, ln_g, ln_b, mix_bf, mix_bias, w_out_bf, *, tile):
    rows, d = x2d.shape
    const2 = lambda i: (0, 0)
    const3 = lambda i: (0, 0, 0)
    kern = functools.partial(_gmlp_kernel, rows=tile)
    return pl.pallas_call(
        kern,
        grid=(rows // tile,),
        in_specs=[
            pl.BlockSpec((tile, d), lambda i: (i, 0)),
            pl.BlockSpec((1, d), const2),
            pl.BlockSpec((d, 2 * GMLP_HALF), const2),
            pl.BlockSpec((1, 2 * GMLP_HALF), const2),
            pl.BlockSpec((1, GMLP_HALF), const2),
            pl.BlockSpec((1, GMLP_HALF), const2),
            pl.BlockSpec((GMLP_HEADS, CHUNK, CHUNK), const3),
            pl.BlockSpec((GMLP_HEADS, CHUNK, GMLP_HEAD_DIM), const3),
            pl.BlockSpec((GMLP_HALF, d), const2),
        ],
        out_specs=[
            pl.BlockSpec((tile, d), lambda i: (i, 0)),
            pl.BlockSpec((tile, GMLP_HALF), lambda i: (i, 0)),
        ],
        out_shape=[
            jax.ShapeDtypeStruct((rows, d), F32),
            jax.ShapeDtypeStruct((rows, GMLP_HALF), F32),
        ],
        scratch_shapes=[pltpu.VMEM((tile, GMLP_HALF), BF16)],
        compiler_params=_params(("parallel",)),
        name="gmlp",
    )(x2d, g, w_in_bf, b_in, ln_g, ln_b, mix_bf, mix_bias, w_out_bf)


KEY_PITCH = N_KEYS + SUBLANES
KEYS_PER_TRIP = 2


def _sorting_network(n):
    pairs = []
    p = 1
    while p < n:
        k = p
        while k >= 1:
            for j in range(k % p, n - k, 2 * k):
                for i in range(min(k, n - j - k)):
                    if (i + j) // (2 * p) == (i + j + k) // (2 * p):
                        pairs.append((i + j, i + j + k))
            k //= 2
        p *= 2
    return pairs


SORT16 = _sorting_network(PEER_TOPK)


def _compare_exchange(v, i, j):
    v[i], v[j] = jnp.maximum(v[i], v[j]), jnp.minimum(v[i], v[j])


def _sort16_desc(v):
    v = list(v)
    for i, j in SORT16:
        _compare_exchange(v, i, j)
    return v


def _merge_top16(a, b):
    c = [jnp.maximum(a[i], b[PEER_TOPK - 1 - i]) for i in range(PEER_TOPK)]
    d = PEER_TOPK // 2
    while d >= 1:
        for i in range(PEER_TOPK):
            if i & d == 0:
                _compare_exchange(c, i, i + d)
        d //= 2
    return c


def _top16_of(values):
    lists = [_sort16_desc(values[i:i + PEER_TOPK]) for i in range(0, len(values), PEER_TOPK)]
    while len(lists) > 1:
        merged = [_merge_top16(lists[i], lists[i + 1]) for i in range(0, len(lists) - 1, 2)]
        if len(lists) % 2:
            merged.append(lists[-1])
        lists = merged
    return lists[0]


def _peer_thresholds(s1_t, s2_t, e2_t, s1_v, s2_v, th_v, c1_v, grp):
    base = grp * (SUBLANES * KEY_PITCH)

    def key_rows(k):
        return pl.ds(base + k, SUBLANES, stride=KEY_PITCH)

    for k in range(N_KEYS):
        s1_v[k] = s1_t[key_rows(k), :]
        s2_v[k] = s2_t[key_rows(k), :]
    v1 = _top16_of([s1_v[k] for k in range(N_KEYS)])
    v2 = _top16_of([s2_v[k] for k in range(N_KEYS)])

    pairs = [(a, b) for a in range(PEER_TOPK) for b in range(PEER_TOPK // (a + 1))]
    cand = {ab: v1[ab[0]] + v2[ab[1]] for ab in pairs}
    first_row = [cand[(0, b)] for b in range(PEER_TOPK)]
    rest = [cand[ab] for ab in pairs if ab[0] > 0]
    rest += [jnp.full_like(v1[0], NEG_INF)] * (-len(rest) % PEER_TOPK)
    tau = _merge_top16(first_row, _top16_of(rest))[PEER_TOPK - 1]

    cmax = cand[(0, 0)]
    z = jnp.zeros_like(tau)
    th_rank = [jnp.full_like(tau, POS_INF) for _ in range(PEER_TOPK)]
    for a, b in pairs:
        sel = cand[(a, b)] >= tau
        z = z + jnp.where(sel, jnp.exp(cand[(a, b)] - cmax), 0.0)
        th_rank[a] = jnp.where(sel, v2[b], th_rank[a])
    c_scale = SQRT_HALF / z

    def per_key(k, carry):
        s1k = s1_v[k]
        th = jnp.full_like(s1k, POS_INF)
        for a in reversed(range(PEER_TOPK)):
            th = jnp.where(s1k >= v1[a], th_rank[a], th)
        th_v[grp, k] = th
        c1_v[grp, k] = jnp.exp(s1k - v1[0]) * c_scale
        e2_t[key_rows(k), :] = jnp.exp(s2_v[k] - v2[0])
        return carry

    lax.fori_loop(0, N_KEYS, per_key, 0, unroll=4)


def _peer_kernel(x_ref, g_ref, gf_ref, wq_ref, k1_ref, k2_ref, u_ref, vt_ref, o_ref,
                 xn_scr, xs_scr, s1_t, s2_t, e2_t, s1_v, s2_v, th_v, c1_v, h_scr, w_scr, acc_scr,
                 *, tm, eb, final_norm):
    e = pl.program_id(1)
    n_e = pl.num_programs(1)
    n_tc = tm // LANES
    keys_per_block = eb // N_KEYS
    n_groups = PEER_HEADS * n_tc // SUBLANES

    def tile_row(h, tc):
        return (h * n_tc + tc) * KEY_PITCH

    @pl.when(e == 0)
    def prepare():
        xn_t = _rms(x_ref[...], g_ref[...]).T
        xn_scr[...] = xn_t.astype(BF16)
        xs_scr[...] = (xn_t * SQRT_HALF).astype(BF16)
        for h in range(PEER_HEADS):
            qt = _dot(wq_ref[h * PEER_QUERY:(h + 1) * PEER_QUERY, :], xn_scr[...])
            s1 = _dot(k1_ref[...], qt[:PEER_HALF].astype(BF16))
            s2 = _dot(k2_ref[...], qt[PEER_HALF:].astype(BF16))
            for tc in range(n_tc):
                rows = pl.ds(tile_row(h, tc), N_KEYS)
                s1_t[rows, :] = s1[:, tc * LANES:(tc + 1) * LANES]
                s2_t[rows, :] = s2[:, tc * LANES:(tc + 1) * LANES]

        def thresholds(grp, carry):
            _peer_thresholds(s1_t, s2_t, e2_t, s1_v, s2_v, th_v, c1_v, grp)
            return carry

        lax.fori_loop(0, n_groups, thresholds, 0)
        acc_scr[...] = jnp.zeros_like(acc_scr)

    hs = _dot(u_ref[...], xs_scr[...])
    h_scr[...] = hs * (1.0 + lax.erf(hs))

    def key_rows(p, carry):
        for tc in range(n_tc):
            lanes = slice(tc * LANES, (tc + 1) * LANES)
            gates = [jnp.zeros((N_KEYS, LANES), F32) for _ in range(KEYS_PER_TRIP)]
            for h in range(PEER_HEADS):
                grp, j = divmod(h * n_tc + tc, SUBLANES)
                tile = pl.ds(tile_row(h, tc), N_KEYS)
                s2 = s2_t[tile, :]
                e2 = e2_t[tile, :]
                for k in range(KEYS_PER_TRIP):
                    i1 = e * keys_per_block + p * KEYS_PER_TRIP + k
                    th = th_v[grp, i1, j:j + 1, :]
                    c1 = c1_v[grp, i1, j:j + 1, :]
                    gates[k] = gates[k] + jnp.where(s2 >= th, c1 * e2, 0.0)
            for k in range(KEYS_PER_TRIP):
                rows = pl.ds(pl.multiple_of((p * KEYS_PER_TRIP + k) * N_KEYS, N_KEYS), N_KEYS)
                w_scr[rows, lanes] = (gates[k] * h_scr[rows, lanes]).astype(BF16)
        return carry

    lax.fori_loop(0, keys_per_block // KEYS_PER_TRIP, key_rows, 0)
    acc_scr[...] += _dot(vt_ref[...], w_scr[...])

    @pl.when(e == n_e - 1)
    def _():
        y = x_ref[...] + acc_scr[...].T
        if final_norm:
            y = _rms(y, gf_ref[...])
        o_ref[...] = y


def _peer(x2d, g, g_final, wq_t_bf, k1_bf, k2_bf, u_bf, v_t_bf, *, layer, final_norm,
          tm=PEER_TOKEN_TILE, eb=PEER_EXPERT_BLOCK):
    n, d = x2d.shape
    n_tc = tm // LANES
    n_e = N_EXPERTS // eb
    const2 = lambda i, e: (0, 0)
    kern = functools.partial(_peer_kernel, tm=tm, eb=eb, final_norm=final_norm)
    n_tiles = PEER_HEADS * n_tc
    assert n_tiles % SUBLANES == 0
    tiles = (n_tiles * KEY_PITCH, LANES)
    per_key = (N_KEYS, SUBLANES, LANES)
    per_key_groups = (n_tiles // SUBLANES,) + per_key
    return pl.pallas_call(
        kern,
        grid=(n // tm, n_e),
        in_specs=[
            pl.BlockSpec((tm, d), lambda i, e: (i, 0)),
            pl.BlockSpec((1, d), const2),
            pl.BlockSpec((1, d), const2),
            pl.BlockSpec((PEER_HEADS * PEER_QUERY, d), const2),
            pl.BlockSpec((N_KEYS, PEER_HALF), const2),
            pl.BlockSpec((N_KEYS, PEER_HALF), const2),
            pl.BlockSpec((None, eb, d), lambda i, e: (layer, e, 0)),
            pl.BlockSpec((None, d, eb), lambda i, e: (layer, 0, e)),
        ],
        out_specs=pl.BlockSpec((tm, d), lambda i, e: (i, 0)),
        out_shape=jax.ShapeDtypeStruct((n, d), F32),
        scratch_shapes=[
            pltpu.VMEM((d, tm), BF16),
            pltpu.VMEM((d, tm), BF16),
            pltpu.VMEM(tiles, F32),
            pltpu.VMEM(tiles, F32),
            pltpu.VMEM(tiles, F32),
            pltpu.VMEM(per_key, F32),
            pltpu.VMEM(per_key, F32),
            pltpu.VMEM(per_key_groups, F32),
            pltpu.VMEM(per_key_groups, F32),
            pltpu.VMEM((eb, tm), F32),
            pltpu.VMEM((eb, tm), BF16),
            pltpu.VMEM((d, tm), F32),
        ],
        compiler_params=_params(("parallel", "arbitrary")),
        name="peer",
    )(x2d, g, g_final, wq_t_bf, k1_bf, k2_bf, u_bf, v_t_bf)


def _peer_tables_kernel(u_ref, v_ref, ub_ref, vt_ref):
    ub_ref[...] = u_ref[...].astype(BF16)
    vt_ref[0] = v_ref[0].T.astype(BF16)


def _peer_tables(peer_u, peer_v, *, tile=1024):
    n_layers, n_exp, d = peer_u.shape
    return pl.pallas_call(
        _peer_tables_kernel,
        grid=(n_layers, n_exp // tile),
        in_specs=[
            pl.BlockSpec((1, tile, d), lambda l, i: (l, i, 0)),
            pl.BlockSpec((1, tile, d), lambda l, i: (l, i, 0)),
        ],
        out_specs=[
            pl.BlockSpec((1, tile, d), lambda l, i: (l, i, 0)),
            pl.BlockSpec((1, d, tile), lambda l, i: (l, 0, i)),
        ],
        out_shape=[
            jax.ShapeDtypeStruct((n_layers, n_exp, d), BF16),
            jax.ShapeDtypeStruct((n_layers, d, n_exp), BF16),
        ],
        compiler_params=_params(("parallel", "parallel")),
        name="peer_tables",
    )(peer_u, peer_v)


def _trunk(x, h0_re, h0_im, w, *, scan_steps):
    bsz, t, d = x.shape
    n = bsz * t
    x2d = x.reshape(n, d)

    if bsz == SUBLANES and t % S5_ROW_TILE == 0:
        u_tm = _norm_matmul(x, w["norm_mix_g"][0], w["perm"], w["ssm_w_in"], tile=S5_ROW_TILE)
        g_tm, hr, hi = _s5_scan(u_tm, h0_re, h0_im, w["disc"], batch=bsz, steps=scan_steps)
        x2d = _glu_out(g_tm, x, w["perm"].T, w["ssm_w_glu"], w["ssm_b_glu"], w["ssm_w_out"],
                       tile=S5_ROW_TILE).reshape(n, d)
    else:
        u_bm = _norm_matmul(x2d[None], w["norm_mix_g"][0], w["perm"], w["ssm_w_in"], tile=n)
        u_tm = u_bm.reshape(bsz, t, d).transpose(1, 0, 2).reshape(n, d)
        g_tm, hr, hi = _s5_scan(u_tm, h0_re, h0_im, w["disc"], batch=bsz, steps=scan_steps)
        g_bm = g_tm.reshape(t, bsz, d).transpose(1, 0, 2).reshape(n, d)
        x2d = _glu_out(g_bm, x2d[None], w["perm"].T, w["ssm_w_glu"], w["ssm_b_glu"], w["ssm_w_out"],
                       tile=n).reshape(n, d)
    x2d = _peer(x2d, w["norm_ffn_g"][0], w["norm_final_g"], *w["peer"][0], *w["peer_tables"], layer=0,
                final_norm=False)

    if t % CHUNK == 0:
        mix, mix_bias = w["mix_full"]
    else:
        mix, mix_bias = w["mix_short"]
    x2d, v = _gmlp(x2d, w["norm_mix_g"][1], w["gmlp_w_in"], w["gmlp_b_in"], w["gmlp_ln_g"], w["gmlp_ln_b"],
                   mix, mix_bias, w["gmlp_w_out"], tile=GMLP_ROW_TILE)
    y2d = _peer(x2d, w["norm_ffn_g"][1], w["norm_final_g"], *w["peer"][1], *w["peer_tables"], layer=1,
                final_norm=True)
    return y2d.reshape(bsz, t, d), hr, hi, v


def kernel(x_prompt, x_sample, state_ssm_re, state_ssm_im, norm_mix_g, norm_ffn_g, norm_final_g, ssm_w_in, ssm_a_re, ssm_a_im, ssm_log_dt, ssm_b_re, ssm_b_im, ssm_c_re, ssm_c_im, ssm_d, ssm_w_glu, ssm_b_glu, ssm_w_out, gmlp_w_in, gmlp_b_in, gmlp_ln_g, gmlp_ln_b, gmlp_w_s, gmlp_b_s, gmlp_w_out, peer_w_q, peer_k1, peer_k2, peer_u, peer_v):
    bp, tp, d = x_prompt.shape
    bs, ts, _ = x_sample.shape
    assert tp % CHUNK == 0 and CHUNK % ts == 0 and d == D_MODEL

    tril = jnp.tril(jnp.ones((CHUNK, CHUNK), F32))
    ws = gmlp_w_s[0] * tril
    bias_full = jnp.broadcast_to(gmlp_b_s[0][:, :, None], (GMLP_HEADS, CHUNK, GMLP_HEAD_DIM))
    reps = CHUNK // ts
    ws_short = jnp.einsum("ab,hij->haibj", jnp.eye(reps, dtype=F32), ws[:, :ts, :ts]).reshape(GMLP_HEADS, CHUNK, CHUNK)
    bias_short = jnp.broadcast_to(jnp.tile(gmlp_b_s[0][:, :ts], (1, reps))[:, :, None],
                                  (GMLP_HEADS, CHUNK, GMLP_HEAD_DIM))

    w = {
        "norm_mix_g": norm_mix_g[:, None, :],
        "norm_ffn_g": norm_ffn_g[:, None, :],
        "norm_final_g": norm_final_g[None, :],
        "perm": _time_major_perm(),
        "ssm_w_in": ssm_w_in[0].astype(BF16),
        "disc": _s5_discretize(ssm_a_re[0], ssm_a_im[0], ssm_log_dt[0], ssm_b_re[0], ssm_b_im[0],
                               ssm_c_re[0], ssm_c_im[0], ssm_d[0]),
        "ssm_w_glu": ssm_w_glu[0].astype(BF16),
        "ssm_b_glu": ssm_b_glu[0][None, :],
        "ssm_w_out": ssm_w_out[0].astype(BF16),
        "gmlp_w_in": gmlp_w_in[0].astype(BF16),
        "gmlp_b_in": gmlp_b_in[0][None, :],
        "gmlp_ln_g": gmlp_ln_g[0][None, :],
        "gmlp_ln_b": gmlp_ln_b[0][None, :],
        "mix_full": (ws.astype(BF16), bias_full),
        "mix_short": (ws_short.astype(BF16), bias_short),
        "gmlp_w_out": gmlp_w_out[0].astype(BF16),
        "peer": [
            (peer_w_q[i].T.astype(BF16), peer_k1[i].astype(BF16), peer_k2[i].astype(BF16)) for i in range(2)
        ],
        "peer_tables": _peer_tables(peer_u, peer_v),
    }

    zeros = jnp.zeros((bp, SSM_COLS), F32)
    y_p, hr_p, hi_p, _ = _trunk(x_prompt, zeros, zeros, w, scan_steps=S5_SCAN_STEPS)
    y_s, hr_s, hi_s, v_s = _trunk(x_sample, state_ssm_re[0].reshape(bs, SSM_COLS),
                                  state_ssm_im[0].reshape(bs, SSM_COLS), w, scan_steps=ts)
    st = lambda a, b: a.reshape(1, b, SSM_GROUPS, SSM_STATE)
    return (y_p, y_s, st(hr_p, bp), st(hi_p, bp), st(hr_s, bs), st(hi_s, bs),
            v_s.reshape(1, bs, ts, GMLP_HALF))
```

```python
import functools
import math

import jax
import jax.numpy as jnp
from jax import lax
from jax.experimental import pallas as pl
from jax.experimental.pallas import tpu as pltpu

F32 = jnp.float32
BF16 = jnp.bfloat16

EPS = 1e-6
D_MODEL = 1024
SSM_GROUP = 16
SSM_GROUPS = 64
SSM_STATE = 64
SSM_COLS = SSM_GROUPS * SSM_STATE
GMLP_HALF = 2 * D_MODEL
GMLP_HEADS = 8
GMLP_HEAD_DIM = GMLP_HALF // GMLP_HEADS
CHUNK = 128
PEER_HEADS = 8
N_KEYS = 128
N_EXPERTS = N_KEYS * N_KEYS
PEER_QUERY = 256
PEER_HALF = 128
PEER_TOPK = 16

LANES = 128
SUBLANES = 8
MXU_DIM = 256
VMEM_LIMIT = 56 * 1024 * 1024

NEG_INF = float("-inf")
POS_INF = float("inf")
SQRT_HALF = math.sqrt(0.5)


def _dot(a, b):
    return jnp.dot(a, b, preferred_element_type=F32)


def _rms(x, g):
    ms = jnp.mean(x * x, axis=-1, keepdims=True)
    return x * lax.rsqrt(ms + EPS) * g


def _gelu(x):
    return 0.5 * x * (1.0 + lax.erf(x * SQRT_HALF))


def _params(sem):
    return pltpu.CompilerParams(dimension_semantics=sem, vmem_limit_bytes=VMEM_LIMIT)


S5_ROW_TILE = 128
S5_SCAN_STEPS = 64
GMLP_ROW_TILE = 512
PEER_TOKEN_TILE = 512
PEER_EXPERT_BLOCK = 2048


PERM_STEPS = MXU_DIM // SUBLANES


def _time_major_perm():
    r = jnp.arange(MXU_DIM)
    src = (r % SUBLANES) * PERM_STEPS + r // SUBLANES
    return (src[:, None] == r[None, :]).astype(BF16)


def _norm_matmul_kernel(x_ref, g_ref, p_ref, w_ref, o_ref, *, batch):
    tile = x_ref.shape[1]
    xn = [_rms(x_ref[b], g_ref[...]).astype(BF16) for b in range(batch)]
    if batch == 1:
        xp = xn[0]
    else:
        groups = []
        for tg in range(tile // PERM_STEPS):
            ts = slice(tg * PERM_STEPS, (tg + 1) * PERM_STEPS)
            rows_bt = jnp.concatenate([xn[b][ts] for b in range(batch)], axis=0)
            groups.append(_dot(p_ref[...], rows_bt).astype(BF16))
        xp = jnp.concatenate(groups, axis=0)
    o_ref[...] = _dot(xp, w_ref[...])


def _norm_matmul(x3d, g, perm, w_bf, *, tile):
    bsz, t, d = x3d.shape
    n = w_bf.shape[1]
    assert bsz == 1 or (bsz == SUBLANES and tile % PERM_STEPS == 0)
    return pl.pallas_call(
        functools.partial(_norm_matmul_kernel, batch=bsz),
        grid=(t // tile,),
        in_specs=[
            pl.BlockSpec((bsz, tile, d), lambda i: (0, i, 0)),
            pl.BlockSpec((1, d), lambda i: (0, 0)),
            pl.BlockSpec((MXU_DIM, MXU_DIM), lambda i: (0, 0)),
            pl.BlockSpec((d, n), lambda i: (0, 0)),
        ],
        out_specs=pl.BlockSpec((tile * bsz, n), lambda i: (i, 0)),
        out_shape=jax.ShapeDtypeStruct((t * bsz, n), F32),
        compiler_params=_params(("parallel",)),
        name="s5_in_proj",
    )(x3d, g, perm, w_bf)
SCAN_COLS = 512
N_SCAN_BLOCKS = SSM_COLS // SCAN_COLS
N_DIAG_BLOCKS = D_MODEL // MXU_DIM
DIAG_COLS = SSM_COLS // N_DIAG_BLOCKS
SCAN_PER_DIAG = DIAG_COLS // SCAN_COLS


def _s5_scan_kernel(u_ref, h0r_ref, h0i_ref, ar_ref, ai_ref, bbr_ref, bbi_ref, ccr_ref, cci_ref, d_ref,
                    g_ref, hr_ref, hi_ref, bur, bui, st_r, st_i, *, steps, batch):
    c = pl.program_id(0)
    n_sub = batch // SUBLANES

    @pl.when(c == 0)
    def _():
        for cb in range(N_SCAN_BLOCKS):
            st_r[cb] = h0r_ref[:, cb * SCAN_COLS:(cb + 1) * SCAN_COLS]
            st_i[cb] = h0i_ref[:, cb * SCAN_COLS:(cb + 1) * SCAN_COLS]

    u = u_ref[...]
    ub = u.astype(BF16)
    for kb in range(N_DIAG_BLOCKS):
        lhs = ub[:, kb * MXU_DIM:(kb + 1) * MXU_DIM]
        pr = _dot(lhs, bbr_ref[kb])
        pi = _dot(lhs, bbi_ref[kb])
        for j in range(SCAN_PER_DIAG):
            bur[kb * SCAN_PER_DIAG + j] = pr[:, j * SCAN_COLS:(j + 1) * SCAN_COLS]
            bui[kb * SCAN_PER_DIAG + j] = pi[:, j * SCAN_COLS:(j + 1) * SCAN_COLS]

    def scan_block(idx, carry):
        cb = idx // n_sub
        s = idx % n_sub
        ar = jnp.broadcast_to(ar_ref[cb], (SUBLANES, SCAN_COLS))
        ai = jnp.broadcast_to(ai_ref[cb], (SUBLANES, SCAN_COLS))
        row0 = pl.multiple_of(s * SUBLANES, SUBLANES)
        h_r = st_r[cb, pl.ds(row0, SUBLANES), :]
        h_i = st_i[cb, pl.ds(row0, SUBLANES), :]

        def step(t, h):
            hr, hi = h
            r = pl.multiple_of(t * batch + row0, SUBLANES)
            nr = ar * hr - ai * hi + bur[cb, pl.ds(r, SUBLANES), :]
            ni = ar * hi + ai * hr + bui[cb, pl.ds(r, SUBLANES), :]
            bur[cb, pl.ds(r, SUBLANES), :] = nr
            bui[cb, pl.ds(r, SUBLANES), :] = ni
            return nr, ni

        h_r, h_i = lax.fori_loop(0, steps, step, (h_r, h_i), unroll=min(steps, 8))
        st_r[cb, pl.ds(row0, SUBLANES), :] = h_r
        st_i[cb, pl.ds(row0, SUBLANES), :] = h_i
        return carry

    lax.fori_loop(0, N_SCAN_BLOCKS * n_sub, scan_block, 0)

    for cb in range(N_SCAN_BLOCKS):
        hr_ref[:, cb * SCAN_COLS:(cb + 1) * SCAN_COLS] = st_r[cb]
        hi_ref[:, cb * SCAN_COLS:(cb + 1) * SCAN_COLS] = st_i[cb]

    for nb in range(N_DIAG_BLOCKS):
        acc = None
        for j in range(SCAN_PER_DIAG):
            cb = nb * SCAN_PER_DIAG + j
            part = (_dot(bur[cb].astype(BF16), ccr_ref[nb, j * SCAN_COLS:(j + 1) * SCAN_COLS, :])
                    + _dot(bui[cb].astype(BF16), cci_ref[nb, j * SCAN_COLS:(j + 1) * SCAN_COLS, :]))
            acc = part if acc is None else acc + part
        cols = slice(nb * MXU_DIM, (nb + 1) * MXU_DIM)
        y = acc + d_ref[:, cols] * u[:, cols]
        g_ref[:, cols] = _gelu(y)


def _s5_scan(u_tm, h0_re, h0_im, disc, *, batch, steps):
    rows, d = u_tm.shape
    t = rows // batch
    n_chunks = t // steps
    blk = steps * batch
    ar, ai, bbr, bbi, ccr, cci, dsk = disc
    const2 = lambda c: (0, 0)
    const3 = lambda c: (0, 0, 0)
    kern = functools.partial(_s5_scan_kernel, steps=steps, batch=batch)
    return pl.pallas_call(
        kern,
        grid=(n_chunks,),
        in_specs=[
            pl.BlockSpec((blk, d), lambda c: (c, 0)),
            pl.BlockSpec((batch, SSM_COLS), const2),
            pl.BlockSpec((batch, SSM_COLS), const2),
            pl.BlockSpec((N_SCAN_BLOCKS, 1, SCAN_COLS), const3),
            pl.BlockSpec((N_SCAN_BLOCKS, 1, SCAN_COLS), const3),
            pl.BlockSpec((N_DIAG_BLOCKS, MXU_DIM, DIAG_COLS), const3),
            pl.BlockSpec((N_DIAG_BLOCKS, MXU_DIM, DIAG_COLS), const3),
            pl.BlockSpec((N_DIAG_BLOCKS, DIAG_COLS, MXU_DIM), const3),
            pl.BlockSpec((N_DIAG_BLOCKS, DIAG_COLS, MXU_DIM), const3),
            pl.BlockSpec((1, d), const2),
        ],
        out_specs=[
            pl.BlockSpec((blk, d), lambda c: (c, 0)),
            pl.BlockSpec((batch, SSM_COLS), const2),
            pl.BlockSpec((batch, SSM_COLS), const2),
        ],
        out_shape=[
            jax.ShapeDtypeStruct((rows, d), F32),
            jax.ShapeDtypeStruct((batch, SSM_COLS), F32),
            jax.ShapeDtypeStruct((batch, SSM_COLS), F32),
        ],
        scratch_shapes=[
            pltpu.VMEM((N_SCAN_BLOCKS, blk, SCAN_COLS), F32),
            pltpu.VMEM((N_SCAN_BLOCKS, blk, SCAN_COLS), F32),
            pltpu.VMEM((N_SCAN_BLOCKS, batch, SCAN_COLS), F32),
            pltpu.VMEM((N_SCAN_BLOCKS, batch, SCAN_COLS), F32),
        ],
        compiler_params=_params(("arbitrary",)),
        name="s5_scan",
    )(u_tm, h0_re, h0_im, ar, ai, bbr, bbi, ccr, cci, dsk)


def _glu_out_kernel(g_ref, x_ref, pt_ref, wg_ref, bg_ref, wo_ref, o_ref, *, batch):
    tile = x_ref.shape[1]
    g = g_ref[...]
    z = _dot(g.astype(BF16), wg_ref[...]) + bg_ref[...]
    o = (g * jax.nn.sigmoid(z)).astype(BF16)
    if batch == 1:
        o_ref[0] = x_ref[0] + _dot(o, wo_ref[...])
    else:
        n_groups = tile // PERM_STEPS
        groups = [_dot(pt_ref[...], o[tg * MXU_DIM:(tg + 1) * MXU_DIM]).astype(BF16) for tg in range(n_groups)]
        mix = _dot(jnp.concatenate(groups, axis=0), wo_ref[...])
        for tg in range(n_groups):
            for b in range(batch):
                ts = slice(tg * PERM_STEPS, (tg + 1) * PERM_STEPS)
                r0 = tg * MXU_DIM + b * PERM_STEPS
                o_ref[b, ts, :] = x_ref[b, ts, :] + mix[r0:r0 + PERM_STEPS]


def _glu_out(g_tm, x3d, perm_t, w_glu_bf, b_glu, w_out_bf, *, tile):
    bsz, t, d = x3d.shape
    assert bsz == 1 or (bsz == SUBLANES and tile % PERM_STEPS == 0)
    const2 = lambda i: (0, 0)
    return pl.pallas_call(
        functools.partial(_glu_out_kernel, batch=bsz),
        grid=(t // tile,),
        in_specs=[
            pl.BlockSpec((tile * bsz, d), lambda i: (i, 0)),
            pl.BlockSpec((bsz, tile, d), lambda i: (0, i, 0)),
            pl.BlockSpec((MXU_DIM, MXU_DIM), const2),
            pl.BlockSpec((d, d), const2),
            pl.BlockSpec((1, d), const2),
            pl.BlockSpec((d, d), const2),
        ],
        out_specs=pl.BlockSpec((bsz, tile, d), lambda i: (0, i, 0)),
        out_shape=jax.ShapeDtypeStruct((bsz, t, d), F32),
        compiler_params=_params(("parallel",)),
        name="s5_glu_out",
    )(g_tm, x3d, perm_t, w_glu_bf, b_glu, w_out_bf)


def _s5_discretize(a_re, a_im, log_dt, b_re, b_im, c_re, c_im, d_skip):
    dt = jnp.exp(log_dt)[:, None]
    mag = jnp.exp(a_re * dt)
    abar_re = mag * jnp.cos(a_im * dt)
    abar_im = mag * jnp.sin(a_im * dt)
    num_re = abar_re - 1.0
    num_im = abar_im
    den = a_re * a_re + a_im * a_im
    f_re = (num_re * a_re + num_im * a_im) / den
    f_im = (num_im * a_re - num_re * a_im) / den
    bb_re = f_re[..., None] * b_re - f_im[..., None] * b_im
    bb_im = f_re[..., None] * b_im + f_im[..., None] * b_re
    gpb = MXU_DIM // SSM_GROUP
    eye = jnp.eye(gpb, dtype=F32)

    def in_blocks(bb):
        bt = jnp.transpose(bb, (0, 2, 1)).reshape(N_DIAG_BLOCKS, gpb, SSM_GROUP, SSM_STATE)
        full = bt[:, :, :, None, :] * eye[None, :, None, :, None]
        return full.reshape(N_DIAG_BLOCKS, MXU_DIM, DIAG_COLS).astype(BF16)

    def out_blocks(cc):
        ct = jnp.transpose(cc, (0, 2, 1)).reshape(N_DIAG_BLOCKS, gpb, SSM_STATE, SSM_GROUP)
        full = ct[:, :, :, None, :] * eye[None, :, None, :, None]
        return full.reshape(N_DIAG_BLOCKS, DIAG_COLS, MXU_DIM).astype(BF16)

    ar = abar_re.reshape(N_SCAN_BLOCKS, 1, SCAN_COLS)
    ai = abar_im.reshape(N_SCAN_BLOCKS, 1, SCAN_COLS)
    return (ar, ai, in_blocks(bb_re), in_blocks(bb_im), out_blocks(c_re), out_blocks(-c_im),
            d_skip.reshape(1, D_MODEL))


def _gmlp_kernel(x_ref, g_ref, win_ref, bin_ref, lng_ref, lnb_ref, mix_ref, mixb_ref, wout_ref,
                 o_ref, v_ref, s_scr, *, rows):
    x = x_ref[...]
    hn = _rms(x, g_ref[...]).astype(BF16)
    z = _gelu(_dot(hn, win_ref[...]) + bin_ref[...])
    u = z[:, :GMLP_HALF]
    v = z[:, GMLP_HALF:]
    mu = jnp.mean(v, axis=-1, keepdims=True)
    vc = v - mu
    vn = vc * lax.rsqrt(jnp.mean(vc * vc, axis=-1, keepdims=True) + EPS) * lng_ref[...] + lnb_ref[...]
    v_ref[...] = vn
    vb = vn.astype(BF16)
    for r in range(rows // CHUNK):
        rs = slice(r * CHUNK, (r + 1) * CHUNK)
        for h in range(GMLP_HEADS):
            cs = slice(h * GMLP_HEAD_DIM, (h + 1) * GMLP_HEAD_DIM)
            mixed = _dot(mix_ref[h], vb[rs, cs]) + mixb_ref[h]
            s_scr[rs, cs] = (u[rs, cs] * mixed).astype(BF16)
    o_ref[...] = x + _dot(s_scr[...], wout_ref[...])


def _gmlp(x2d, g, w_in_bf, b_in, ln_g, ln_b, mix_bf, mix_bias, w_out_bf, *, tile):
    rows, d = x2d.shape
    const2 = lambda i: (0, 0)
    const3 = lambda i: (0, 0, 0)
    kern = functools.partial(_gmlp_kernel, rows=tile)
    return pl.pallas_call(
        kern,
        grid=(rows // tile,),
        in_specs=[
            pl.BlockSpec((tile, d), lambda i: (i, 0)),
            pl.BlockSpec((1, d), const2),
            pl.BlockSpec((d, 2 * GMLP_HALF), const2),
            pl.BlockSpec((1, 2 * GMLP_HALF), const2),
            pl.BlockSpec((1, GMLP_HALF), const2),
            pl.BlockSpec((1, GMLP_HALF), const2),
            pl.BlockSpec((GMLP_HEADS, CHUNK, CHUNK), const3),
            pl.BlockSpec((GMLP_HEADS, CHUNK, GMLP_HEAD_DIM), const3),
            pl.BlockSpec((GMLP_HALF, d), const2),
        ],
        out_specs=[
            pl.BlockSpec((tile, d), lambda i: (i, 0)),
            pl.BlockSpec((tile, GMLP_HALF), lambda i: (i, 0)),
        ],
        out_shape=[
            jax.ShapeDtypeStruct((rows, d), F32),
            jax.ShapeDtypeStruct((rows, GMLP_HALF), F32),
        ],
        scratch_shapes=[pltpu.VMEM((tile, GMLP_HALF), BF16)],
        compiler_params=_params(("parallel",)),
        name="gmlp",
    )(x2d, g, w_in_bf, b_in, ln_g, ln_b, mix_bf, mix_bias, w_out_bf)


KEY_PITCH = N_KEYS + SUBLANES
KEYS_PER_TRIP = 2


def _sorting_network(n):
    pairs = []
    p = 1
    while p < n:
        k = p
        while k >= 1:
            for j in range(k % p, n - k, 2 * k):
                for i in range(min(k, n - j - k)):
                    if (i + j) // (2 * p) == (i + j + k) // (2 * p):
                        pairs.append((i + j, i + j + k))
            k //= 2
        p *= 2
    return pairs


SORT16 = _sorting_network(PEER_TOPK)


def _compare_exchange(v, i, j):
    v[i], v[j] = jnp.maximum(v[i], v[j]), jnp.minimum(v[i], v[j])


def _sort16_desc(v):
    v = list(v)
    for i, j in SORT16:
        _compare_exchange(v, i, j)
    return v


def _merge_top16(a, b):
    c = [jnp.maximum(a[i], b[PEER_TOPK - 1 - i]) for i in range(PEER_TOPK)]
    d = PEER_TOPK // 2
    while d >= 1:
        for i in range(PEER_TOPK):
            if i & d == 0:
                _compare_exchange(c, i, i + d)
        d //= 2
    return c


def _top16_of(values):
    lists = [_sort16_desc(values[i:i + PEER_TOPK]) for i in range(0, len(values), PEER_TOPK)]
    while len(lists) > 1:
        merged = [_merge_top16(lists[i], lists[i + 1]) for i in range(0, len(lists) - 1, 2)]
        if len(lists) % 2:
            merged.append(lists[-1])
        lists = merged
    return lists[0]


def _peer_thresholds(s1_t, s2_t, e2_t, s1_v, s2_v, th_v, c1_v, grp):
    base = grp * (SUBLANES * KEY_PITCH)

    def key_rows(k):
        return pl.ds(base + k, SUBLANES, stride=KEY_PITCH)

    for k in range(N_KEYS):
        s1_v[k] = s1_t[key_rows(k), :]
        s2_v[k] = s2_t[key_rows(k), :]
    v1 = _top16_of([s1_v[k] for k in range(N_KEYS)])
    v2 = _top16_of([s2_v[k] for k in range(N_KEYS)])

    pairs = [(a, b) for a in range(PEER_TOPK) for b in range(PEER_TOPK // (a + 1))]
    cand = {ab: v1[ab[0]] + v2[ab[1]] for ab in pairs}
    first_row = [cand[(0, b)] for b in range(PEER_TOPK)]
    rest = [cand[ab] for ab in pairs if ab[0] > 0]
    rest += [jnp.full_like(v1[0], NEG_INF)] * (-len(rest) % PEER_TOPK)
    tau = _merge_top16(first_row, _top16_of(rest))[PEER_TOPK - 1]

    cmax = cand[(0, 0)]
    z = jnp.zeros_like(tau)
    th_rank = [jnp.full_like(tau, POS_INF) for _ in range(PEER_TOPK)]
    for a, b in pairs:
        sel = cand[(a, b)] >= tau
        z = z + jnp.where(sel, jnp.exp(cand[(a, b)] - cmax), 0.0)
        th_rank[a] = jnp.where(sel, v2[b], th_rank[a])
    c_scale = SQRT_HALF / z

    def per_key(k, carry):
        s1k = s1_v[k]
        th = jnp.full_like(s1k, POS_INF)
        for a in reversed(range(PEER_TOPK)):
            th = jnp.where(s1k >= v1[a], th_rank[a], th)
        th_v[grp, k] = th
        c1_v[grp, k] = jnp.exp(s1k - v1[0]) * c_scale
        e2_t[key_rows(k), :] = jnp.exp(s2_v[k] - v2[0])
        return carry

    lax.fori_loop(0, N_KEYS, per_key, 0, unroll=4)


def _peer_kernel(x_ref, g_ref, gf_ref, kq_ref, u_ref, vt_ref, o_ref,
                 xn_scr, xs_scr, s1_t, s2_t, e2_t, s1_v, s2_v, th_v, c1_v, h_scr, w_scr, acc_scr,
                 *, tm, eb, final_norm):
    e = pl.program_id(1)
    n_e = pl.num_programs(1)
    n_tc = tm // LANES
    keys_per_block = eb // N_KEYS
    n_groups = PEER_HEADS * n_tc // SUBLANES

    def tile_row(h, tc):
        return (h * n_tc + tc) * KEY_PITCH

    @pl.when(e == 0)
    def prepare():
        xn_t = _rms(x_ref[...], g_ref[...]).T
        xn_scr[...] = xn_t.astype(BF16)
        xs_scr[...] = (xn_t * SQRT_HALF).astype(BF16)
        for h in range(PEER_HEADS):
            s12 = _dot(kq_ref[h * PEER_QUERY:(h + 1) * PEER_QUERY, :], xn_scr[...])
            for tc in range(n_tc):
                rows = pl.ds(tile_row(h, tc), N_KEYS)
                s1_t[rows, :] = s12[:N_KEYS, tc * LANES:(tc + 1) * LANES]
                s2_t[rows, :] = s12[N_KEYS:, tc * LANES:(tc + 1) * LANES]

        def thresholds(grp, carry):
            _peer_thresholds(s1_t, s2_t, e2_t, s1_v, s2_v, th_v, c1_v, grp)
            return carry

        lax.fori_loop(0, n_groups, thresholds, 0)
        acc_scr[...] = jnp.zeros_like(acc_scr)

    hs = _dot(u_ref[...], xs_scr[...])
    h_scr[...] = hs * (1.0 + lax.erf(hs))

    def key_rows(p, carry):
        for tc in range(n_tc):
            lanes = slice(tc * LANES, (tc + 1) * LANES)
            gates = [jnp.zeros((N_KEYS, LANES), F32) for _ in range(KEYS_PER_TRIP)]
            for h in range(PEER_HEADS):
                grp, j = divmod(h * n_tc + tc, SUBLANES)
                tile = pl.ds(tile_row(h, tc), N_KEYS)
                s2 = s2_t[tile, :]
                e2 = e2_t[tile, :]
                for k in range(KEYS_PER_TRIP):
                    i1 = e * keys_per_block + p * KEYS_PER_TRIP + k
                    th = th_v[grp, i1, j:j + 1, :]
                    c1 = c1_v[grp, i1, j:j + 1, :]
                    gates[k] = gates[k] + jnp.where(s2 >= th, c1 * e2, 0.0)
            for k in range(KEYS_PER_TRIP):
                rows = pl.ds(pl.multiple_of((p * KEYS_PER_TRIP + k) * N_KEYS, N_KEYS), N_KEYS)
                w_scr[rows, lanes] = (gates[k] * h_scr[rows, lanes]).astype(BF16)
        return carry

    lax.fori_loop(0, keys_per_block // KEYS_PER_TRIP, key_rows, 0)
    acc_scr[...] += _dot(vt_ref[...], w_scr[...])

    @pl.when(e == n_e - 1)
    def _():
        y = x_ref[...] + acc_scr[...].T
        if final_norm:
            y = _rms(y, gf_ref[...])
        o_ref[...] = y


def _peer(x2d, g, g_final, kq_bf, u_bf, v_t_bf, *, layer, final_norm,
          tm=PEER_TOKEN_TILE, eb=PEER_EXPERT_BLOCK):
    n, d = x2d.shape
    n_tc = tm // LANES
    n_e = N_EXPERTS // eb
    const2 = lambda i, e: (0, 0)
    kern = functools.partial(_peer_kernel, tm=tm, eb=eb, final_norm=final_norm)
    n_tiles = PEER_HEADS * n_tc
    assert n_tiles % SUBLANES == 0
    tiles = (n_tiles * KEY_PITCH, LANES)
    per_key = (N_KEYS, SUBLANES, LANES)
    per_key_groups = (n_tiles // SUBLANES,) + per_key
    return pl.pallas_call(
        kern,
        grid=(n // tm, n_e),
        in_specs=[
            pl.BlockSpec((tm, d), lambda i, e: (i, 0)),
            pl.BlockSpec((1, d), const2),
            pl.BlockSpec((1, d), const2),
            pl.BlockSpec((None, PEER_HEADS * 2 * N_KEYS, d), lambda i, e: (layer, 0, 0)),
            pl.BlockSpec((None, eb, d), lambda i, e: (layer, e, 0)),
            pl.BlockSpec((None, d, eb), lambda i, e: (layer, 0, e)),
        ],
        out_specs=pl.BlockSpec((tm, d), lambda i, e: (i, 0)),
        out_shape=jax.ShapeDtypeStruct((n, d), F32),
        scratch_shapes=[
            pltpu.VMEM((d, tm), BF16),
            pltpu.VMEM((d, tm), BF16),
            pltpu.VMEM(tiles, F32),
            pltpu.VMEM(tiles, F32),
            pltpu.VMEM(tiles, F32),
            pltpu.VMEM(per_key, F32),
            pltpu.VMEM(per_key, F32),
            pltpu.VMEM(per_key_groups, F32),
            pltpu.VMEM(per_key_groups, F32),
            pltpu.VMEM((eb, tm), F32),
            pltpu.VMEM((eb, tm), BF16),
            pltpu.VMEM((d, tm), F32),
        ],
        compiler_params=_params(("parallel", "arbitrary")),
        name="peer",
    )(x2d, g, g_final, kq_bf, u_bf, v_t_bf)


def _peer_score_weights_kernel(wq_ref, k1_ref, k2_ref, o_ref):
    o_ref[:N_KEYS, :] = _dot(k1_ref[...], wq_ref[:PEER_HALF, :]).astype(BF16)
    o_ref[N_KEYS:, :] = _dot(k2_ref[...], wq_ref[PEER_HALF:, :]).astype(BF16)


def _peer_score_weights(wq_t_bf, k1_bf, k2_bf):
    n_layers, rows, d = wq_t_bf.shape
    key_spec = pl.BlockSpec((None, N_KEYS, PEER_HALF), lambda l, h: (l, 0, 0))
    return pl.pallas_call(
        _peer_score_weights_kernel,
        grid=(n_layers, PEER_HEADS),
        in_specs=[pl.BlockSpec((None, PEER_QUERY, d), lambda l, h: (l, h, 0)), key_spec, key_spec],
        out_specs=pl.BlockSpec((None, 2 * N_KEYS, d), lambda l, h: (l, h, 0)),
        out_shape=jax.ShapeDtypeStruct((n_layers, PEER_HEADS * 2 * N_KEYS, d), BF16),
        compiler_params=_params(("parallel", "parallel")),
        name="peer_score_weights",
    )(wq_t_bf, k1_bf, k2_bf)


def _peer_tables_kernel(u_ref, v_ref, ub_ref, vt_ref):
    ub_ref[...] = u_ref[...].astype(BF16)
    vt_ref[0] = v_ref[0].T.astype(BF16)


def _peer_tables(peer_u, peer_v, *, tile=1024):
    n_layers, n_exp, d = peer_u.shape
    return pl.pallas_call(
        _peer_tables_kernel,
        grid=(n_layers, n_exp // tile),
        in_specs=[
            pl.BlockSpec((1, tile, d), lambda l, i: (l, i, 0)),
            pl.BlockSpec((1, tile, d), lambda l, i: (l, i, 0)),
        ],
        out_specs=[
            pl.BlockSpec((1, tile, d), lambda l, i: (l, i, 0)),
            pl.BlockSpec((1, d, tile), lambda l, i: (l, 0, i)),
        ],
        out_shape=[
            jax.ShapeDtypeStruct((n_layers, n_exp, d), BF16),
            jax.ShapeDtypeStruct((n_layers, d, n_exp), BF16),
        ],
        compiler_params=_params(("parallel", "parallel")),
        name="peer_tables",
    )(peer_u, peer_v)


def _trunk(x, h0_re, h0_im, w, *, scan_steps):
    bsz, t, d = x.shape
    n = bsz * t
    x2d = x.reshape(n, d)

    if bsz == SUBLANES and t % S5_ROW_TILE == 0:
        u_tm = _norm_matmul(x, w["norm_mix_g"][0], w["perm"], w["ssm_w_in"], tile=S5_ROW_TILE)
        g_tm, hr, hi = _s5_scan(u_tm, h0_re, h0_im, w["disc"], batch=bsz, steps=scan_steps)
        x2d = _glu_out(g_tm, x, w["perm"].T, w["ssm_w_glu"], w["ssm_b_glu"], w["ssm_w_out"],
                       tile=S5_ROW_TILE).reshape(n, d)
    else:
        u_bm = _norm_matmul(x2d[None], w["norm_mix_g"][0], w["perm"], w["ssm_w_in"], tile=n)
        u_tm = u_bm.reshape(bsz, t, d).transpose(1, 0, 2).reshape(n, d)
        g_tm, hr, hi = _s5_scan(u_tm, h0_re, h0_im, w["disc"], batch=bsz, steps=scan_steps)
        g_bm = g_tm.reshape(t, bsz, d).transpose(1, 0, 2).reshape(n, d)
        x2d = _glu_out(g_bm, x2d[None], w["perm"].T, w["ssm_w_glu"], w["ssm_b_glu"], w["ssm_w_out"],
                       tile=n).reshape(n, d)
    x2d = _peer(x2d, w["norm_ffn_g"][0], w["norm_final_g"], *w["peer_weights"], layer=0,
                final_norm=False)

    if t % CHUNK == 0:
        mix, mix_bias = w["mix_full"]
    else:
        mix, mix_bias = w["mix_short"]
    x2d, v = _gmlp(x2d, w["norm_mix_g"][1], w["gmlp_w_in"], w["gmlp_b_in"], w["gmlp_ln_g"], w["gmlp_ln_b"],
                   mix, mix_bias, w["gmlp_w_out"], tile=GMLP_ROW_TILE)
    y2d = _peer(x2d, w["norm_ffn_g"][1], w["norm_final_g"], *w["peer_weights"], layer=1,
                final_norm=True)
    return y2d.reshape(bsz, t, d), hr, hi, v


def kernel(x_prompt, x_sample, state_ssm_re, state_ssm_im, norm_mix_g, norm_ffn_g, norm_final_g, ssm_w_in, ssm_a_re, ssm_a_im, ssm_log_dt, ssm_b_re, ssm_b_im, ssm_c_re, ssm_c_im, ssm_d, ssm_w_glu, ssm_b_glu, ssm_w_out, gmlp_w_in, gmlp_b_in, gmlp_ln_g, gmlp_ln_b, gmlp_w_s, gmlp_b_s, gmlp_w_out, peer_w_q, peer_k1, peer_k2, peer_u, peer_v):
    bp, tp, d = x_prompt.shape
    bs, ts, _ = x_sample.shape
    assert tp % CHUNK == 0 and CHUNK % ts == 0 and d == D_MODEL

    tril = jnp.tril(jnp.ones((CHUNK, CHUNK), F32))
    ws = gmlp_w_s[0] * tril
    bias_full = jnp.broadcast_to(gmlp_b_s[0][:, :, None], (GMLP_HEADS, CHUNK, GMLP_HEAD_DIM))
    reps = CHUNK // ts
    ws_short = jnp.einsum("ab,hij->haibj", jnp.eye(reps, dtype=F32), ws[:, :ts, :ts]).reshape(GMLP_HEADS, CHUNK, CHUNK)
    bias_short = jnp.broadcast_to(jnp.tile(gmlp_b_s[0][:, :ts], (1, reps))[:, :, None],
                                  (GMLP_HEADS, CHUNK, GMLP_HEAD_DIM))

    w = {
        "norm_mix_g": norm_mix_g[:, None, :],
        "norm_ffn_g": norm_ffn_g[:, None, :],
        "norm_final_g": norm_final_g[None, :],
        "perm": _time_major_perm(),
        "ssm_w_in": ssm_w_in[0].astype(BF16),
        "disc": _s5_discretize(ssm_a_re[0], ssm_a_im[0], ssm_log_dt[0], ssm_b_re[0], ssm_b_im[0],
                               ssm_c_re[0], ssm_c_im[0], ssm_d[0]),
        "ssm_w_glu": ssm_w_glu[0].astype(BF16),
        "ssm_b_glu": ssm_b_glu[0][None, :],
        "ssm_w_out": ssm_w_out[0].astype(BF16),
        "gmlp_w_in": gmlp_w_in[0].astype(BF16),
        "gmlp_b_in": gmlp_b_in[0][None, :],
        "gmlp_ln_g": gmlp_ln_g[0][None, :],
        "gmlp_ln_b": gmlp_ln_b[0][None, :],
        "mix_full": (ws.astype(BF16), bias_full),
        "mix_short": (ws_short.astype(BF16), bias_short),
        "gmlp_w_out": gmlp_w_out[0].astype(BF16),
        "peer_weights": (_peer_score_weights(jnp.swapaxes(peer_w_q, 1, 2).astype(BF16), peer_k1.astype(BF16),
                                             peer_k2.astype(BF16)),) + tuple(_peer_tables(peer_u, peer_v)),
    }

    zeros = jnp.zeros((bp, SSM_COLS), F32)
    y_p, hr_p, hi_p, _ = _trunk(x_prompt, zeros, zeros, w, scan_steps=S5_SCAN_STEPS)
    y_s, hr_s, hi_s, v_s = _trunk(x_sample, state_ssm_re[0].reshape(bs, SSM_COLS),
                                  state_ssm_im[0].reshape(bs, SSM_COLS), w, scan_steps=ts)
    st = lambda a, b: a.reshape(1, b, SSM_GROUPS, SSM_STATE)
    return (y_p, y_s, st(hr_p, bp), st(hi_p, bp), st(hr_s, bs), st(hi_s, bs),
            v_s.reshape(1, bs, ts, GMLP_HALF))
```

```python
import functools
import math

import jax
import jax.numpy as jnp
from jax import lax
from jax.experimental import pallas as pl
from jax.experimental.pallas import tpu as pltpu

F32 = jnp.float32
BF16 = jnp.bfloat16

EPS = 1e-6
D_MODEL = 1024
SSM_GROUP = 16
SSM_GROUPS = 64
SSM_STATE = 64
SSM_COLS = SSM_GROUPS * SSM_STATE
GMLP_HALF = 2 * D_MODEL
GMLP_HEADS = 8
GMLP_HEAD_DIM = GMLP_HALF // GMLP_HEADS
CHUNK = 128
PEER_HEADS = 8
N_KEYS = 128
N_EXPERTS = N_KEYS * N_KEYS
PEER_QUERY = 256
PEER_HALF = 128
PEER_TOPK = 16

LANES = 128
SUBLANES = 8
MXU_DIM = 256
VMEM_LIMIT = 56 * 1024 * 1024

NEG_INF = float("-inf")
POS_INF = float("inf")
SQRT_HALF = math.sqrt(0.5)


def _dot(a, b):
    return jnp.dot(a, b, preferred_element_type=F32)


def _rms(x, g):
    ms = jnp.mean(x * x, axis=-1, keepdims=True)
    return x * lax.rsqrt(ms + EPS) * g


def _gelu(x):
    return 0.5 * x * (1.0 + lax.erf(x * SQRT_HALF))


def _params(sem):
    return pltpu.CompilerParams(dimension_semantics=sem, vmem_limit_bytes=VMEM_LIMIT)


S5_ROW_TILE = 128
S5_SCAN_STEPS = 64
GMLP_ROW_TILE = 512
PEER_TOKEN_TILE = 512
PEER_EXPERT_BLOCK = 2048


PERM_STEPS = MXU_DIM // SUBLANES


def _time_major_perm():
    r = jnp.arange(MXU_DIM)
    src = (r % SUBLANES) * PERM_STEPS + r // SUBLANES
    return (src[:, None] == r[None, :]).astype(BF16)


def _norm_matmul_kernel(x_ref, g_ref, p_ref, w_ref, o_ref, *, batch):
    tile = x_ref.shape[1]
    xn = [_rms(x_ref[b], g_ref[...]).astype(BF16) for b in range(batch)]
    if batch == 1:
        xp = xn[0]
    else:
        groups = []
        for tg in range(tile // PERM_STEPS):
            ts = slice(tg * PERM_STEPS, (tg + 1) * PERM_STEPS)
            rows_bt = jnp.concatenate([xn[b][ts] for b in range(batch)], axis=0)
            groups.append(_dot(p_ref[...], rows_bt).astype(BF16))
        xp = jnp.concatenate(groups, axis=0)
    o_ref[...] = _dot(xp, w_ref[...])


def _norm_matmul(x3d, g, perm, w_bf, *, tile):
    bsz, t, d = x3d.shape
    n = w_bf.shape[1]
    assert bsz == 1 or (bsz == SUBLANES and tile % PERM_STEPS == 0)
    return pl.pallas_call(
        functools.partial(_norm_matmul_kernel, batch=bsz),
        grid=(t // tile,),
        in_specs=[
            pl.BlockSpec((bsz, tile, d), lambda i: (0, i, 0)),
            pl.BlockSpec((1, d), lambda i: (0, 0)),
            pl.BlockSpec((MXU_DIM, MXU_DIM), lambda i: (0, 0)),
            pl.BlockSpec((d, n), lambda i: (0, 0)),
        ],
        out_specs=pl.BlockSpec((tile * bsz, n), lambda i: (i, 0)),
        out_shape=jax.ShapeDtypeStruct((t * bsz, n), F32),
        compiler_params=_params(("parallel",)),
        name="s5_in_proj",
    )(x3d, g, perm, w_bf)
SCAN_COLS = 1024
N_SCAN_BLOCKS = SSM_COLS // SCAN_COLS
N_DIAG_BLOCKS = D_MODEL // MXU_DIM
DIAG_COLS = SSM_COLS // N_DIAG_BLOCKS
SCAN_PER_DIAG = DIAG_COLS // SCAN_COLS


def _s5_scan_kernel(u_ref, h0r_ref, h0i_ref, ar_ref, ai_ref, bbr_ref, bbi_ref, ccr_ref, cci_ref, d_ref,
                    g_ref, hr_ref, hi_ref, bur, bui, st_r, st_i, *, steps, batch):
    c = pl.program_id(0)
    n_sub = batch // SUBLANES

    @pl.when(c == 0)
    def _():
        for cb in range(N_SCAN_BLOCKS):
            st_r[cb] = h0r_ref[:, cb * SCAN_COLS:(cb + 1) * SCAN_COLS]
            st_i[cb] = h0i_ref[:, cb * SCAN_COLS:(cb + 1) * SCAN_COLS]

    u = u_ref[...]
    ub = u.astype(BF16)
    for kb in range(N_DIAG_BLOCKS):
        lhs = ub[:, kb * MXU_DIM:(kb + 1) * MXU_DIM]
        pr = _dot(lhs, bbr_ref[kb])
        pi = _dot(lhs, bbi_ref[kb])
        for j in range(SCAN_PER_DIAG):
            bur[kb * SCAN_PER_DIAG + j] = pr[:, j * SCAN_COLS:(j + 1) * SCAN_COLS]
            bui[kb * SCAN_PER_DIAG + j] = pi[:, j * SCAN_COLS:(j + 1) * SCAN_COLS]

    def scan_block(idx, carry):
        cb = idx // n_sub
        s = idx % n_sub
        ar = jnp.broadcast_to(ar_ref[cb], (SUBLANES, SCAN_COLS))
        ai = jnp.broadcast_to(ai_ref[cb], (SUBLANES, SCAN_COLS))
        row0 = pl.multiple_of(s * SUBLANES, SUBLANES)
        h_r = st_r[cb, pl.ds(row0, SUBLANES), :]
        h_i = st_i[cb, pl.ds(row0, SUBLANES), :]

        def step(t, h):
            hr, hi = h
            r = pl.multiple_of(t * batch + row0, SUBLANES)
            nr = ar * hr - ai * hi + bur[cb, pl.ds(r, SUBLANES), :]
            ni = ar * hi + ai * hr + bui[cb, pl.ds(r, SUBLANES), :]
            bur[cb, pl.ds(r, SUBLANES), :] = nr
            bui[cb, pl.ds(r, SUBLANES), :] = ni
            return nr, ni

        h_r, h_i = lax.fori_loop(0, steps, step, (h_r, h_i), unroll=min(steps, 8))
        st_r[cb, pl.ds(row0, SUBLANES), :] = h_r
        st_i[cb, pl.ds(row0, SUBLANES), :] = h_i
        return carry

    lax.fori_loop(0, N_SCAN_BLOCKS * n_sub, scan_block, 0)

    for cb in range(N_SCAN_BLOCKS):
        hr_ref[:, cb * SCAN_COLS:(cb + 1) * SCAN_COLS] = st_r[cb]
        hi_ref[:, cb * SCAN_COLS:(cb + 1) * SCAN_COLS] = st_i[cb]

    for nb in range(N_DIAG_BLOCKS):
        acc = None
        for j in range(SCAN_PER_DIAG):
            cb = nb * SCAN_PER_DIAG + j
            part = (_dot(bur[cb].astype(BF16), ccr_ref[nb, j * SCAN_COLS:(j + 1) * SCAN_COLS, :])
                    + _dot(bui[cb].astype(BF16), cci_ref[nb, j * SCAN_COLS:(j + 1) * SCAN_COLS, :]))
            acc = part if acc is None else acc + part
        cols = slice(nb * MXU_DIM, (nb + 1) * MXU_DIM)
        y = acc + d_ref[:, cols] * u[:, cols]
        g_ref[:, cols] = _gelu(y)


def _s5_scan(u_tm, h0_re, h0_im, disc, *, batch, steps):
    rows, d = u_tm.shape
    t = rows // batch
    n_chunks = t // steps
    blk = steps * batch
    ar, ai, bbr, bbi, ccr, cci, dsk = disc
    const2 = lambda c: (0, 0)
    const3 = lambda c: (0, 0, 0)
    kern = functools.partial(_s5_scan_kernel, steps=steps, batch=batch)
    return pl.pallas_call(
        kern,
        grid=(n_chunks,),
        in_specs=[
            pl.BlockSpec((blk, d), lambda c: (c, 0)),
            pl.BlockSpec((batch, SSM_COLS), const2),
            pl.BlockSpec((batch, SSM_COLS), const2),
            pl.BlockSpec((N_SCAN_BLOCKS, 1, SCAN_COLS), const3),
            pl.BlockSpec((N_SCAN_BLOCKS, 1, SCAN_COLS), const3),
            pl.BlockSpec((N_DIAG_BLOCKS, MXU_DIM, DIAG_COLS), const3),
            pl.BlockSpec((N_DIAG_BLOCKS, MXU_DIM, DIAG_COLS), const3),
            pl.BlockSpec((N_DIAG_BLOCKS, DIAG_COLS, MXU_DIM), const3),
            pl.BlockSpec((N_DIAG_BLOCKS, DIAG_COLS, MXU_DIM), const3),
            pl.BlockSpec((1, d), const2),
        ],
        out_specs=[
            pl.BlockSpec((blk, d), lambda c: (c, 0)),
            pl.BlockSpec((batch, SSM_COLS), const2),
            pl.BlockSpec((batch, SSM_COLS), const2),
        ],
        out_shape=[
            jax.ShapeDtypeStruct((rows, d), F32),
            jax.ShapeDtypeStruct((batch, SSM_COLS), F32),
            jax.ShapeDtypeStruct((batch, SSM_COLS), F32),
        ],
        scratch_shapes=[
            pltpu.VMEM((N_SCAN_BLOCKS, blk, SCAN_COLS), F32),
            pltpu.VMEM((N_SCAN_BLOCKS, blk, SCAN_COLS), F32),
            pltpu.VMEM((N_SCAN_BLOCKS, batch, SCAN_COLS), F32),
            pltpu.VMEM((N_SCAN_BLOCKS, batch, SCAN_COLS), F32),
        ],
        compiler_params=_params(("arbitrary",)),
        name="s5_scan",
    )(u_tm, h0_re, h0_im, ar, ai, bbr, bbi, ccr, cci, dsk)


def _glu_out_kernel(g_ref, x_ref, pt_ref, wg_ref, bg_ref, wo_ref, o_ref, *, batch):
    tile = x_ref.shape[1]
    g = g_ref[...]
    z = _dot(g.astype(BF16), wg_ref[...]) + bg_ref[...]
    o = (g * jax.nn.sigmoid(z)).astype(BF16)
    if batch == 1:
        o_ref[0] = x_ref[0] + _dot(o, wo_ref[...])
    else:
        n_groups = tile // PERM_STEPS
        groups = [_dot(pt_ref[...], o[tg * MXU_DIM:(tg + 1) * MXU_DIM]).astype(BF16) for tg in range(n_groups)]
        mix = _dot(jnp.concatenate(groups, axis=0), wo_ref[...])
        for tg in range(n_groups):
            for b in range(batch):
                ts = slice(tg * PERM_STEPS, (tg + 1) * PERM_STEPS)
                r0 = tg * MXU_DIM + b * PERM_STEPS
                o_ref[b, ts, :] = x_ref[b, ts, :] + mix[r0:r0 + PERM_STEPS]


def _glu_out(g_tm, x3d, perm_t, w_glu_bf, b_glu, w_out_bf, *, tile):
    bsz, t, d = x3d.shape
    assert bsz == 1 or (bsz == SUBLANES and tile % PERM_STEPS == 0)
    const2 = lambda i: (0, 0)
    return pl.pallas_call(
        functools.partial(_glu_out_kernel, batch=bsz),
        grid=(t // tile,),
        in_specs=[
            pl.BlockSpec((tile * bsz, d), lambda i: (i, 0)),
            pl.BlockSpec((bsz, tile, d), lambda i: (0, i, 0)),
            pl.BlockSpec((MXU_DIM, MXU_DIM), const2),
            pl.BlockSpec((d, d), const2),
            pl.BlockSpec((1, d), const2),
            pl.BlockSpec((d, d), const2),
        ],
        out_specs=pl.BlockSpec((bsz, tile, d), lambda i: (0, i, 0)),
        out_shape=jax.ShapeDtypeStruct((bsz, t, d), F32),
        compiler_params=_params(("parallel",)),
        name="s5_glu_out",
    )(g_tm, x3d, perm_t, w_glu_bf, b_glu, w_out_bf)


def _s5_discretize(a_re, a_im, log_dt, b_re, b_im, c_re, c_im, d_skip):
    dt = jnp.exp(log_dt)[:, None]
    mag = jnp.exp(a_re * dt)
    abar_re = mag * jnp.cos(a_im * dt)
    abar_im = mag * jnp.sin(a_im * dt)
    num_re = abar_re - 1.0
    num_im = abar_im
    den = a_re * a_re + a_im * a_im
    f_re = (num_re * a_re + num_im * a_im) / den
    f_im = (num_im * a_re - num_re * a_im) / den
    bb_re = f_re[..., None] * b_re - f_im[..., None] * b_im
    bb_im = f_re[..., None] * b_im + f_im[..., None] * b_re
    gpb = MXU_DIM // SSM_GROUP
    eye = jnp.eye(gpb, dtype=F32)

    def in_blocks(bb):
        bt = jnp.transpose(bb, (0, 2, 1)).reshape(N_DIAG_BLOCKS, gpb, SSM_GROUP, SSM_STATE)
        full = bt[:, :, :, None, :] * eye[None, :, None, :, None]
        return full.reshape(N_DIAG_BLOCKS, MXU_DIM, DIAG_COLS).astype(BF16)

    def out_blocks(cc):
        ct = jnp.transpose(cc, (0, 2, 1)).reshape(N_DIAG_BLOCKS, gpb, SSM_STATE, SSM_GROUP)
        full = ct[:, :, :, None, :] * eye[None, :, None, :, None]
        return full.reshape(N_DIAG_BLOCKS, DIAG_COLS, MXU_DIM).astype(BF16)

    ar = abar_re.reshape(N_SCAN_BLOCKS, 1, SCAN_COLS)
    ai = abar_im.reshape(N_SCAN_BLOCKS, 1, SCAN_COLS)
    return (ar, ai, in_blocks(bb_re), in_blocks(bb_im), out_blocks(c_re), out_blocks(-c_im),
            d_skip.reshape(1, D_MODEL))


def _gmlp_kernel(x_ref, g_ref, win_ref, bin_ref, lng_ref, lnb_ref, mix_ref, mixb_ref, wout_ref,
                 o_ref, v_ref, s_scr, *, rows):
    x = x_ref[...]
    hn = _rms(x, g_ref[...]).astype(BF16)
    z = _gelu(_dot(hn, win_ref[...]) + bin_ref[...])
    u = z[:, :GMLP_HALF]
    v = z[:, GMLP_HALF:]
    mu = jnp.mean(v, axis=-1, keepdims=True)
    vc = v - mu
    vn = vc * lax.rsqrt(jnp.mean(vc * vc, axis=-1, keepdims=True) + EPS) * lng_ref[...] + lnb_ref[...]
    v_ref[...] = vn
    vb = vn.astype(BF16)
    for r in range(rows // CHUNK):
        rs = slice(r * CHUNK, (r + 1) * CHUNK)
        for h in range(GMLP_HEADS):
            cs = slice(h * GMLP_HEAD_DIM, (h + 1) * GMLP_HEAD_DIM)
            mixed = _dot(mix_ref[h], vb[rs, cs]) + mixb_ref[h]
            s_scr[rs, cs] = (u[rs, cs] * mixed).astype(BF16)
    o_ref[...] = x + _dot(s_scr[...], wout_ref[...])


def _gmlp(x2d, g, w_in_bf, b_in, ln_g, ln_b, mix_bf, mix_bias, w_out_bf, *, tile):
    rows, d = x2d.shape
    const2 = lambda i: (0, 0)
    const3 = lambda i: (0, 0, 0)
    kern = functools.partial(_gmlp_kernel, rows=tile)
    return pl.pallas_call(
        kern,
        grid=(rows // tile,),
        in_specs=[
            pl.BlockSpec((tile, d), lambda i: (i, 0)),
            pl.BlockSpec((1, d), const2),
            pl.BlockSpec((d, 2 * GMLP_HALF), const2),
            pl.BlockSpec((1, 2 * GMLP_HALF), const2),
            pl.BlockSpec((1, GMLP_HALF), const2),
            pl.BlockSpec((1, GMLP_HALF), const2),
            pl.BlockSpec((GMLP_HEADS, CHUNK, CHUNK), const3),
            pl.BlockSpec((GMLP_HEADS, CHUNK, GMLP_HEAD_DIM), const3),
            pl.BlockSpec((GMLP_HALF, d), const2),
        ],
        out_specs=[
            pl.BlockSpec((tile, d), lambda i: (i, 0)),
            pl.BlockSpec((tile, GMLP_HALF), lambda i: (i, 0)),
        ],
        out_shape=[
            jax.ShapeDtypeStruct((rows, d), F32),
            jax.ShapeDtypeStruct((rows, GMLP_HALF), F32),
        ],
        scratch_shapes=[pltpu.VMEM((tile, GMLP_HALF), BF16)],
        compiler_params=_params(("parallel",)),
        name="gmlp",
    )(x2d, g, w_in_bf, b_in, ln_g, ln_b, mix_bf, mix_bias, w_out_bf)


KEY_PITCH = N_KEYS + SUBLANES
KEYS_PER_TRIP = 2


def _sorting_network(n):
    pairs = []
    p = 1
    while p < n:
        k = p
        while k >= 1:
            for j in range(k % p, n - k, 2 * k):
                for i in range(min(k, n - j - k)):
                    if (i + j) // (2 * p) == (i + j + k) // (2 * p):
                        pairs.append((i + j, i + j + k))
            k //= 2
        p *= 2
    return pairs


SORT16 = _sorting_network(PEER_TOPK)


def _compare_exchange(v, i, j):
    v[i], v[j] = jnp.maximum(v[i], v[j]), jnp.minimum(v[i], v[j])


def _sort16_desc(v):
    v = list(v)
    for i, j in SORT16:
        _compare_exchange(v, i, j)
    return v


def _merge_top16(a, b):
    c = [jnp.maximum(a[i], b[PEER_TOPK - 1 - i]) for i in range(PEER_TOPK)]
    d = PEER_TOPK // 2
    while d >= 1:
        for i in range(PEER_TOPK):
            if i & d == 0:
                _compare_exchange(c, i, i + d)
        d //= 2
    return c


def _top16_of(values):
    lists = [_sort16_desc(values[i:i + PEER_TOPK]) for i in range(0, len(values), PEER_TOPK)]
    while len(lists) > 1:
        merged = [_merge_top16(lists[i], lists[i + 1]) for i in range(0, len(lists) - 1, 2)]
        if len(lists) % 2:
            merged.append(lists[-1])
        lists = merged
    return lists[0]


def _peer_thresholds(s1_t, s2_t, e2_t, s1_v, s2_v, th_v, c1_v, grp):
    base = grp * (SUBLANES * KEY_PITCH)

    def key_rows(k):
        return pl.ds(base + k, SUBLANES, stride=KEY_PITCH)

    for k in range(N_KEYS):
        s1_v[k] = s1_t[key_rows(k), :]
        s2_v[k] = s2_t[key_rows(k), :]
    v1 = _top16_of([s1_v[k] for k in range(N_KEYS)])
    v2 = _top16_of([s2_v[k] for k in range(N_KEYS)])

    pairs = [(a, b) for a in range(PEER_TOPK) for b in range(PEER_TOPK // (a + 1))]
    cand = {ab: v1[ab[0]] + v2[ab[1]] for ab in pairs}
    first_row = [cand[(0, b)] for b in range(PEER_TOPK)]
    rest = [cand[ab] for ab in pairs if ab[0] > 0]
    rest += [jnp.full_like(v1[0], NEG_INF)] * (-len(rest) % PEER_TOPK)
    tau = _merge_top16(first_row, _top16_of(rest))[PEER_TOPK - 1]

    cmax = cand[(0, 0)]
    z = jnp.zeros_like(tau)
    th_rank = [jnp.full_like(tau, POS_INF) for _ in range(PEER_TOPK)]
    for a, b in pairs:
        sel = cand[(a, b)] >= tau
        z = z + jnp.where(sel, jnp.exp(cand[(a, b)] - cmax), 0.0)
        th_rank[a] = jnp.where(sel, v2[b], th_rank[a])
    c_scale = SQRT_HALF / z

    def per_key(k, carry):
        s1k = s1_v[k]
        th = jnp.full_like(s1k, POS_INF)
        for a in reversed(range(PEER_TOPK)):
            th = jnp.where(s1k >= v1[a], th_rank[a], th)
        th_v[grp, k] = th
        c1_v[grp, k] = jnp.exp(s1k - v1[0]) * c_scale
        e2_t[key_rows(k), :] = jnp.exp(s2_v[k] - v2[0])
        return carry

    lax.fori_loop(0, N_KEYS, per_key, 0, unroll=4)


def _peer_kernel(x_ref, g_ref, gf_ref, kq_ref, u_ref, vt_ref, o_ref,
                 xn_scr, xs_scr, s1_t, s2_t, e2_t, s1_v, s2_v, th_v, c1_v, h_scr, w_scr, acc_scr,
                 *, tm, eb, final_norm):
    e = pl.program_id(1)
    n_e = pl.num_programs(1)
    n_tc = tm // LANES
    keys_per_block = eb // N_KEYS
    n_groups = PEER_HEADS * n_tc // SUBLANES

    def tile_row(h, tc):
        return (h * n_tc + tc) * KEY_PITCH

    @pl.when(e == 0)
    def prepare():
        xn_t = _rms(x_ref[...], g_ref[...]).T
        xn_scr[...] = xn_t.astype(BF16)
        xs_scr[...] = (xn_t * SQRT_HALF).astype(BF16)
        for h in range(PEER_HEADS):
            s12 = _dot(kq_ref[h * 2 * N_KEYS:(h + 1) * 2 * N_KEYS, :], xn_scr[...])
            for tc in range(n_tc):
                rows = pl.ds(tile_row(h, tc), N_KEYS)
                s1_t[rows, :] = s12[:N_KEYS, tc * LANES:(tc + 1) * LANES]
                s2_t[rows, :] = s12[N_KEYS:, tc * LANES:(tc + 1) * LANES]

        def thresholds(grp, carry):
            _peer_thresholds(s1_t, s2_t, e2_t, s1_v, s2_v, th_v, c1_v, grp)
            return carry

        lax.fori_loop(0, n_groups, thresholds, 0)
        acc_scr[...] = jnp.zeros_like(acc_scr)

    hs = _dot(u_ref[...], xs_scr[...])
    h_scr[...] = hs * (1.0 + lax.erf(hs))

    def key_rows(p, carry):
        for tc in range(n_tc):
            lanes = slice(tc * LANES, (tc + 1) * LANES)
            gates = [jnp.zeros((N_KEYS, LANES), F32) for _ in range(KEYS_PER_TRIP)]
            for h in range(PEER_HEADS):
                grp, j = divmod(h * n_tc + tc, SUBLANES)
                tile = pl.ds(tile_row(h, tc), N_KEYS)
                s2 = s2_t[tile, :]
                e2 = e2_t[tile, :]
                for k in range(KEYS_PER_TRIP):
                    i1 = e * keys_per_block + p * KEYS_PER_TRIP + k
                    th = th_v[grp, i1, j:j + 1, :]
                    c1 = c1_v[grp, i1, j:j + 1, :]
                    gates[k] = gates[k] + jnp.where(s2 >= th, c1 * e2, 0.0)
            for k in range(KEYS_PER_TRIP):
                rows = pl.ds(pl.multiple_of((p * KEYS_PER_TRIP + k) * N_KEYS, N_KEYS), N_KEYS)
                w_scr[rows, lanes] = (gates[k] * h_scr[rows, lanes]).astype(BF16)
        return carry

    lax.fori_loop(0, keys_per_block // KEYS_PER_TRIP, key_rows, 0)
    acc_scr[...] += _dot(vt_ref[...], w_scr[...])

    @pl.when(e == n_e - 1)
    def _():
        y = x_ref[...] + acc_scr[...].T
        if final_norm:
            y = _rms(y, gf_ref[...])
        o_ref[...] = y


def _peer(x2d, g, g_final, kq_bf, u_bf, v_t_bf, *, layer, final_norm,
          tm=PEER_TOKEN_TILE, eb=PEER_EXPERT_BLOCK):
    n, d = x2d.shape
    n_tc = tm // LANES
    n_e = N_EXPERTS // eb
    const2 = lambda i, e: (0, 0)
    kern = functools.partial(_peer_kernel, tm=tm, eb=eb, final_norm=final_norm)
    n_tiles = PEER_HEADS * n_tc
    assert n_tiles % SUBLANES == 0
    tiles = (n_tiles * KEY_PITCH, LANES)
    per_key = (N_KEYS, SUBLANES, LANES)
    per_key_groups = (n_tiles // SUBLANES,) + per_key
    return pl.pallas_call(
        kern,
        grid=(n // tm, n_e),
        in_specs=[
            pl.BlockSpec((tm, d), lambda i, e: (i, 0)),
            pl.BlockSpec((1, d), const2),
            pl.BlockSpec((1, d), const2),
            pl.BlockSpec((None, PEER_HEADS * 2 * N_KEYS, d), lambda i, e: (layer, 0, 0)),
            pl.BlockSpec((None, eb, d), lambda i, e: (layer, e, 0)),
            pl.BlockSpec((None, d, eb), lambda i, e: (layer, 0, e)),
        ],
        out_specs=pl.BlockSpec((tm, d), lambda i, e: (i, 0)),
        out_shape=jax.ShapeDtypeStruct((n, d), F32),
        scratch_shapes=[
            pltpu.VMEM((d, tm), BF16),
            pltpu.VMEM((d, tm), BF16),
            pltpu.VMEM(tiles, F32),
            pltpu.VMEM(tiles, F32),
            pltpu.VMEM(tiles, F32),
            pltpu.VMEM(per_key, F32),
            pltpu.VMEM(per_key, F32),
            pltpu.VMEM(per_key_groups, F32),
            pltpu.VMEM(per_key_groups, F32),
            pltpu.VMEM((eb, tm), F32),
            pltpu.VMEM((eb, tm), BF16),
            pltpu.VMEM((d, tm), F32),
        ],
        compiler_params=_params(("parallel", "arbitrary")),
        name="peer",
    )(x2d, g, g_final, kq_bf, u_bf, v_t_bf)


def _peer_score_weights_kernel(wq_ref, k1_ref, k2_ref, o_ref):
    o_ref[:N_KEYS, :] = _dot(k1_ref[...], wq_ref[:PEER_HALF, :]).astype(BF16)
    o_ref[N_KEYS:, :] = _dot(k2_ref[...], wq_ref[PEER_HALF:, :]).astype(BF16)


def _peer_score_weights(wq_t_bf, k1_bf, k2_bf):
    n_layers, rows, d = wq_t_bf.shape
    key_spec = pl.BlockSpec((None, N_KEYS, PEER_HALF), lambda l, h: (l, 0, 0))
    return pl.pallas_call(
        _peer_score_weights_kernel,
        grid=(n_layers, PEER_HEADS),
        in_specs=[pl.BlockSpec((None, PEER_QUERY, d), lambda l, h: (l, h, 0)), key_spec, key_spec],
        out_specs=pl.BlockSpec((None, 2 * N_KEYS, d), lambda l, h: (l, h, 0)),
        out_shape=jax.ShapeDtypeStruct((n_layers, PEER_HEADS * 2 * N_KEYS, d), BF16),
        compiler_params=_params(("parallel", "parallel")),
        name="peer_score_weights",
    )(wq_t_bf, k1_bf, k2_bf)


def _peer_tables_kernel(u_ref, v_ref, ub_ref, vt_ref):
    ub_ref[...] = u_ref[...].astype(BF16)
    vt_ref[0] = v_ref[0].T.astype(BF16)


def _peer_tables(peer_u, peer_v, *, tile=1024):
    n_layers, n_exp, d = peer_u.shape
    return pl.pallas_call(
        _peer_tables_kernel,
        grid=(n_layers, n_exp // tile),
        in_specs=[
            pl.BlockSpec((1, tile, d), lambda l, i: (l, i, 0)),
            pl.BlockSpec((1, tile, d), lambda l, i: (l, i, 0)),
        ],
        out_specs=[
            pl.BlockSpec((1, tile, d), lambda l, i: (l, i, 0)),
            pl.BlockSpec((1, d, tile), lambda l, i: (l, 0, i)),
        ],
        out_shape=[
            jax.ShapeDtypeStruct((n_layers, n_exp, d), BF16),
            jax.ShapeDtypeStruct((n_layers, d, n_exp), BF16),
        ],
        compiler_params=_params(("parallel", "parallel")),
        name="peer_tables",
    )(peer_u, peer_v)


def _trunk(x, h0_re, h0_im, w, *, scan_steps):
    bsz, t, d = x.shape
    n = bsz * t
    x2d = x.reshape(n, d)

    if bsz == SUBLANES and t % S5_ROW_TILE == 0:
        u_tm = _norm_matmul(x, w["norm_mix_g"][0], w["perm"], w["ssm_w_in"], tile=S5_ROW_TILE)
        g_tm, hr, hi = _s5_scan(u_tm, h0_re, h0_im, w["disc"], batch=bsz, steps=scan_steps)
        x2d = _glu_out(g_tm, x, w["perm"].T, w["ssm_w_glu"], w["ssm_b_glu"], w["ssm_w_out"],
                       tile=S5_ROW_TILE).reshape(n, d)
    else:
        u_bm = _norm_matmul(x2d[None], w["norm_mix_g"][0], w["perm"], w["ssm_w_in"], tile=n)
        u_tm = u_bm.reshape(bsz, t, d).transpose(1, 0, 2).reshape(n, d)
        g_tm, hr, hi = _s5_scan(u_tm, h0_re, h0_im, w["disc"], batch=bsz, steps=scan_steps)
        g_bm = g_tm.reshape(t, bsz, d).transpose(1, 0, 2).reshape(n, d)
        x2d = _glu_out(g_bm, x2d[None], w["perm"].T, w["ssm_w_glu"], w["ssm_b_glu"], w["ssm_w_out"],
                       tile=n).reshape(n, d)
    x2d = _peer(x2d, w["norm_ffn_g"][0], w["norm_final_g"], *w["peer_weights"], layer=0,
                final_norm=False)

    if t % CHUNK == 0:
        mix, mix_bias = w["mix_full"]
    else:
        mix, mix_bias = w["mix_short"]
    x2d, v = _gmlp(x2d, w["norm_mix_g"][1], w["gmlp_w_in"], w["gmlp_b_in"], w["gmlp_ln_g"], w["gmlp_ln_b"],
                   mix, mix_bias, w["gmlp_w_out"], tile=GMLP_ROW_TILE)
    y2d = _peer(x2d, w["norm_ffn_g"][1], w["norm_final_g"], *w["peer_weights"], layer=1,
                final_norm=True)
    return y2d.reshape(bsz, t, d), hr, hi, v


def kernel(x_prompt, x_sample, state_ssm_re, state_ssm_im, norm_mix_g, norm_ffn_g, norm_final_g, ssm_w_in, ssm_a_re, ssm_a_im, ssm_log_dt, ssm_b_re, ssm_b_im, ssm_c_re, ssm_c_im, ssm_d, ssm_w_glu, ssm_b_glu, ssm_w_out, gmlp_w_in, gmlp_b_in, gmlp_ln_g, gmlp_ln_b, gmlp_w_s, gmlp_b_s, gmlp_w_out, peer_w_q, peer_k1, peer_k2, peer_u, peer_v):
    bp, tp, d = x_prompt.shape
    bs, ts, _ = x_sample.shape
    assert tp % CHUNK == 0 and CHUNK % ts == 0 and d == D_MODEL

    tril = jnp.tril(jnp.ones((CHUNK, CHUNK), F32))
    ws = gmlp_w_s[0] * tril
    bias_full = jnp.broadcast_to(gmlp_b_s[0][:, :, None], (GMLP_HEADS, CHUNK, GMLP_HEAD_DIM))
    reps = CHUNK // ts
    ws_short = jnp.einsum("ab,hij->haibj", jnp.eye(reps, dtype=F32), ws[:, :ts, :ts]).reshape(GMLP_HEADS, CHUNK, CHUNK)
    bias_short = jnp.broadcast_to(jnp.tile(gmlp_b_s[0][:, :ts], (1, reps))[:, :, None],
                                  (GMLP_HEADS, CHUNK, GMLP_HEAD_DIM))

    w = {
        "norm_mix_g": norm_mix_g[:, None, :],
        "norm_ffn_g": norm_ffn_g[:, None, :],
        "norm_final_g": norm_final_g[None, :],
        "perm": _time_major_perm(),
        "ssm_w_in": ssm_w_in[0].astype(BF16),
        "disc": _s5_discretize(ssm_a_re[0], ssm_a_im[0], ssm_log_dt[0], ssm_b_re[0], ssm_b_im[0],
                               ssm_c_re[0], ssm_c_im[0], ssm_d[0]),
        "ssm_w_glu": ssm_w_glu[0].astype(BF16),
        "ssm_b_glu": ssm_b_glu[0][None, :],
        "ssm_w_out": ssm_w_out[0].astype(BF16),
        "gmlp_w_in": gmlp_w_in[0].astype(BF16),
        "gmlp_b_in": gmlp_b_in[0][None, :],
        "gmlp_ln_g": gmlp_ln_g[0][None, :],
        "gmlp_ln_b": gmlp_ln_b[0][None, :],
        "mix_full": (ws.astype(BF16), bias_full),
        "mix_short": (ws_short.astype(BF16), bias_short),
        "gmlp_w_out": gmlp_w_out[0].astype(BF16),
        "peer_weights": (_peer_score_weights(jnp.swapaxes(peer_w_q, 1, 2).astype(BF16), peer_k1.astype(BF16),
                                             peer_k2.astype(BF16)),) + tuple(_peer_tables(peer_u, peer_v)),
    }

    zeros = jnp.zeros((bp, SSM_COLS), F32)
    y_p, hr_p, hi_p, _ = _trunk(x_prompt, zeros, zeros, w, scan_steps=S5_SCAN_STEPS)
    y_s, hr_s, hi_s, v_s = _trunk(x_sample, state_ssm_re[0].reshape(bs, SSM_COLS),
                                  state_ssm_im[0].reshape(bs, SSM_COLS), w, scan_steps=ts)
    st = lambda a, b: a.reshape(1, b, SSM_GROUPS, SSM_STATE)
    return (y_p, y_s, st(hr_p, bp), st(hi_p, bp), st(hr_s, bs), st(hi_s, bs),
            v_s.reshape(1, bs, ts, GMLP_HALF))
```

```python
import functools
import math

import jax
import jax.numpy as jnp
from jax import lax
from jax.experimental import pallas as pl
from jax.experimental.pallas import tpu as pltpu

F32 = jnp.float32
BF16 = jnp.bfloat16

EPS = 1e-6
D_MODEL = 1024
SSM_GROUP = 16
SSM_GROUPS = 64
SSM_STATE = 64
SSM_COLS = SSM_GROUPS * SSM_STATE
GMLP_HALF = 2 * D_MODEL
GMLP_HEADS = 8
GMLP_HEAD_DIM = GMLP_HALF // GMLP_HEADS
CHUNK = 128
PEER_HEADS = 8
N_KEYS = 128
N_EXPERTS = N_KEYS * N_KEYS
PEER_QUERY = 256
PEER_HALF = 128
PEER_TOPK = 16

LANES = 128
SUBLANES = 8
MXU_DIM = 256
VMEM_LIMIT = 56 * 1024 * 1024

NEG_INF = float("-inf")
POS_INF = float("inf")
SQRT_HALF = math.sqrt(0.5)


def _dot(a, b):
    return jnp.dot(a, b, preferred_element_type=F32)


def _rms(x, g):
    ms = jnp.mean(x * x, axis=-1, keepdims=True)
    return x * lax.rsqrt(ms + EPS) * g


def _gelu(x):
    return 0.5 * x * (1.0 + lax.erf(x * SQRT_HALF))


def _params(sem):
    return pltpu.CompilerParams(dimension_semantics=sem, vmem_limit_bytes=VMEM_LIMIT)


S5_ROW_TILE = 128
S5_SCAN_STEPS = 64
GMLP_ROW_TILE = 512
PEER_TOKEN_TILE = 512
PEER_EXPERT_BLOCK = 2048


PERM_STEPS = MXU_DIM // SUBLANES


def _time_major_perm():
    r = jnp.arange(MXU_DIM)
    src = (r % SUBLANES) * PERM_STEPS + r // SUBLANES
    return (src[:, None] == r[None, :]).astype(BF16)


def _norm_matmul_kernel(x_ref, g_ref, p_ref, w_ref, o_ref, *, batch):
    tile = x_ref.shape[1]
    xn = [_rms(x_ref[b], g_ref[...]).astype(BF16) for b in range(batch)]
    if batch == 1:
        xp = xn[0]
    else:
        groups = []
        for tg in range(tile // PERM_STEPS):
            ts = slice(tg * PERM_STEPS, (tg + 1) * PERM_STEPS)
            rows_bt = jnp.concatenate([xn[b][ts] for b in range(batch)], axis=0)
            groups.append(_dot(p_ref[...], rows_bt).astype(BF16))
        xp = jnp.concatenate(groups, axis=0)
    o_ref[...] = _dot(xp, w_ref[...])


def _norm_matmul(x3d, g, perm, w_bf, *, tile):
    bsz, t, d = x3d.shape
    n = w_bf.shape[1]
    assert bsz == 1 or (bsz == SUBLANES and tile % PERM_STEPS == 0)
    return pl.pallas_call(
        functools.partial(_norm_matmul_kernel, batch=bsz),
        grid=(t // tile,),
        in_specs=[
            pl.BlockSpec((bsz, tile, d), lambda i: (0, i, 0)),
            pl.BlockSpec((1, d), lambda i: (0, 0)),
            pl.BlockSpec((MXU_DIM, MXU_DIM), lambda i: (0, 0)),
            pl.BlockSpec((d, n), lambda i: (0, 0)),
        ],
        out_specs=pl.BlockSpec((tile * bsz, n), lambda i: (i, 0)),
        out_shape=jax.ShapeDtypeStruct((t * bsz, n), F32),
        compiler_params=_params(("parallel",)),
        name="s5_in_proj",
    )(x3d, g, perm, w_bf)
SCAN_COLS = 1024
N_SCAN_BLOCKS = SSM_COLS // SCAN_COLS
N_DIAG_BLOCKS = D_MODEL // MXU_DIM
DIAG_COLS = SSM_COLS // N_DIAG_BLOCKS
SCAN_PER_DIAG = DIAG_COLS // SCAN_COLS


def _s5_scan_kernel(u_ref, h0r_ref, h0i_ref, ar_ref, ai_ref, bbr_ref, bbi_ref, ccr_ref, cci_ref, d_ref,
                    g_ref, hr_ref, hi_ref, bur, bui, st_r, st_i, *, steps, batch):
    c = pl.program_id(0)
    n_sub = batch // SUBLANES

    @pl.when(c == 0)
    def _():
        for cb in range(N_SCAN_BLOCKS):
            st_r[cb] = h0r_ref[:, cb * SCAN_COLS:(cb + 1) * SCAN_COLS]
            st_i[cb] = h0i_ref[:, cb * SCAN_COLS:(cb + 1) * SCAN_COLS]

    u = u_ref[...]
    ub = u.astype(BF16)
    for kb in range(N_DIAG_BLOCKS):
        lhs = ub[:, kb * MXU_DIM:(kb + 1) * MXU_DIM]
        pr = _dot(lhs, bbr_ref[kb])
        pi = _dot(lhs, bbi_ref[kb])
        for j in range(SCAN_PER_DIAG):
            bur[kb * SCAN_PER_DIAG + j] = pr[:, j * SCAN_COLS:(j + 1) * SCAN_COLS]
            bui[kb * SCAN_PER_DIAG + j] = pi[:, j * SCAN_COLS:(j + 1) * SCAN_COLS]

    def scan_block(idx, carry):
        cb = idx // n_sub
        s = idx % n_sub
        ar = jnp.broadcast_to(ar_ref[cb], (SUBLANES, SCAN_COLS))
        ai = jnp.broadcast_to(ai_ref[cb], (SUBLANES, SCAN_COLS))
        row0 = pl.multiple_of(s * SUBLANES, SUBLANES)
        h_r = st_r[cb, pl.ds(row0, SUBLANES), :]
        h_i = st_i[cb, pl.ds(row0, SUBLANES), :]

        def step(t, h):
            hr, hi = h
            r = pl.multiple_of(t * batch + row0, SUBLANES)
            nr = ar * hr - ai * hi + bur[cb, pl.ds(r, SUBLANES), :]
            ni = ar * hi + ai * hr + bui[cb, pl.ds(r, SUBLANES), :]
            bur[cb, pl.ds(r, SUBLANES), :] = nr
            bui[cb, pl.ds(r, SUBLANES), :] = ni
            return nr, ni

        h_r, h_i = lax.fori_loop(0, steps, step, (h_r, h_i), unroll=min(steps, 8))
        st_r[cb, pl.ds(row0, SUBLANES), :] = h_r
        st_i[cb, pl.ds(row0, SUBLANES), :] = h_i
        return carry

    lax.fori_loop(0, N_SCAN_BLOCKS * n_sub, scan_block, 0)

    for cb in range(N_SCAN_BLOCKS):
        hr_ref[:, cb * SCAN_COLS:(cb + 1) * SCAN_COLS] = st_r[cb]
        hi_ref[:, cb * SCAN_COLS:(cb + 1) * SCAN_COLS] = st_i[cb]

    for nb in range(N_DIAG_BLOCKS):
        acc = None
        for j in range(SCAN_PER_DIAG):
            cb = nb * SCAN_PER_DIAG + j
            part = (_dot(bur[cb].astype(BF16), ccr_ref[nb, j * SCAN_COLS:(j + 1) * SCAN_COLS, :])
                    + _dot(bui[cb].astype(BF16), cci_ref[nb, j * SCAN_COLS:(j + 1) * SCAN_COLS, :]))
            acc = part if acc is None else acc + part
        cols = slice(nb * MXU_DIM, (nb + 1) * MXU_DIM)
        y = acc + d_ref[:, cols] * u[:, cols]
        g_ref[:, cols] = _gelu(y)


def _s5_scan(u_tm, h0_re, h0_im, disc, *, batch, steps):
    rows, d = u_tm.shape
    t = rows // batch
    n_chunks = t // steps
    blk = steps * batch
    ar, ai, bbr, bbi, ccr, cci, dsk = disc
    const2 = lambda c: (0, 0)
    const3 = lambda c: (0, 0, 0)
    kern = functools.partial(_s5_scan_kernel, steps=steps, batch=batch)
    return pl.pallas_call(
        kern,
        grid=(n_chunks,),
        in_specs=[
            pl.BlockSpec((blk, d), lambda c: (c, 0)),
            pl.BlockSpec((batch, SSM_COLS), const2),
            pl.BlockSpec((batch, SSM_COLS), const2),
            pl.BlockSpec((N_SCAN_BLOCKS, 1, SCAN_COLS), const3),
            pl.BlockSpec((N_SCAN_BLOCKS, 1, SCAN_COLS), const3),
            pl.BlockSpec((N_DIAG_BLOCKS, MXU_DIM, DIAG_COLS), const3),
            pl.BlockSpec((N_DIAG_BLOCKS, MXU_DIM, DIAG_COLS), const3),
            pl.BlockSpec((N_DIAG_BLOCKS, DIAG_COLS, MXU_DIM), const3),
            pl.BlockSpec((N_DIAG_BLOCKS, DIAG_COLS, MXU_DIM), const3),
            pl.BlockSpec((1, d), const2),
        ],
        out_specs=[
            pl.BlockSpec((blk, d), lambda c: (c, 0)),
            pl.BlockSpec((batch, SSM_COLS), const2),
            pl.BlockSpec((batch, SSM_COLS), const2),
        ],
        out_shape=[
            jax.ShapeDtypeStruct((rows, d), F32),
            jax.ShapeDtypeStruct((batch, SSM_COLS), F32),
            jax.ShapeDtypeStruct((batch, SSM_COLS), F32),
        ],
        scratch_shapes=[
            pltpu.VMEM((N_SCAN_BLOCKS, blk, SCAN_COLS), F32),
            pltpu.VMEM((N_SCAN_BLOCKS, blk, SCAN_COLS), F32),
            pltpu.VMEM((N_SCAN_BLOCKS, batch, SCAN_COLS), F32),
            pltpu.VMEM((N_SCAN_BLOCKS, batch, SCAN_COLS), F32),
        ],
        compiler_params=_params(("arbitrary",)),
        name="s5_scan",
    )(u_tm, h0_re, h0_im, ar, ai, bbr, bbi, ccr, cci, dsk)


def _glu_out_kernel(g_ref, x_ref, pt_ref, wg_ref, bg_ref, wo_ref, o_ref, *, batch):
    tile = x_ref.shape[1]
    g = g_ref[...]
    z = _dot(g.astype(BF16), wg_ref[...]) + bg_ref[...]
    o = (g * jax.nn.sigmoid(z)).astype(BF16)
    if batch == 1:
        o_ref[0] = x_ref[0] + _dot(o, wo_ref[...])
    else:
        n_groups = tile // PERM_STEPS
        groups = [_dot(pt_ref[...], o[tg * MXU_DIM:(tg + 1) * MXU_DIM]).astype(BF16) for tg in range(n_groups)]
        mix = _dot(jnp.concatenate(groups, axis=0), wo_ref[...])
        for tg in range(n_groups):
            for b in range(batch):
                ts = slice(tg * PERM_STEPS, (tg + 1) * PERM_STEPS)
                r0 = tg * MXU_DIM + b * PERM_STEPS
                o_ref[b, ts, :] = x_ref[b, ts, :] + mix[r0:r0 + PERM_STEPS]


def _glu_out(g_tm, x3d, perm_t, w_glu_bf, b_glu, w_out_bf, *, tile):
    bsz, t, d = x3d.shape
    assert bsz == 1 or (bsz == SUBLANES and tile % PERM_STEPS == 0)
    const2 = lambda i: (0, 0)
    return pl.pallas_call(
        functools.partial(_glu_out_kernel, batch=bsz),
        grid=(t // tile,),
        in_specs=[
            pl.BlockSpec((tile * bsz, d), lambda i: (i, 0)),
            pl.BlockSpec((bsz, tile, d), lambda i: (0, i, 0)),
            pl.BlockSpec((MXU_DIM, MXU_DIM), const2),
            pl.BlockSpec((d, d), const2),
            pl.BlockSpec((1, d), const2),
            pl.BlockSpec((d, d), const2),
        ],
        out_specs=pl.BlockSpec((bsz, tile, d), lambda i: (0, i, 0)),
        out_shape=jax.ShapeDtypeStruct((bsz, t, d), F32),
        compiler_params=_params(("parallel",)),
        name="s5_glu_out",
    )(g_tm, x3d, perm_t, w_glu_bf, b_glu, w_out_bf)


def _s5_discretize(a_re, a_im, log_dt, b_re, b_im, c_re, c_im, d_skip):
    dt = jnp.exp(log_dt)[:, None]
    mag = jnp.exp(a_re * dt)
    abar_re = mag * jnp.cos(a_im * dt)
    abar_im = mag * jnp.sin(a_im * dt)
    num_re = abar_re - 1.0
    num_im = abar_im
    den = a_re * a_re + a_im * a_im
    f_re = (num_re * a_re + num_im * a_im) / den
    f_im = (num_im * a_re - num_re * a_im) / den
    bb_re = f_re[..., None] * b_re - f_im[..., None] * b_im
    bb_im = f_re[..., None] * b_im + f_im[..., None] * b_re
    gpb = MXU_DIM // SSM_GROUP
    eye = jnp.eye(gpb, dtype=F32)

    def in_blocks(bb):
        bt = jnp.transpose(bb, (0, 2, 1)).reshape(N_DIAG_BLOCKS, gpb, SSM_GROUP, SSM_STATE)
        full = bt[:, :, :, None, :] * eye[None, :, None, :, None]
        return full.reshape(N_DIAG_BLOCKS, MXU_DIM, DIAG_COLS).astype(BF16)

    def out_blocks(cc):
        ct = jnp.transpose(cc, (0, 2, 1)).reshape(N_DIAG_BLOCKS, gpb, SSM_STATE, SSM_GROUP)
        full = ct[:, :, :, None, :] * eye[None, :, None, :, None]
        return full.reshape(N_DIAG_BLOCKS, DIAG_COLS, MXU_DIM).astype(BF16)

    ar = abar_re.reshape(N_SCAN_BLOCKS, 1, SCAN_COLS)
    ai = abar_im.reshape(N_SCAN_BLOCKS, 1, SCAN_COLS)
    return (ar, ai, in_blocks(bb_re), in_blocks(bb_im), out_blocks(c_re), out_blocks(-c_im),
            d_skip.reshape(1, D_MODEL))


def _gmlp_kernel(x_ref, g_ref, win_ref, bin_ref, lng_ref, lnb_ref, mix_ref, mixb_ref, wout_ref,
                 o_ref, v_ref, s_scr, *, rows):
    x = x_ref[...]
    hn = _rms(x, g_ref[...]).astype(BF16)
    z = _gelu(_dot(hn, win_ref[...]) + bin_ref[...])
    u = z[:, :GMLP_HALF]
    v = z[:, GMLP_HALF:]
    mu = jnp.mean(v, axis=-1, keepdims=True)
    vc = v - mu
    vn = vc * lax.rsqrt(jnp.mean(vc * vc, axis=-1, keepdims=True) + EPS) * lng_ref[...] + lnb_ref[...]
    v_ref[...] = vn
    vb = vn.astype(BF16)
    for r in range(rows // CHUNK):
        rs = slice(r * CHUNK, (r + 1) * CHUNK)
        for h in range(GMLP_HEADS):
            cs = slice(h * GMLP_HEAD_DIM, (h + 1) * GMLP_HEAD_DIM)
            mixed = _dot(mix_ref[h], vb[rs, cs]) + mixb_ref[h]
            s_scr[rs, cs] = (u[rs, cs] * mixed).astype(BF16)
    o_ref[...] = x + _dot(s_scr[...], wout_ref[...])


def _gmlp(x2d, g, w_in_bf, b_in, ln_g, ln_b, mix_bf, mix_bias, w_out_bf, *, tile):
    rows, d = x2d.shape
    const2 = lambda i: (0, 0)
    const3 = lambda i: (0, 0, 0)
    kern = functools.partial(_gmlp_kernel, rows=tile)
    return pl.pallas_call(
        kern,
        grid=(rows // tile,),
        in_specs=[
            pl.BlockSpec((tile, d), lambda i: (i, 0)),
            pl.BlockSpec((1, d), const2),
            pl.BlockSpec((d, 2 * GMLP_HALF), const2),
            pl.BlockSpec((1, 2 * GMLP_HALF), const2),
            pl.BlockSpec((1, GMLP_HALF), const2),
            pl.BlockSpec((1, GMLP_HALF), const2),
            pl.BlockSpec((GMLP_HEADS, CHUNK, CHUNK), const3),
            pl.BlockSpec((GMLP_HEADS, CHUNK, GMLP_HEAD_DIM), const3),
            pl.BlockSpec((GMLP_HALF, d), const2),
        ],
        out_specs=[
            pl.BlockSpec((tile, d), lambda i: (i, 0)),
            pl.BlockSpec((tile, GMLP_HALF), lambda i: (i, 0)),
        ],
        out_shape=[
            jax.ShapeDtypeStruct((rows, d), F32),
            jax.ShapeDtypeStruct((rows, GMLP_HALF), F32),
        ],
        scratch_shapes=[pltpu.VMEM((tile, GMLP_HALF), BF16)],
        compiler_params=_params(("parallel",)),
        name="gmlp",
    )(x2d, g, w_in_bf, b_in, ln_g, ln_b, mix_bf, mix_bias, w_out_bf)


KEY_PITCH = N_KEYS + SUBLANES
KEYS_PER_TRIP = 2


def _sorting_network(n):
    pairs = []
    p = 1
    while p < n:
        k = p
        while k >= 1:
            for j in range(k % p, n - k, 2 * k):
                for i in range(min(k, n - j - k)):
                    if (i + j) // (2 * p) == (i + j + k) // (2 * p):
                        pairs.append((i + j, i + j + k))
            k //= 2
        p *= 2
    return pairs


SORT16 = _sorting_network(PEER_TOPK)


def _compare_exchange(v, i, j):
    v[i], v[j] = jnp.maximum(v[i], v[j]), jnp.minimum(v[i], v[j])


def _sort16_desc(v):
    v = list(v)
    for i, j in SORT16:
        _compare_exchange(v, i, j)
    return v


def _merge_top16(a, b):
    c = [jnp.maximum(a[i], b[PEER_TOPK - 1 - i]) for i in range(PEER_TOPK)]
    d = PEER_TOPK // 2
    while d >= 1:
        for i in range(PEER_TOPK):
            if i & d == 0:
                _compare_exchange(c, i, i + d)
        d //= 2
    return c


def _top16_of(values):
    lists = [_sort16_desc(values[i:i + PEER_TOPK]) for i in range(0, len(values), PEER_TOPK)]
    while len(lists) > 1:
        merged = [_merge_top16(lists[i], lists[i + 1]) for i in range(0, len(lists) - 1, 2)]
        if len(lists) % 2:
            merged.append(lists[-1])
        lists = merged
    return lists[0]


def _peer_thresholds(s1_t, s2_t, e2_t, s1_v, s2_v, th_v, c1_v, grp):
    base = grp * (SUBLANES * KEY_PITCH)

    def key_rows(k):
        return pl.ds(base + k, SUBLANES, stride=KEY_PITCH)

    for k in range(N_KEYS):
        s1_v[k] = s1_t[key_rows(k), :]
        s2_v[k] = s2_t[key_rows(k), :]
    v1 = _top16_of([s1_v[k] for k in range(N_KEYS)])
    v2 = _top16_of([s2_v[k] for k in range(N_KEYS)])

    pairs = [(a, b) for a in range(PEER_TOPK) for b in range(PEER_TOPK // (a + 1))]
    cand = {ab: v1[ab[0]] + v2[ab[1]] for ab in pairs}
    first_row = [cand[(0, b)] for b in range(PEER_TOPK)]
    rest = [cand[ab] for ab in pairs if ab[0] > 0]
    rest += [jnp.full_like(v1[0], NEG_INF)] * (-len(rest) % PEER_TOPK)
    tau = _merge_top16(first_row, _top16_of(rest))[PEER_TOPK - 1]

    cmax = cand[(0, 0)]
    z = jnp.zeros_like(tau)
    th_rank = [jnp.full_like(tau, POS_INF) for _ in range(PEER_TOPK)]
    for a, b in pairs:
        sel = cand[(a, b)] >= tau
        z = z + jnp.where(sel, jnp.exp(cand[(a, b)] - cmax), 0.0)
        th_rank[a] = jnp.where(sel, v2[b], th_rank[a])
    c_scale = SQRT_HALF / z

    def per_key(k, carry):
        s1k = s1_v[k]
        th = jnp.full_like(s1k, POS_INF)
        for a in reversed(range(PEER_TOPK)):
            th = jnp.where(s1k >= v1[a], th_rank[a], th)
        th_v[grp, k] = th
        c1_v[grp, k] = jnp.exp(s1k - v1[0]) * c_scale
        e2_t[key_rows(k), :] = jnp.exp(s2_v[k] - v2[0])
        return carry

    lax.fori_loop(0, N_KEYS, per_key, 0, unroll=8)


def _peer_kernel(x_ref, g_ref, gf_ref, kq_ref, u_ref, vt_ref, o_ref,
                 xn_scr, xs_scr, s1_t, s2_t, e2_t, s1_v, s2_v, th_v, c1_v, h_scr, w_scr, acc_scr,
                 *, tm, eb, final_norm):
    e = pl.program_id(1)
    n_e = pl.num_programs(1)
    n_tc = tm // LANES
    keys_per_block = eb // N_KEYS
    n_groups = PEER_HEADS * n_tc // SUBLANES

    def tile_row(h, tc):
        return (h * n_tc + tc) * KEY_PITCH

    @pl.when(e == 0)
    def prepare():
        xn_t = _rms(x_ref[...], g_ref[...]).T
        xn_scr[...] = xn_t.astype(BF16)
        xs_scr[...] = (xn_t * SQRT_HALF).astype(BF16)
        for h in range(PEER_HEADS):
            s12 = _dot(kq_ref[h * 2 * N_KEYS:(h + 1) * 2 * N_KEYS, :], xn_scr[...])
            for tc in range(n_tc):
                rows = pl.ds(tile_row(h, tc), N_KEYS)
                s1_t[rows, :] = s12[:N_KEYS, tc * LANES:(tc + 1) * LANES]
                s2_t[rows, :] = s12[N_KEYS:, tc * LANES:(tc + 1) * LANES]

        def thresholds(grp, carry):
            _peer_thresholds(s1_t, s2_t, e2_t, s1_v, s2_v, th_v, c1_v, grp)
            return carry

        lax.fori_loop(0, n_groups, thresholds, 0)
        acc_scr[...] = jnp.zeros_like(acc_scr)

    hs = _dot(u_ref[...], xs_scr[...])
    h_scr[...] = hs * (1.0 + lax.erf(hs))

    def key_rows(p, carry):
        for tc in range(n_tc):
            lanes = slice(tc * LANES, (tc + 1) * LANES)
            gates = [jnp.zeros((N_KEYS, LANES), F32) for _ in range(KEYS_PER_TRIP)]
            for h in range(PEER_HEADS):
                grp, j = divmod(h * n_tc + tc, SUBLANES)
                tile = pl.ds(tile_row(h, tc), N_KEYS)
                s2 = s2_t[tile, :]
                e2 = e2_t[tile, :]
                for k in range(KEYS_PER_TRIP):
                    i1 = e * keys_per_block + p * KEYS_PER_TRIP + k
                    th = th_v[grp, i1, j:j + 1, :]
                    c1 = c1_v[grp, i1, j:j + 1, :]
                    gates[k] = gates[k] + jnp.where(s2 >= th, c1 * e2, 0.0)
            for k in range(KEYS_PER_TRIP):
                rows = pl.ds(pl.multiple_of((p * KEYS_PER_TRIP + k) * N_KEYS, N_KEYS), N_KEYS)
                w_scr[rows, lanes] = (gates[k] * h_scr[rows, lanes]).astype(BF16)
        return carry

    lax.fori_loop(0, keys_per_block // KEYS_PER_TRIP, key_rows, 0)
    acc_scr[...] += _dot(vt_ref[...], w_scr[...])

    @pl.when(e == n_e - 1)
    def _():
        y = x_ref[...] + acc_scr[...].T
        if final_norm:
            y = _rms(y, gf_ref[...])
        o_ref[...] = y


def _peer(x2d, g, g_final, kq_bf, u_bf, v_t_bf, *, layer, final_norm,
          tm=PEER_TOKEN_TILE, eb=PEER_EXPERT_BLOCK):
    n, d = x2d.shape
    n_tc = tm // LANES
    n_e = N_EXPERTS // eb
    const2 = lambda i, e: (0, 0)
    kern = functools.partial(_peer_kernel, tm=tm, eb=eb, final_norm=final_norm)
    n_tiles = PEER_HEADS * n_tc
    assert n_tiles % SUBLANES == 0
    tiles = (n_tiles * KEY_PITCH, LANES)
    per_key = (N_KEYS, SUBLANES, LANES)
    per_key_groups = (n_tiles // SUBLANES,) + per_key
    return pl.pallas_call(
        kern,
        grid=(n // tm, n_e),
        in_specs=[
            pl.BlockSpec((tm, d), lambda i, e: (i, 0)),
            pl.BlockSpec((1, d), const2),
            pl.BlockSpec((1, d), const2),
            pl.BlockSpec((None, PEER_HEADS * 2 * N_KEYS, d), lambda i, e: (layer, 0, 0)),
            pl.BlockSpec((None, eb, d), lambda i, e: (layer, e, 0)),
            pl.BlockSpec((None, d, eb), lambda i, e: (layer, 0, e)),
        ],
        out_specs=pl.BlockSpec((tm, d), lambda i, e: (i, 0)),
        out_shape=jax.ShapeDtypeStruct((n, d), F32),
        scratch_shapes=[
            pltpu.VMEM((d, tm), BF16),
            pltpu.VMEM((d, tm), BF16),
            pltpu.VMEM(tiles, F32),
            pltpu.VMEM(tiles, F32),
            pltpu.VMEM(tiles, F32),
            pltpu.VMEM(per_key, F32),
            pltpu.VMEM(per_key, F32),
            pltpu.VMEM(per_key_groups, F32),
            pltpu.VMEM(per_key_groups, F32),
            pltpu.VMEM((eb, tm), F32),
            pltpu.VMEM((eb, tm), BF16),
            pltpu.VMEM((d, tm), F32),
        ],
        compiler_params=_params(("parallel", "arbitrary")),
        name="peer",
    )(x2d, g, g_final, kq_bf, u_bf, v_t_bf)


def _peer_score_weights_kernel(wq_ref, k1_ref, k2_ref, o_ref):
    o_ref[:N_KEYS, :] = _dot(k1_ref[...], wq_ref[:PEER_HALF, :]).astype(BF16)
    o_ref[N_KEYS:, :] = _dot(k2_ref[...], wq_ref[PEER_HALF:, :]).astype(BF16)


def _peer_score_weights(wq_t_bf, k1_bf, k2_bf):
    n_layers, rows, d = wq_t_bf.shape
    key_spec = pl.BlockSpec((None, N_KEYS, PEER_HALF), lambda l, h: (l, 0, 0))
    return pl.pallas_call(
        _peer_score_weights_kernel,
        grid=(n_layers, PEER_HEADS),
        in_specs=[pl.BlockSpec((None, PEER_QUERY, d), lambda l, h: (l, h, 0)), key_spec, key_spec],
        out_specs=pl.BlockSpec((None, 2 * N_KEYS, d), lambda l, h: (l, h, 0)),
        out_shape=jax.ShapeDtypeStruct((n_layers, PEER_HEADS * 2 * N_KEYS, d), BF16),
        compiler_params=_params(("parallel", "parallel")),
        name="peer_score_weights",
    )(wq_t_bf, k1_bf, k2_bf)


def _peer_tables_kernel(u_ref, v_ref, ub_ref, vt_ref):
    ub_ref[...] = u_ref[...].astype(BF16)
    vt_ref[0] = v_ref[0].T.astype(BF16)


def _peer_tables(peer_u, peer_v, *, tile=1024):
    n_layers, n_exp, d = peer_u.shape
    return pl.pallas_call(
        _peer_tables_kernel,
        grid=(n_layers, n_exp // tile),
        in_specs=[
            pl.BlockSpec((1, tile, d), lambda l, i: (l, i, 0)),
            pl.BlockSpec((1, tile, d), lambda l, i: (l, i, 0)),
        ],
        out_specs=[
            pl.BlockSpec((1, tile, d), lambda l, i: (l, i, 0)),
            pl.BlockSpec((1, d, tile), lambda l, i: (l, 0, i)),
        ],
        out_shape=[
            jax.ShapeDtypeStruct((n_layers, n_exp, d), BF16),
            jax.ShapeDtypeStruct((n_layers, d, n_exp), BF16),
        ],
        compiler_params=_params(("parallel", "parallel")),
        name="peer_tables",
    )(peer_u, peer_v)


def _trunk(x, h0_re, h0_im, w, *, scan_steps):
    bsz, t, d = x.shape
    n = bsz * t
    x2d = x.reshape(n, d)

    if bsz == SUBLANES and t % S5_ROW_TILE == 0:
        u_tm = _norm_matmul(x, w["norm_mix_g"][0], w["perm"], w["ssm_w_in"], tile=S5_ROW_TILE)
        g_tm, hr, hi = _s5_scan(u_tm, h0_re, h0_im, w["disc"], batch=bsz, steps=scan_steps)
        x2d = _glu_out(g_tm, x, w["perm"].T, w["ssm_w_glu"], w["ssm_b_glu"], w["ssm_w_out"],
                       tile=S5_ROW_TILE).reshape(n, d)
    else:
        u_bm = _norm_matmul(x2d[None], w["norm_mix_g"][0], w["perm"], w["ssm_w_in"], tile=n)
        u_tm = u_bm.reshape(bsz, t, d).transpose(1, 0, 2).reshape(n, d)
        g_tm, hr, hi = _s5_scan(u_tm, h0_re, h0_im, w["disc"], batch=bsz, steps=scan_steps)
        g_bm = g_tm.reshape(t, bsz, d).transpose(1, 0, 2).reshape(n, d)
        x2d = _glu_out(g_bm, x2d[None], w["perm"].T, w["ssm_w_glu"], w["ssm_b_glu"], w["ssm_w_out"],
                       tile=n).reshape(n, d)
    x2d = _peer(x2d, w["norm_ffn_g"][0], w["norm_final_g"], *w["peer_weights"], layer=0,
                final_norm=False)

    if t % CHUNK == 0:
        mix, mix_bias = w["mix_full"]
    else:
        mix, mix_bias = w["mix_short"]
    x2d, v = _gmlp(x2d, w["norm_mix_g"][1], w["gmlp_w_in"], w["gmlp_b_in"], w["gmlp_ln_g"], w["gmlp_ln_b"],
                   mix, mix_bias, w["gmlp_w_out"], tile=GMLP_ROW_TILE)
    y2d = _peer(x2d, w["norm_ffn_g"][1], w["norm_final_g"], *w["peer_weights"], layer=1,
                final_norm=True)
    return y2d.reshape(bsz, t, d), hr, hi, v


def kernel(x_prompt, x_sample, state_ssm_re, state_ssm_im, norm_mix_g, norm_ffn_g, norm_final_g, ssm_w_in, ssm_a_re, ssm_a_im, ssm_log_dt, ssm_b_re, ssm_b_im, ssm_c_re, ssm_c_im, ssm_d, ssm_w_glu, ssm_b_glu, ssm_w_out, gmlp_w_in, gmlp_b_in, gmlp_ln_g, gmlp_ln_b, gmlp_w_s, gmlp_b_s, gmlp_w_out, peer_w_q, peer_k1, peer_k2, peer_u, peer_v):
    bp, tp, d = x_prompt.shape
    bs, ts, _ = x_sample.shape
    assert tp % CHUNK == 0 and CHUNK % ts == 0 and d == D_MODEL

    tril = jnp.tril(jnp.ones((CHUNK, CHUNK), F32))
    ws = gmlp_w_s[0] * tril
    bias_full = jnp.broadcast_to(gmlp_b_s[0][:, :, None], (GMLP_HEADS, CHUNK, GMLP_HEAD_DIM))
    reps = CHUNK // ts
    ws_short = jnp.einsum("ab,hij->haibj", jnp.eye(reps, dtype=F32), ws[:, :ts, :ts]).reshape(GMLP_HEADS, CHUNK, CHUNK)
    bias_short = jnp.broadcast_to(jnp.tile(gmlp_b_s[0][:, :ts], (1, reps))[:, :, None],
                                  (GMLP_HEADS, CHUNK, GMLP_HEAD_DIM))

    w = {
        "norm_mix_g": norm_mix_g[:, None, :],
        "norm_ffn_g": norm_ffn_g[:, None, :],
        "norm_final_g": norm_final_g[None, :],
        "perm": _time_major_perm(),
        "ssm_w_in": ssm_w_in[0].astype(BF16),
        "disc": _s5_discretize(ssm_a_re[0], ssm_a_im[0], ssm_log_dt[0], ssm_b_re[0], ssm_b_im[0],
                               ssm_c_re[0], ssm_c_im[0], ssm_d[0]),
        "ssm_w_glu": ssm_w_glu[0].astype(BF16),
        "ssm_b_glu": ssm_b_glu[0][None, :],
        "ssm_w_out": ssm_w_out[0].astype(BF16),
        "gmlp_w_in": gmlp_w_in[0].astype(BF16),
        "gmlp_b_in": gmlp_b_in[0][None, :],
        "gmlp_ln_g": gmlp_ln_g[0][None, :],
        "gmlp_ln_b": gmlp_ln_b[0][None, :],
        "mix_full": (ws.astype(BF16), bias_full),
        "mix_short": (ws_short.astype(BF16), bias_short),
        "gmlp_w_out": gmlp_w_out[0].astype(BF16),
        "peer_weights": (_peer_score_weights(jnp.swapaxes(peer_w_q, 1, 2).astype(BF16), peer_k1.astype(BF16),
                                             peer_k2.astype(BF16)),) + tuple(_peer_tables(peer_u, peer_v)),
    }

    zeros = jnp.zeros((bp, SSM_COLS), F32)
    y_p, hr_p, hi_p, _ = _trunk(x_prompt, zeros, zeros, w, scan_steps=S5_SCAN_STEPS)
    y_s, hr_s, hi_s, v_s = _trunk(x_sample, state_ssm_re[0].reshape(bs, SSM_COLS),
                                  state_ssm_im[0].reshape(bs, SSM_COLS), w, scan_steps=ts)
    st = lambda a, b: a.reshape(1, b, SSM_GROUPS, SSM_STATE)
    return (y_p, y_s, st(hr_p, bp), st(hi_p, bp), st(hr_s, bs), st(hi_s, bs),
            v_s.reshape(1, bs, ts, GMLP_HALF))
```

```python
import functools
import math

import jax
import jax.numpy as jnp
from jax import lax
from jax.experimental import pallas as pl
from jax.experimental.pallas import tpu as pltpu

F32 = jnp.float32
BF16 = jnp.bfloat16

EPS = 1e-6
D_MODEL = 1024
SSM_GROUP = 16
SSM_GROUPS = 64
SSM_STATE = 64
SSM_COLS = SSM_GROUPS * SSM_STATE
GMLP_HALF = 2 * D_MODEL
GMLP_HEADS = 8
GMLP_HEAD_DIM = GMLP_HALF // GMLP_HEADS
CHUNK = 128
PEER_HEADS = 8
N_KEYS = 128
N_EXPERTS = N_KEYS * N_KEYS
PEER_QUERY = 256
PEER_HALF = 128
PEER_TOPK = 16

LANES = 128
SUBLANES = 8
MXU_DIM = 256
VMEM_LIMIT = 56 * 1024 * 1024

NEG_INF = float("-inf")
POS_INF = float("inf")
SQRT_HALF = math.sqrt(0.5)


def _dot(a, b):
    return jnp.dot(a, b, preferred_element_type=F32)


def _rms(x, g):
    ms = jnp.mean(x * x, axis=-1, keepdims=True)
    return x * lax.rsqrt(ms + EPS) * g


def _gelu(x):
    return 0.5 * x * (1.0 + lax.erf(x * SQRT_HALF))


def _params(sem):
    return pltpu.CompilerParams(dimension_semantics=sem, vmem_limit_bytes=VMEM_LIMIT)


S5_ROW_TILE = 128
S5_SCAN_STEPS = 64
GMLP_ROW_TILE = 512
PEER_TOKEN_TILE = 512
PEER_EXPERT_BLOCK = 1024


PERM_STEPS = MXU_DIM // SUBLANES


def _time_major_perm():
    r = jnp.arange(MXU_DIM)
    src = (r % SUBLANES) * PERM_STEPS + r // SUBLANES
    return (src[:, None] == r[None, :]).astype(BF16)


def _norm_matmul_kernel(x_ref, g_ref, p_ref, w_ref, o_ref, *, batch):
    tile = x_ref.shape[1]
    xn = [_rms(x_ref[b], g_ref[...]).astype(BF16) for b in range(batch)]
    if batch == 1:
        xp = xn[0]
    else:
        groups = []
        for tg in range(tile // PERM_STEPS):
            ts = slice(tg * PERM_STEPS, (tg + 1) * PERM_STEPS)
            rows_bt = jnp.concatenate([xn[b][ts] for b in range(batch)], axis=0)
            groups.append(_dot(p_ref[...], rows_bt).astype(BF16))
        xp = jnp.concatenate(groups, axis=0)
    o_ref[...] = _dot(xp, w_ref[...])


def _norm_matmul(x3d, g, perm, w_bf, *, tile):
    bsz, t, d = x3d.shape
    n = w_bf.shape[1]
    assert bsz == 1 or (bsz == SUBLANES and tile % PERM_STEPS == 0)
    return pl.pallas_call(
        functools.partial(_norm_matmul_kernel, batch=bsz),
        grid=(t // tile,),
        in_specs=[
            pl.BlockSpec((bsz, tile, d), lambda i: (0, i, 0)),
            pl.BlockSpec((1, d), lambda i: (0, 0)),
            pl.BlockSpec((MXU_DIM, MXU_DIM), lambda i: (0, 0)),
            pl.BlockSpec((d, n), lambda i: (0, 0)),
        ],
        out_specs=pl.BlockSpec((tile * bsz, n), lambda i: (i, 0)),
        out_shape=jax.ShapeDtypeStruct((t * bsz, n), F32),
        compiler_params=_params(("parallel",)),
        name="s5_in_proj",
    )(x3d, g, perm, w_bf)
SCAN_COLS = 1024
N_SCAN_BLOCKS = SSM_COLS // SCAN_COLS
N_DIAG_BLOCKS = D_MODEL // MXU_DIM
DIAG_COLS = SSM_COLS // N_DIAG_BLOCKS
SCAN_PER_DIAG = DIAG_COLS // SCAN_COLS


def _s5_scan_kernel(u_ref, h0r_ref, h0i_ref, ar_ref, ai_ref, bbr_ref, bbi_ref, ccr_ref, cci_ref, d_ref,
                    g_ref, hr_ref, hi_ref, bur, bui, st_r, st_i, *, steps, batch):
    c = pl.program_id(0)
    n_sub = batch // SUBLANES

    @pl.when(c == 0)
    def _():
        for cb in range(N_SCAN_BLOCKS):
            st_r[cb] = h0r_ref[:, cb * SCAN_COLS:(cb + 1) * SCAN_COLS]
            st_i[cb] = h0i_ref[:, cb * SCAN_COLS:(cb + 1) * SCAN_COLS]

    u = u_ref[...]
    ub = u.astype(BF16)
    for kb in range(N_DIAG_BLOCKS):
        lhs = ub[:, kb * MXU_DIM:(kb + 1) * MXU_DIM]
        pr = _dot(lhs, bbr_ref[kb])
        pi = _dot(lhs, bbi_ref[kb])
        for j in range(SCAN_PER_DIAG):
            bur[kb * SCAN_PER_DIAG + j] = pr[:, j * SCAN_COLS:(j + 1) * SCAN_COLS]
            bui[kb * SCAN_PER_DIAG + j] = pi[:, j * SCAN_COLS:(j + 1) * SCAN_COLS]

    def scan_block(idx, carry):
        cb = idx // n_sub
        s = idx % n_sub
        ar = jnp.broadcast_to(ar_ref[cb], (SUBLANES, SCAN_COLS))
        ai = jnp.broadcast_to(ai_ref[cb], (SUBLANES, SCAN_COLS))
        row0 = pl.multiple_of(s * SUBLANES, SUBLANES)
        h_r = st_r[cb, pl.ds(row0, SUBLANES), :]
        h_i = st_i[cb, pl.ds(row0, SUBLANES), :]

        def step(t, h):
            hr, hi = h
            r = pl.multiple_of(t * batch + row0, SUBLANES)
            nr = ar * hr - ai * hi + bur[cb, pl.ds(r, SUBLANES), :]
            ni = ar * hi + ai * hr + bui[cb, pl.ds(r, SUBLANES), :]
            bur[cb, pl.ds(r, SUBLANES), :] = nr
            bui[cb, pl.ds(r, SUBLANES), :] = ni
            return nr, ni

        h_r, h_i = lax.fori_loop(0, steps, step, (h_r, h_i), unroll=min(steps, 8))
        st_r[cb, pl.ds(row0, SUBLANES), :] = h_r
        st_i[cb, pl.ds(row0, SUBLANES), :] = h_i
        return carry

    lax.fori_loop(0, N_SCAN_BLOCKS * n_sub, scan_block, 0)

    for cb in range(N_SCAN_BLOCKS):
        hr_ref[:, cb * SCAN_COLS:(cb + 1) * SCAN_COLS] = st_r[cb]
        hi_ref[:, cb * SCAN_COLS:(cb + 1) * SCAN_COLS] = st_i[cb]

    for nb in range(N_DIAG_BLOCKS):
        acc = None
        for j in range(SCAN_PER_DIAG):
            cb = nb * SCAN_PER_DIAG + j
            part = (_dot(bur[cb].astype(BF16), ccr_ref[nb, j * SCAN_COLS:(j + 1) * SCAN_COLS, :])
                    + _dot(bui[cb].astype(BF16), cci_ref[nb, j * SCAN_COLS:(j + 1) * SCAN_COLS, :]))
            acc = part if acc is None else acc + part
        cols = slice(nb * MXU_DIM, (nb + 1) * MXU_DIM)
        y = acc + d_ref[:, cols] * u[:, cols]
        g_ref[:, cols] = _gelu(y)


def _s5_scan(u_tm, h0_re, h0_im, disc, *, batch, steps):
    rows, d = u_tm.shape
    t = rows // batch
    n_chunks = t // steps
    blk = steps * batch
    ar, ai, bbr, bbi, ccr, cci, dsk = disc
    const2 = lambda c: (0, 0)
    const3 = lambda c: (0, 0, 0)
    kern = functools.partial(_s5_scan_kernel, steps=steps, batch=batch)
    return pl.pallas_call(
        kern,
        grid=(n_chunks,),
        in_specs=[
            pl.BlockSpec((blk, d), lambda c: (c, 0)),
            pl.BlockSpec((batch, SSM_COLS), const2),
            pl.BlockSpec((batch, SSM_COLS), const2),
            pl.BlockSpec((N_SCAN_BLOCKS, 1, SCAN_COLS), const3),
            pl.BlockSpec((N_SCAN_BLOCKS, 1, SCAN_COLS), const3),
            pl.BlockSpec((N_DIAG_BLOCKS, MXU_DIM, DIAG_COLS), const3),
            pl.BlockSpec((N_DIAG_BLOCKS, MXU_DIM, DIAG_COLS), const3),
            pl.BlockSpec((N_DIAG_BLOCKS, DIAG_COLS, MXU_DIM), const3),
            pl.BlockSpec((N_DIAG_BLOCKS, DIAG_COLS, MXU_DIM), const3),
            pl.BlockSpec((1, d), const2),
        ],
        out_specs=[
            pl.BlockSpec((blk, d), lambda c: (c, 0)),
            pl.BlockSpec((batch, SSM_COLS), const2),
            pl.BlockSpec((batch, SSM_COLS), const2),
        ],
        out_shape=[
            jax.ShapeDtypeStruct((rows, d), F32),
            jax.ShapeDtypeStruct((batch, SSM_COLS), F32),
            jax.ShapeDtypeStruct((batch, SSM_COLS), F32),
        ],
        scratch_shapes=[
            pltpu.VMEM((N_SCAN_BLOCKS, blk, SCAN_COLS), F32),
            pltpu.VMEM((N_SCAN_BLOCKS, blk, SCAN_COLS), F32),
            pltpu.VMEM((N_SCAN_BLOCKS, batch, SCAN_COLS), F32),
            pltpu.VMEM((N_SCAN_BLOCKS, batch, SCAN_COLS), F32),
        ],
        compiler_params=_params(("arbitrary",)),
        name="s5_scan",
    )(u_tm, h0_re, h0_im, ar, ai, bbr, bbi, ccr, cci, dsk)


def _glu_out_kernel(g_ref, x_ref, pt_ref, wg_ref, bg_ref, wo_ref, o_ref, *, batch):
    tile = x_ref.shape[1]
    g = g_ref[...]
    z = _dot(g.astype(BF16), wg_ref[...]) + bg_ref[...]
    o = (g * jax.nn.sigmoid(z)).astype(BF16)
    if batch == 1:
        o_ref[0] = x_ref[0] + _dot(o, wo_ref[...])
    else:
        n_groups = tile // PERM_STEPS
        groups = [_dot(pt_ref[...], o[tg * MXU_DIM:(tg + 1) * MXU_DIM]).astype(BF16) for tg in range(n_groups)]
        mix = _dot(jnp.concatenate(groups, axis=0), wo_ref[...])
        for tg in range(n_groups):
            for b in range(batch):
                ts = slice(tg * PERM_STEPS, (tg + 1) * PERM_STEPS)
                r0 = tg * MXU_DIM + b * PERM_STEPS
                o_ref[b, ts, :] = x_ref[b, ts, :] + mix[r0:r0 + PERM_STEPS]


def _glu_out(g_tm, x3d, perm_t, w_glu_bf, b_glu, w_out_bf, *, tile):
    bsz, t, d = x3d.shape
    assert bsz == 1 or (bsz == SUBLANES and tile % PERM_STEPS == 0)
    const2 = lambda i: (0, 0)
    return pl.pallas_call(
        functools.partial(_glu_out_kernel, batch=bsz),
        grid=(t // tile,),
        in_specs=[
            pl.BlockSpec((tile * bsz, d), lambda i: (i, 0)),
            pl.BlockSpec((bsz, tile, d), lambda i: (0, i, 0)),
            pl.BlockSpec((MXU_DIM, MXU_DIM), const2),
            pl.BlockSpec((d, d), const2),
            pl.BlockSpec((1, d), const2),
            pl.BlockSpec((d, d), const2),
        ],
        out_specs=pl.BlockSpec((bsz, tile, d), lambda i: (0, i, 0)),
        out_shape=jax.ShapeDtypeStruct((bsz, t, d), F32),
        compiler_params=_params(("parallel",)),
        name="s5_glu_out",
    )(g_tm, x3d, perm_t, w_glu_bf, b_glu, w_out_bf)


def _s5_discretize(a_re, a_im, log_dt, b_re, b_im, c_re, c_im, d_skip):
    dt = jnp.exp(log_dt)[:, None]
    mag = jnp.exp(a_re * dt)
    abar_re = mag * jnp.cos(a_im * dt)
    abar_im = mag * jnp.sin(a_im * dt)
    num_re = abar_re - 1.0
    num_im = abar_im
    den = a_re * a_re + a_im * a_im
    f_re = (num_re * a_re + num_im * a_im) / den
    f_im = (num_im * a_re - num_re * a_im) / den
    bb_re = f_re[..., None] * b_re - f_im[..., None] * b_im
    bb_im = f_re[..., None] * b_im + f_im[..., None] * b_re
    gpb = MXU_DIM // SSM_GROUP
    eye = jnp.eye(gpb, dtype=F32)

    def in_blocks(bb):
        bt = jnp.transpose(bb, (0, 2, 1)).reshape(N_DIAG_BLOCKS, gpb, SSM_GROUP, SSM_STATE)
        full = bt[:, :, :, None, :] * eye[None, :, None, :, None]
        return full.reshape(N_DIAG_BLOCKS, MXU_DIM, DIAG_COLS).astype(BF16)

    def out_blocks(cc):
        ct = jnp.transpose(cc, (0, 2, 1)).reshape(N_DIAG_BLOCKS, gpb, SSM_STATE, SSM_GROUP)
        full = ct[:, :, :, None, :] * eye[None, :, None, :, None]
        return full.reshape(N_DIAG_BLOCKS, DIAG_COLS, MXU_DIM).astype(BF16)

    ar = abar_re.reshape(N_SCAN_BLOCKS, 1, SCAN_COLS)
    ai = abar_im.reshape(N_SCAN_BLOCKS, 1, SCAN_COLS)
    return (ar, ai, in_blocks(bb_re), in_blocks(bb_im), out_blocks(c_re), out_blocks(-c_im),
            d_skip.reshape(1, D_MODEL))


def _gmlp_kernel(x_ref, g_ref, win_ref, bin_ref, lng_ref, lnb_ref, mix_ref, mixb_ref, wout_ref,
                 o_ref, v_ref, s_scr, *, rows):
    x = x_ref[...]
    hn = _rms(x, g_ref[...]).astype(BF16)
    z = _gelu(_dot(hn, win_ref[...]) + bin_ref[...])
    u = z[:, :GMLP_HALF]
    v = z[:, GMLP_HALF:]
    mu = jnp.mean(v, axis=-1, keepdims=True)
    vc = v - mu
    vn = vc * lax.rsqrt(jnp.mean(vc * vc, axis=-1, keepdims=True) + EPS) * lng_ref[...] + lnb_ref[...]
    v_ref[...] = vn
    vb = vn.astype(BF16)
    for r in range(rows // CHUNK):
        rs = slice(r * CHUNK, (r + 1) * CHUNK)
        for h in range(GMLP_HEADS):
            cs = slice(h * GMLP_HEAD_DIM, (h + 1) * GMLP_HEAD_DIM)
            mixed = _dot(mix_ref[h], vb[rs, cs]) + mixb_ref[h]
            s_scr[rs, cs] = (u[rs, cs] * mixed).astype(BF16)
    o_ref[...] = x + _dot(s_scr[...], wout_ref[...])


def _gmlp(x2d, g, w_in_bf, b_in, ln_g, ln_b, mix_bf, mix_bias, w_out_bf, *, tile):
    rows, d = x2d.shape
    const2 = lambda i: (0, 0)
    const3 = lambda i: (0, 0, 0)
    kern = functools.partial(_gmlp_kernel, rows=tile)
    return pl.pallas_call(
        kern,
        grid=(rows // tile,),
        in_specs=[
            pl.BlockSpec((tile, d), lambda i: (i, 0)),
            pl.BlockSpec((1, d), const2),
            pl.BlockSpec((d, 2 * GMLP_HALF), const2),
            pl.BlockSpec((1, 2 * GMLP_HALF), const2),
            pl.BlockSpec((1, GMLP_HALF), const2),
            pl.BlockSpec((1, GMLP_HALF), const2),
            pl.BlockSpec((GMLP_HEADS, CHUNK, CHUNK), const3),
            pl.BlockSpec((GMLP_HEADS, CHUNK, GMLP_HEAD_DIM), const3),
            pl.BlockSpec((GMLP_HALF, d), const2),
        ],
        out_specs=[
            pl.BlockSpec((tile, d), lambda i: (i, 0)),
            pl.BlockSpec((tile, GMLP_HALF), lambda i: (i, 0)),
        ],
        out_shape=[
            jax.ShapeDtypeStruct((rows, d), F32),
            jax.ShapeDtypeStruct((rows, GMLP_HALF), F32),
        ],
        scratch_shapes=[pltpu.VMEM((tile, GMLP_HALF), BF16)],
        compiler_params=_params(("parallel",)),
        name="gmlp",
    )(x2d, g, w_in_bf, b_in, ln_g, ln_b, mix_bf, mix_bias, w_out_bf)


KEY_PITCH = N_KEYS + SUBLANES
KEYS_PER_TRIP = 2


def _sorting_network(n):
    pairs = []
    p = 1
    while p < n:
        k = p
        while k >= 1:
            for j in range(k % p, n - k, 2 * k):
                for i in range(min(k, n - j - k)):
                    if (i + j) // (2 * p) == (i + j + k) // (2 * p):
                        pairs.append((i + j, i + j + k))
            k //= 2
        p *= 2
    return pairs


SORT16 = _sorting_network(PEER_TOPK)


def _compare_exchange(v, i, j):
    v[i], v[j] = jnp.maximum(v[i], v[j]), jnp.minimum(v[i], v[j])


def _sort16_desc(v):
    v = list(v)
    for i, j in SORT16:
        _compare_exchange(v, i, j)
    return v


def _merge_top16(a, b):
    c = [jnp.maximum(a[i], b[PEER_TOPK - 1 - i]) for i in range(PEER_TOPK)]
    d = PEER_TOPK // 2
    while d >= 1:
        for i in range(PEER_TOPK):
            if i & d == 0:
                _compare_exchange(c, i, i + d)
        d //= 2
    return c


def _top16_of(values):
    lists = [_sort16_desc(values[i:i + PEER_TOPK]) for i in range(0, len(values), PEER_TOPK)]
    while len(lists) > 1:
        merged = [_merge_top16(lists[i], lists[i + 1]) for i in range(0, len(lists) - 1, 2)]
        if len(lists) % 2:
            merged.append(lists[-1])
        lists = merged
    return lists[0]


def _peer_thresholds(s1_t, s2_t, e2_t, s1_v, s2_v, th_v, c1_v, grp):
    base = grp * (SUBLANES * KEY_PITCH)

    def key_rows(k):
        return pl.ds(base + k, SUBLANES, stride=KEY_PITCH)

    for k in range(N_KEYS):
        s1_v[k] = s1_t[key_rows(k), :]
        s2_v[k] = s2_t[key_rows(k), :]
    v1 = _top16_of([s1_v[k] for k in range(N_KEYS)])
    v2 = _top16_of([s2_v[k] for k in range(N_KEYS)])

    pairs = [(a, b) for a in range(PEER_TOPK) for b in range(PEER_TOPK // (a + 1))]
    cand = {ab: v1[ab[0]] + v2[ab[1]] for ab in pairs}
    first_row = [cand[(0, b)] for b in range(PEER_TOPK)]
    rest = [cand[ab] for ab in pairs if ab[0] > 0]
    rest += [jnp.full_like(v1[0], NEG_INF)] * (-len(rest) % PEER_TOPK)
    tau = _merge_top16(first_row, _top16_of(rest))[PEER_TOPK - 1]

    cmax = cand[(0, 0)]
    z = jnp.zeros_like(tau)
    th_rank = [jnp.full_like(tau, POS_INF) for _ in range(PEER_TOPK)]
    for a, b in pairs:
        sel = cand[(a, b)] >= tau
        z = z + jnp.where(sel, jnp.exp(cand[(a, b)] - cmax), 0.0)
        th_rank[a] = jnp.where(sel, v2[b], th_rank[a])
    c_scale = SQRT_HALF / z

    def per_key(k, carry):
        s1k = s1_v[k]
        th = jnp.full_like(s1k, POS_INF)
        for a in reversed(range(PEER_TOPK)):
            th = jnp.where(s1k >= v1[a], th_rank[a], th)
        th_v[grp, k] = th
        c1_v[grp, k] = jnp.exp(s1k - v1[0]) * c_scale
        e2_t[key_rows(k), :] = jnp.exp(s2_v[k] - v2[0])
        return carry

    lax.fori_loop(0, N_KEYS, per_key, 0, unroll=8)


def _peer_kernel(x_ref, g_ref, gf_ref, kq_ref, u_ref, vt_ref, o_ref,
                 xn_scr, xs_scr, s1_t, s2_t, e2_t, s1_v, s2_v, th_v, c1_v, h_scr, w_scr, acc_scr,
                 *, tm, eb, final_norm):
    e = pl.program_id(1)
    n_e = pl.num_programs(1)
    n_tc = tm // LANES
    keys_per_block = eb // N_KEYS
    n_groups = PEER_HEADS * n_tc // SUBLANES

    def tile_row(h, tc):
        return (h * n_tc + tc) * KEY_PITCH

    @pl.when(e == 0)
    def prepare():
        xn_t = _rms(x_ref[...], g_ref[...]).T
        xn_scr[...] = xn_t.astype(BF16)
        xs_scr[...] = (xn_t * SQRT_HALF).astype(BF16)
        for h in range(PEER_HEADS):
            s12 = _dot(kq_ref[h * 2 * N_KEYS:(h + 1) * 2 * N_KEYS, :], xn_scr[...])
            for tc in range(n_tc):
                rows = pl.ds(tile_row(h, tc), N_KEYS)
                s1_t[rows, :] = s12[:N_KEYS, tc * LANES:(tc + 1) * LANES]
                s2_t[rows, :] = s12[N_KEYS:, tc * LANES:(tc + 1) * LANES]

        def thresholds(grp, carry):
            _peer_thresholds(s1_t, s2_t, e2_t, s1_v, s2_v, th_v, c1_v, grp)
            return carry

        lax.fori_loop(0, n_groups, thresholds, 0)
        acc_scr[...] = jnp.zeros_like(acc_scr)

    hs = _dot(u_ref[...], xs_scr[...])
    h_scr[...] = hs * (1.0 + lax.erf(hs))

    def key_rows(p):
        for tc in range(n_tc):
            lanes = slice(tc * LANES, (tc + 1) * LANES)
            gates = [jnp.zeros((N_KEYS, LANES), F32) for _ in range(KEYS_PER_TRIP)]
            for h in range(PEER_HEADS):
                grp, j = divmod(h * n_tc + tc, SUBLANES)
                tile = pl.ds(tile_row(h, tc), N_KEYS)
                s2 = s2_t[tile, :]
                e2 = e2_t[tile, :]
                for k in range(KEYS_PER_TRIP):
                    i1 = e * keys_per_block + p * KEYS_PER_TRIP + k
                    th = th_v[grp, i1, j:j + 1, :]
                    c1 = c1_v[grp, i1, j:j + 1, :]
                    gates[k] = gates[k] + jnp.where(s2 >= th, c1 * e2, 0.0)
            for k in range(KEYS_PER_TRIP):
                rows = slice((p * KEYS_PER_TRIP + k) * N_KEYS, (p * KEYS_PER_TRIP + k + 1) * N_KEYS)
                w_scr[rows, lanes] = (gates[k] * h_scr[rows, lanes]).astype(BF16)

    for p in range(keys_per_block // KEYS_PER_TRIP):
        key_rows(p)
    acc_scr[...] += _dot(vt_ref[...], w_scr[...])

    @pl.when(e == n_e - 1)
    def _():
        y = x_ref[...] + acc_scr[...].T
        if final_norm:
            y = _rms(y, gf_ref[...])
        o_ref[...] = y


def _peer(x2d, g, g_final, kq_bf, u_bf, v_t_bf, *, layer, final_norm,
          tm=PEER_TOKEN_TILE, eb=PEER_EXPERT_BLOCK):
    n, d = x2d.shape
    n_tc = tm // LANES
    n_e = N_EXPERTS // eb
    const2 = lambda i, e: (0, 0)
    kern = functools.partial(_peer_kernel, tm=tm, eb=eb, final_norm=final_norm)
    n_tiles = PEER_HEADS * n_tc
    assert n_tiles % SUBLANES == 0
    tiles = (n_tiles * KEY_PITCH, LANES)
    per_key = (N_KEYS, SUBLANES, LANES)
    per_key_groups = (n_tiles // SUBLANES,) + per_key
    return pl.pallas_call(
        kern,
        grid=(n // tm, n_e),
        in_specs=[
            pl.BlockSpec((tm, d), lambda i, e: (i, 0)),
            pl.BlockSpec((1, d), const2),
            pl.BlockSpec((1, d), const2),
            pl.BlockSpec((None, PEER_HEADS * 2 * N_KEYS, d), lambda i, e: (layer, 0, 0)),
            pl.BlockSpec((None, eb, d), lambda i, e: (layer, e, 0)),
            pl.BlockSpec((None, d, eb), lambda i, e: (layer, 0, e)),
        ],
        out_specs=pl.BlockSpec((tm, d), lambda i, e: (i, 0)),
        out_shape=jax.ShapeDtypeStruct((n, d), F32),
        scratch_shapes=[
            pltpu.VMEM((d, tm), BF16),
            pltpu.VMEM((d, tm), BF16),
            pltpu.VMEM(tiles, F32),
            pltpu.VMEM(tiles, F32),
            pltpu.VMEM(tiles, F32),
            pltpu.VMEM(per_key, F32),
            pltpu.VMEM(per_key, F32),
            pltpu.VMEM(per_key_groups, F32),
            pltpu.VMEM(per_key_groups, F32),
            pltpu.VMEM((eb, tm), F32),
            pltpu.VMEM((eb, tm), BF16),
            pltpu.VMEM((d, tm), F32),
        ],
        compiler_params=_params(("parallel", "arbitrary")),
        name="peer",
    )(x2d, g, g_final, kq_bf, u_bf, v_t_bf)


def _peer_score_weights_kernel(wq_ref, k1_ref, k2_ref, o_ref):
    o_ref[:N_KEYS, :] = _dot(k1_ref[...], wq_ref[:PEER_HALF, :]).astype(BF16)
    o_ref[N_KEYS:, :] = _dot(k2_ref[...], wq_ref[PEER_HALF:, :]).astype(BF16)


def _peer_score_weights(wq_t_bf, k1_bf, k2_bf):
    n_layers, rows, d = wq_t_bf.shape
    key_spec = pl.BlockSpec((None, N_KEYS, PEER_HALF), lambda l, h: (l, 0, 0))
    return pl.pallas_call(
        _peer_score_weights_kernel,
        grid=(n_layers, PEER_HEADS),
        in_specs=[pl.BlockSpec((None, PEER_QUERY, d), lambda l, h: (l, h, 0)), key_spec, key_spec],
        out_specs=pl.BlockSpec((None, 2 * N_KEYS, d), lambda l, h: (l, h, 0)),
        out_shape=jax.ShapeDtypeStruct((n_layers, PEER_HEADS * 2 * N_KEYS, d), BF16),
        compiler_params=_params(("parallel", "parallel")),
        name="peer_score_weights",
    )(wq_t_bf, k1_bf, k2_bf)


def _peer_tables_kernel(u_ref, v_ref, ub_ref, vt_ref):
    ub_ref[...] = u_ref[...].astype(BF16)
    vt_ref[0] = v_ref[0].T.astype(BF16)


def _peer_tables(peer_u, peer_v, *, tile=1024):
    n_layers, n_exp, d = peer_u.shape
    return pl.pallas_call(
        _peer_tables_kernel,
        grid=(n_layers, n_exp // tile),
        in_specs=[
            pl.BlockSpec((1, tile, d), lambda l, i: (l, i, 0)),
            pl.BlockSpec((1, tile, d), lambda l, i: (l, i, 0)),
        ],
        out_specs=[
            pl.BlockSpec((1, tile, d), lambda l, i: (l, i, 0)),
            pl.BlockSpec((1, d, tile), lambda l, i: (l, 0, i)),
        ],
        out_shape=[
            jax.ShapeDtypeStruct((n_layers, n_exp, d), BF16),
            jax.ShapeDtypeStruct((n_layers, d, n_exp), BF16),
        ],
        compiler_params=_params(("parallel", "parallel")),
        name="peer_tables",
    )(peer_u, peer_v)


def _trunk(x, h0_re, h0_im, w, *, scan_steps):
    bsz, t, d = x.shape
    n = bsz * t
    x2d = x.reshape(n, d)

    if bsz == SUBLANES and t % S5_ROW_TILE == 0:
        u_tm = _norm_matmul(x, w["norm_mix_g"][0], w["perm"], w["ssm_w_in"], tile=S5_ROW_TILE)
        g_tm, hr, hi = _s5_scan(u_tm, h0_re, h0_im, w["disc"], batch=bsz, steps=scan_steps)
        x2d = _glu_out(g_tm, x, w["perm"].T, w["ssm_w_glu"], w["ssm_b_glu"], w["ssm_w_out"],
                       tile=S5_ROW_TILE).reshape(n, d)
    else:
        u_bm = _norm_matmul(x2d[None], w["norm_mix_g"][0], w["perm"], w["ssm_w_in"], tile=n)
        u_tm = u_bm.reshape(bsz, t, d).transpose(1, 0, 2).reshape(n, d)
        g_tm, hr, hi = _s5_scan(u_tm, h0_re, h0_im, w["disc"], batch=bsz, steps=scan_steps)
        g_bm = g_tm.reshape(t, bsz, d).transpose(1, 0, 2).reshape(n, d)
        x2d = _glu_out(g_bm, x2d[None], w["perm"].T, w["ssm_w_glu"], w["ssm_b_glu"], w["ssm_w_out"],
                       tile=n).reshape(n, d)
    x2d = _peer(x2d, w["norm_ffn_g"][0], w["norm_final_g"], *w["peer_weights"], layer=0,
                final_norm=False)

    if t % CHUNK == 0:
        mix, mix_bias = w["mix_full"]
    else:
        mix, mix_bias = w["mix_short"]
    x2d, v = _gmlp(x2d, w["norm_mix_g"][1], w["gmlp_w_in"], w["gmlp_b_in"], w["gmlp_ln_g"], w["gmlp_ln_b"],
                   mix, mix_bias, w["gmlp_w_out"], tile=GMLP_ROW_TILE)
    y2d = _peer(x2d, w["norm_ffn_g"][1], w["norm_final_g"], *w["peer_weights"], layer=1,
                final_norm=True)
    return y2d.reshape(bsz, t, d), hr, hi, v


def kernel(x_prompt, x_sample, state_ssm_re, state_ssm_im, norm_mix_g, norm_ffn_g, norm_final_g, ssm_w_in, ssm_a_re, ssm_a_im, ssm_log_dt, ssm_b_re, ssm_b_im, ssm_c_re, ssm_c_im, ssm_d, ssm_w_glu, ssm_b_glu, ssm_w_out, gmlp_w_in, gmlp_b_in, gmlp_ln_g, gmlp_ln_b, gmlp_w_s, gmlp_b_s, gmlp_w_out, peer_w_q, peer_k1, peer_k2, peer_u, peer_v):
    bp, tp, d = x_prompt.shape
    bs, ts, _ = x_sample.shape
    assert tp % CHUNK == 0 and CHUNK % ts == 0 and d == D_MODEL

    tril = jnp.tril(jnp.ones((CHUNK, CHUNK), F32))
    ws = gmlp_w_s[0] * tril
    bias_full = jnp.broadcast_to(gmlp_b_s[0][:, :, None], (GMLP_HEADS, CHUNK, GMLP_HEAD_DIM))
    reps = CHUNK // ts
    ws_short = jnp.einsum("ab,hij->haibj", jnp.eye(reps, dtype=F32), ws[:, :ts, :ts]).reshape(GMLP_HEADS, CHUNK, CHUNK)
    bias_short = jnp.broadcast_to(jnp.tile(gmlp_b_s[0][:, :ts], (1, reps))[:, :, None],
                                  (GMLP_HEADS, CHUNK, GMLP_HEAD_DIM))

    w = {
        "norm_mix_g": norm_mix_g[:, None, :],
        "norm_ffn_g": norm_ffn_g[:, None, :],
        "norm_final_g": norm_final_g[None, :],
        "perm": _time_major_perm(),
        "ssm_w_in": ssm_w_in[0].astype(BF16),
        "disc": _s5_discretize(ssm_a_re[0], ssm_a_im[0], ssm_log_dt[0], ssm_b_re[0], ssm_b_im[0],
                               ssm_c_re[0], ssm_c_im[0], ssm_d[0]),
        "ssm_w_glu": ssm_w_glu[0].astype(BF16),
        "ssm_b_glu": ssm_b_glu[0][None, :],
        "ssm_w_out": ssm_w_out[0].astype(BF16),
        "gmlp_w_in": gmlp_w_in[0].astype(BF16),
        "gmlp_b_in": gmlp_b_in[0][None, :],
        "gmlp_ln_g": gmlp_ln_g[0][None, :],
        "gmlp_ln_b": gmlp_ln_b[0][None, :],
        "mix_full": (ws.astype(BF16), bias_full),
        "mix_short": (ws_short.astype(BF16), bias_short),
        "gmlp_w_out": gmlp_w_out[0].astype(BF16),
        "peer_weights": (_peer_score_weights(jnp.swapaxes(peer_w_q, 1, 2).astype(BF16), peer_k1.astype(BF16),
                                             peer_k2.astype(BF16)),) + tuple(_peer_tables(peer_u, peer_v)),
    }

    zeros = jnp.zeros((bp, SSM_COLS), F32)
    y_p, hr_p, hi_p, _ = _trunk(x_prompt, zeros, zeros, w, scan_steps=S5_SCAN_STEPS)
    y_s, hr_s, hi_s, v_s = _trunk(x_sample, state_ssm_re[0].reshape(bs, SSM_COLS),
                                  state_ssm_im[0].reshape(bs, SSM_COLS), w, scan_steps=ts)
    st = lambda a, b: a.reshape(1, b, SSM_GROUPS, SSM_STATE)
    return (y_p, y_s, st(hr_p, bp), st(hi_p, bp), st(hr_s, bs), st(hi_s, bs),
            v_s.reshape(1, bs, ts, GMLP_HALF))
```

```python
import functools
import math

import jax
import jax.numpy as jnp
from jax import lax
from jax.experimental import pallas as pl
from jax.experimental.pallas import tpu as pltpu

F32 = jnp.float32
BF16 = jnp.bfloat16

EPS = 1e-6
D_MODEL = 1024
SSM_GROUP = 16
SSM_GROUPS = 64
SSM_STATE = 64
SSM_COLS = SSM_GROUPS * SSM_STATE
GMLP_HALF = 2 * D_MODEL
GMLP_HEADS = 8
GMLP_HEAD_DIM = GMLP_HALF // GMLP_HEADS
CHUNK = 128
PEER_HEADS = 8
N_KEYS = 128
N_EXPERTS = N_KEYS * N_KEYS
PEER_QUERY = 256
PEER_HALF = 128
PEER_TOPK = 16

LANES = 128
SUBLANES = 8
MXU_DIM = 256
VMEM_LIMIT = 56 * 1024 * 1024

NEG_INF = float("-inf")
POS_INF = float("inf")
SQRT_HALF = math.sqrt(0.5)


def _dot(a, b):
    return jnp.dot(a, b, preferred_element_type=F32)


def _rms(x, g):
    ms = jnp.mean(x * x, axis=-1, keepdims=True)
    return x * lax.rsqrt(ms + EPS) * g


def _gelu(x):
    return 0.5 * x * (1.0 + lax.erf(x * SQRT_HALF))


def _params(sem):
    return pltpu.CompilerParams(dimension_semantics=sem, vmem_limit_bytes=VMEM_LIMIT)


S5_ROW_TILE = 128
S5_SCAN_STEPS = 64
GMLP_ROW_TILE = 512
PEER_TOKEN_TILE = 512
PEER_EXPERT_BLOCK = 2048


PERM_STEPS = MXU_DIM // SUBLANES


def _time_major_perm():
    r = jnp.arange(MXU_DIM)
    src = (r % SUBLANES) * PERM_STEPS + r // SUBLANES
    return (src[:, None] == r[None, :]).astype(BF16)


def _norm_matmul_kernel(x_ref, g_ref, p_ref, w_ref, o_ref, *, batch):
    tile = x_ref.shape[1]
    xn = [_rms(x_ref[b], g_ref[...]).astype(BF16) for b in range(batch)]
    if batch == 1:
        xp = xn[0]
    else:
        groups = []
        for tg in range(tile // PERM_STEPS):
            ts = slice(tg * PERM_STEPS, (tg + 1) * PERM_STEPS)
            rows_bt = jnp.concatenate([xn[b][ts] for b in range(batch)], axis=0)
            groups.append(_dot(p_ref[...], rows_bt).astype(BF16))
        xp = jnp.concatenate(groups, axis=0)
    o_ref[...] = _dot(xp, w_ref[...]).astype(o_ref.dtype)


def _norm_matmul(x3d, g, perm, w_bf, *, tile):
    bsz, t, d = x3d.shape
    n = w_bf.shape[1]
    assert bsz == 1 or (bsz == SUBLANES and tile % PERM_STEPS == 0)
    return pl.pallas_call(
        functools.partial(_norm_matmul_kernel, batch=bsz),
        grid=(t // tile,),
        in_specs=[
            pl.BlockSpec((bsz, tile, d), lambda i: (0, i, 0)),
            pl.BlockSpec((1, d), lambda i: (0, 0)),
            pl.BlockSpec((MXU_DIM, MXU_DIM), lambda i: (0, 0)),
            pl.BlockSpec((d, n), lambda i: (0, 0)),
        ],
        out_specs=pl.BlockSpec((tile * bsz, n), lambda i: (i, 0)),
        out_shape=jax.ShapeDtypeStruct((t * bsz, n), BF16),
        compiler_params=_params(("parallel",)),
        name="s5_in_proj",
    )(x3d, g, perm, w_bf)
SCAN_COLS = 1024
N_SCAN_BLOCKS = SSM_COLS // SCAN_COLS
N_DIAG_BLOCKS = D_MODEL // MXU_DIM
DIAG_COLS = SSM_COLS // N_DIAG_BLOCKS
SCAN_PER_DIAG = DIAG_COLS // SCAN_COLS


def _s5_scan_kernel(u_ref, h0r_ref, h0i_ref, ar_ref, ai_ref, bbr_ref, bbi_ref, ccr_ref, cci_ref, d_ref,
                    g_ref, hr_ref, hi_ref, bur, bui, st_r, st_i, *, steps, batch):
    c = pl.program_id(0)
    n_sub = batch // SUBLANES

    @pl.when(c == 0)
    def _():
        for cb in range(N_SCAN_BLOCKS):
            st_r[cb] = h0r_ref[:, cb * SCAN_COLS:(cb + 1) * SCAN_COLS]
            st_i[cb] = h0i_ref[:, cb * SCAN_COLS:(cb + 1) * SCAN_COLS]

    ub = u_ref[...]
    u = ub.astype(F32)
    for kb in range(N_DIAG_BLOCKS):
        lhs = ub[:, kb * MXU_DIM:(kb + 1) * MXU_DIM]
        pr = _dot(lhs, bbr_ref[kb])
        pi = _dot(lhs, bbi_ref[kb])
        for j in range(SCAN_PER_DIAG):
            bur[kb * SCAN_PER_DIAG + j] = pr[:, j * SCAN_COLS:(j + 1) * SCAN_COLS]
            bui[kb * SCAN_PER_DIAG + j] = pi[:, j * SCAN_COLS:(j + 1) * SCAN_COLS]

    def scan_block(idx, carry):
        cb = idx // n_sub
        s = idx % n_sub
        ar = jnp.broadcast_to(ar_ref[cb], (SUBLANES, SCAN_COLS))
        ai = jnp.broadcast_to(ai_ref[cb], (SUBLANES, SCAN_COLS))
        row0 = pl.multiple_of(s * SUBLANES, SUBLANES)
        h_r = st_r[cb, pl.ds(row0, SUBLANES), :]
        h_i = st_i[cb, pl.ds(row0, SUBLANES), :]

        def step(t, h):
            hr, hi = h
            r = pl.multiple_of(t * batch + row0, SUBLANES)
            nr = ar * hr - ai * hi + bur[cb, pl.ds(r, SUBLANES), :]
            ni = ar * hi + ai * hr + bui[cb, pl.ds(r, SUBLANES), :]
            bur[cb, pl.ds(r, SUBLANES), :] = nr
            bui[cb, pl.ds(r, SUBLANES), :] = ni
            return nr, ni

        h_r, h_i = lax.fori_loop(0, steps, step, (h_r, h_i), unroll=min(steps, 8))
        st_r[cb, pl.ds(row0, SUBLANES), :] = h_r
        st_i[cb, pl.ds(row0, SUBLANES), :] = h_i
        return carry

    lax.fori_loop(0, N_SCAN_BLOCKS * n_sub, scan_block, 0)

    for cb in range(N_SCAN_BLOCKS):
        hr_ref[:, cb * SCAN_COLS:(cb + 1) * SCAN_COLS] = st_r[cb]
        hi_ref[:, cb * SCAN_COLS:(cb + 1) * SCAN_COLS] = st_i[cb]

    for nb in range(N_DIAG_BLOCKS):
        acc = None
        for j in range(SCAN_PER_DIAG):
            cb = nb * SCAN_PER_DIAG + j
            part = (_dot(bur[cb].astype(BF16), ccr_ref[nb, j * SCAN_COLS:(j + 1) * SCAN_COLS, :])
                    + _dot(bui[cb].astype(BF16), cci_ref[nb, j * SCAN_COLS:(j + 1) * SCAN_COLS, :]))
            acc = part if acc is None else acc + part
        cols = slice(nb * MXU_DIM, (nb + 1) * MXU_DIM)
        y = acc + d_ref[:, cols] * u[:, cols]
        g_ref[:, cols] = _gelu(y).astype(g_ref.dtype)


def _s5_scan(u_tm, h0_re, h0_im, disc, *, batch, steps):
    rows, d = u_tm.shape
    t = rows // batch
    n_chunks = t // steps
    blk = steps * batch
    ar, ai, bbr, bbi, ccr, cci, dsk = disc
    const2 = lambda c: (0, 0)
    const3 = lambda c: (0, 0, 0)
    kern = functools.partial(_s5_scan_kernel, steps=steps, batch=batch)
    return pl.pallas_call(
        kern,
        grid=(n_chunks,),
        in_specs=[
            pl.BlockSpec((blk, d), lambda c: (c, 0)),
            pl.BlockSpec((batch, SSM_COLS), const2),
            pl.BlockSpec((batch, SSM_COLS), const2),
            pl.BlockSpec((N_SCAN_BLOCKS, 1, SCAN_COLS), const3),
            pl.BlockSpec((N_SCAN_BLOCKS, 1, SCAN_COLS), const3),
            pl.BlockSpec((N_DIAG_BLOCKS, MXU_DIM, DIAG_COLS), const3),
            pl.BlockSpec((N_DIAG_BLOCKS, MXU_DIM, DIAG_COLS), const3),
            pl.BlockSpec((N_DIAG_BLOCKS, DIAG_COLS, MXU_DIM), const3),
            pl.BlockSpec((N_DIAG_BLOCKS, DIAG_COLS, MXU_DIM), const3),
            pl.BlockSpec((1, d), const2),
        ],
        out_specs=[
            pl.BlockSpec((blk, d), lambda c: (c, 0)),
            pl.BlockSpec((batch, SSM_COLS), const2),
            pl.BlockSpec((batch, SSM_COLS), const2),
        ],
        out_shape=[
            jax.ShapeDtypeStruct((rows, d), BF16),
            jax.ShapeDtypeStruct((batch, SSM_COLS), F32),
            jax.ShapeDtypeStruct((batch, SSM_COLS), F32),
        ],
        scratch_shapes=[
            pltpu.VMEM((N_SCAN_BLOCKS, blk, SCAN_COLS), F32),
            pltpu.VMEM((N_SCAN_BLOCKS, blk, SCAN_COLS), F32),
            pltpu.VMEM((N_SCAN_BLOCKS, batch, SCAN_COLS), F32),
            pltpu.VMEM((N_SCAN_BLOCKS, batch, SCAN_COLS), F32),
        ],
        compiler_params=_params(("arbitrary",)),
        name="s5_scan",
    )(u_tm, h0_re, h0_im, ar, ai, bbr, bbi, ccr, cci, dsk)


def _glu_out_kernel(g_ref, x_ref, pt_ref, wg_ref, bg_ref, wo_ref, o_ref, *, batch):
    tile = x_ref.shape[1]
    g = g_ref[...]
    z = _dot(g, wg_ref[...]) + bg_ref[...]
    o = (g.astype(F32) * jax.nn.sigmoid(z)).astype(BF16)
    if batch == 1:
        o_ref[0] = x_ref[0] + _dot(o, wo_ref[...])
    else:
        n_groups = tile // PERM_STEPS
        groups = [_dot(pt_ref[...], o[tg * MXU_DIM:(tg + 1) * MXU_DIM]).astype(BF16) for tg in range(n_groups)]
        mix = _dot(jnp.concatenate(groups, axis=0), wo_ref[...])
        for tg in range(n_groups):
            for b in range(batch):
                ts = slice(tg * PERM_STEPS, (tg + 1) * PERM_STEPS)
                r0 = tg * MXU_DIM + b * PERM_STEPS
                o_ref[b, ts, :] = x_ref[b, ts, :] + mix[r0:r0 + PERM_STEPS]


def _glu_out(g_tm, x3d, perm_t, w_glu_bf, b_glu, w_out_bf, *, tile):
    bsz, t, d = x3d.shape
    assert bsz == 1 or (bsz == SUBLANES and tile % PERM_STEPS == 0)
    const2 = lambda i: (0, 0)
    return pl.pallas_call(
        functools.partial(_glu_out_kernel, batch=bsz),
        grid=(t // tile,),
        in_specs=[
            pl.BlockSpec((tile * bsz, d), lambda i: (i, 0)),
            pl.BlockSpec((bsz, tile, d), lambda i: (0, i, 0)),
            pl.BlockSpec((MXU_DIM, MXU_DIM), const2),
            pl.BlockSpec((d, d), const2),
            pl.BlockSpec((1, d), const2),
            pl.BlockSpec((d, d), const2),
        ],
        out_specs=pl.BlockSpec((bsz, tile, d), lambda i: (0, i, 0)),
        out_shape=jax.ShapeDtypeStruct((bsz, t, d), F32),
        compiler_params=_params(("parallel",)),
        name="s5_glu_out",
    )(g_tm, x3d, perm_t, w_glu_bf, b_glu, w_out_bf)


def _s5_discretize(a_re, a_im, log_dt, b_re, b_im, c_re, c_im, d_skip):
    dt = jnp.exp(log_dt)[:, None]
    mag = jnp.exp(a_re * dt)
    abar_re = mag * jnp.cos(a_im * dt)
    abar_im = mag * jnp.sin(a_im * dt)
    num_re = abar_re - 1.0
    num_im = abar_im
    den = a_re * a_re + a_im * a_im
    f_re = (num_re * a_re + num_im * a_im) / den
    f_im = (num_im * a_re - num_re * a_im) / den
    bb_re = f_re[..., None] * b_re - f_im[..., None] * b_im
    bb_im = f_re[..., None] * b_im + f_im[..., None] * b_re
    gpb = MXU_DIM // SSM_GROUP
    eye = jnp.eye(gpb, dtype=F32)

    def in_blocks(bb):
        bt = jnp.transpose(bb, (0, 2, 1)).reshape(N_DIAG_BLOCKS, gpb, SSM_GROUP, SSM_STATE)
        full = bt[:, :, :, None, :] * eye[None, :, None, :, None]
        return full.reshape(N_DIAG_BLOCKS, MXU_DIM, DIAG_COLS).astype(BF16)

    def out_blocks(cc):
        ct = jnp.transpose(cc, (0, 2, 1)).reshape(N_DIAG_BLOCKS, gpb, SSM_STATE, SSM_GROUP)
        full = ct[:, :, :, None, :] * eye[None, :, None, :, None]
        return full.reshape(N_DIAG_BLOCKS, DIAG_COLS, MXU_DIM).astype(BF16)

    ar = abar_re.reshape(N_SCAN_BLOCKS, 1, SCAN_COLS)
    ai = abar_im.reshape(N_SCAN_BLOCKS, 1, SCAN_COLS)
    return (ar, ai, in_blocks(bb_re), in_blocks(bb_im), out_blocks(c_re), out_blocks(-c_im),
            d_skip.reshape(1, D_MODEL))


def _gmlp_kernel(x_ref, g_ref, win_ref, bin_ref, lng_ref, lnb_ref, mix_ref, mixb_ref, wout_ref,
                 o_ref, v_ref, s_scr, *, rows):
    x = x_ref[...]
    hn = _rms(x, g_ref[...]).astype(BF16)
    z = _gelu(_dot(hn, win_ref[...]) + bin_ref[...])
    u = z[:, :GMLP_HALF]
    v = z[:, GMLP_HALF:]
    mu = jnp.mean(v, axis=-1, keepdims=True)
    vc = v - mu
    vn = vc * lax.rsqrt(jnp.mean(vc * vc, axis=-1, keepdims=True) + EPS) * lng_ref[...] + lnb_ref[...]
    v_ref[...] = vn
    vb = vn.astype(BF16)
    for r in range(rows // CHUNK):
        rs = slice(r * CHUNK, (r + 1) * CHUNK)
        for h in range(GMLP_HEADS):
            cs = slice(h * GMLP_HEAD_DIM, (h + 1) * GMLP_HEAD_DIM)
            mixed = _dot(mix_ref[h], vb[rs, cs]) + mixb_ref[h]
            s_scr[rs, cs] = (u[rs, cs] * mixed).astype(BF16)
    o_ref[...] = x + _dot(s_scr[...], wout_ref[...])


def _gmlp(x2d, g, w_in_bf, b_in, ln_g, ln_b, mix_bf, mix_bias, w_out_bf, *, tile):
    rows, d = x2d.shape
    const2 = lambda i: (0, 0)
    const3 = lambda i: (0, 0, 0)
    kern = functools.partial(_gmlp_kernel, rows=tile)
    return pl.pallas_call(
        kern,
        grid=(rows // tile,),
        in_specs=[
            pl.BlockSpec((tile, d), lambda i: (i, 0)),
            pl.BlockSpec((1, d), const2),
            pl.BlockSpec((d, 2 * GMLP_HALF), const2),
            pl.BlockSpec((1, 2 * GMLP_HALF), const2),
            pl.BlockSpec((1, GMLP_HALF), const2),
            pl.BlockSpec((1, GMLP_HALF), const2),
            pl.BlockSpec((GMLP_HEADS, CHUNK, CHUNK), const3),
            pl.BlockSpec((GMLP_HEADS, CHUNK, GMLP_HEAD_DIM), const3),
            pl.BlockSpec((GMLP_HALF, d), const2),
        ],
        out_specs=[
            pl.BlockSpec((tile, d), lambda i: (i, 0)),
            pl.BlockSpec((tile, GMLP_HALF), lambda i: (i, 0)),
        ],
        out_shape=[
            jax.ShapeDtypeStruct((rows, d), F32),
            jax.ShapeDtypeStruct((rows, GMLP_HALF), F32),
        ],
        scratch_shapes=[pltpu.VMEM((tile, GMLP_HALF), BF16)],
        compiler_params=_params(("parallel",)),
        name="gmlp",
    )(x2d, g, w_in_bf, b_in, ln_g, ln_b, mix_bf, mix_bias, w_out_bf)


KEY_PITCH = N_KEYS + SUBLANES
KEYS_PER_TRIP = 2


def _sorting_network(n):
    pairs = []
    p = 1
    while p < n:
        k = p
        while k >= 1:
            for j in range(k % p, n - k, 2 * k):
                for i in range(min(k, n - j - k)):
                    if (i + j) // (2 * p) == (i + j + k) // (2 * p):
                        pairs.append((i + j, i + j + k))
            k //= 2
        p *= 2
    return pairs


SORT16 = _sorting_network(PEER_TOPK)


def _compare_exchange(v, i, j):
    v[i], v[j] = jnp.maximum(v[i], v[j]), jnp.minimum(v[i], v[j])


def _sort16_desc(v):
    v = list(v)
    for i, j in SORT16:
        _compare_exchange(v, i, j)
    return v


def _merge_top16(a, b):
    c = [jnp.maximum(a[i], b[PEER_TOPK - 1 - i]) for i in range(PEER_TOPK)]
    d = PEER_TOPK // 2
    while d >= 1:
        for i in range(PEER_TOPK):
            if i & d == 0:
                _compare_exchange(c, i, i + d)
        d //= 2
    return c


def _top16_of(values):
    lists = [_sort16_desc(values[i:i + PEER_TOPK]) for i in range(0, len(values), PEER_TOPK)]
    while len(lists) > 1:
        merged = [_merge_top16(lists[i], lists[i + 1]) for i in range(0, len(lists) - 1, 2)]
        if len(lists) % 2:
            merged.append(lists[-1])
        lists = merged
    return lists[0]


def _peer_thresholds(s1_t, s2_t, e2_t, s1_v, s2_v, th_v, c1_v, grp):
    base = grp * (SUBLANES * KEY_PITCH)

    def key_rows(k):
        return pl.ds(base + k, SUBLANES, stride=KEY_PITCH)

    for k in range(N_KEYS):
        s1_v[k] = s1_t[key_rows(k), :]
        s2_v[k] = s2_t[key_rows(k), :]
    v1 = _top16_of([s1_v[k] for k in range(N_KEYS)])
    v2 = _top16_of([s2_v[k] for k in range(N_KEYS)])

    pairs = [(a, b) for a in range(PEER_TOPK) for b in range(PEER_TOPK // (a + 1))]
    cand = {ab: v1[ab[0]] + v2[ab[1]] for ab in pairs}
    first_row = [cand[(0, b)] for b in range(PEER_TOPK)]
    rest = [cand[ab] for ab in pairs if ab[0] > 0]
    rest += [jnp.full_like(v1[0], NEG_INF)] * (-len(rest) % PEER_TOPK)
    tau = _merge_top16(first_row, _top16_of(rest))[PEER_TOPK - 1]

    cmax = cand[(0, 0)]
    z = jnp.zeros_like(tau)
    th_rank = [jnp.full_like(tau, POS_INF) for _ in range(PEER_TOPK)]
    for a, b in pairs:
        sel = cand[(a, b)] >= tau
        z = z + jnp.where(sel, jnp.exp(cand[(a, b)] - cmax), 0.0)
        th_rank[a] = jnp.where(sel, v2[b], th_rank[a])
    c_scale = SQRT_HALF / z

    def per_key(k, carry):
        s1k = s1_v[k]
        th = jnp.full_like(s1k, POS_INF)
        for a in reversed(range(PEER_TOPK)):
            th = jnp.where(s1k >= v1[a], th_rank[a], th)
        th_v[grp, k] = th
        c1_v[grp, k] = jnp.exp(s1k - v1[0]) * c_scale
        e2_t[key_rows(k), :] = jnp.exp(s2_v[k] - v2[0])
        return carry

    lax.fori_loop(0, N_KEYS, per_key, 0, unroll=8)


def _peer_kernel(x_ref, g_ref, gf_ref, kq_ref, u_ref, vt_ref, o_ref,
                 xn_scr, xs_scr, s1_t, s2_t, e2_t, s1_v, s2_v, th_v, c1_v, h_scr, w_scr, acc_scr,
                 *, tm, eb, final_norm):
    e = pl.program_id(1)
    n_e = pl.num_programs(1)
    n_tc = tm // LANES
    keys_per_block = eb // N_KEYS
    n_groups = PEER_HEADS * n_tc // SUBLANES

    def tile_row(h, tc):
        return (h * n_tc + tc) * KEY_PITCH

    @pl.when(e == 0)
    def prepare():
        xn_t = _rms(x_ref[...], g_ref[...]).T
        xn_scr[...] = xn_t.astype(BF16)
        xs_scr[...] = (xn_t * SQRT_HALF).astype(BF16)
        for h in range(PEER_HEADS):
            s12 = _dot(kq_ref[h * 2 * N_KEYS:(h + 1) * 2 * N_KEYS, :], xn_scr[...])
            for tc in range(n_tc):
                rows = pl.ds(tile_row(h, tc), N_KEYS)
                s1_t[rows, :] = s12[:N_KEYS, tc * LANES:(tc + 1) * LANES]
                s2_t[rows, :] = s12[N_KEYS:, tc * LANES:(tc + 1) * LANES]

        def thresholds(grp, carry):
            _peer_thresholds(s1_t, s2_t, e2_t, s1_v, s2_v, th_v, c1_v, grp)
            return carry

        lax.fori_loop(0, n_groups, thresholds, 0)
        acc_scr[...] = jnp.zeros_like(acc_scr)

    hs = _dot(u_ref[...], xs_scr[...])
    h_scr[...] = hs * (1.0 + lax.erf(hs))

    def key_rows(p, carry):
        for tc in range(n_tc):
            lanes = slice(tc * LANES, (tc + 1) * LANES)
            gates = [jnp.zeros((N_KEYS, LANES), F32) for _ in range(KEYS_PER_TRIP)]
            for h in range(PEER_HEADS):
                grp, j = divmod(h * n_tc + tc, SUBLANES)
                tile = pl.ds(tile_row(h, tc), N_KEYS)
                s2 = s2_t[tile, :]
                e2 = e2_t[tile, :]
                for k in range(KEYS_PER_TRIP):
                    i1 = e * keys_per_block + p * KEYS_PER_TRIP + k
                    th = th_v[grp, i1, j:j + 1, :]
                    c1 = c1_v[grp, i1, j:j + 1, :]
                    gates[k] = gates[k] + jnp.where(s2 >= th, c1 * e2, 0.0)
            for k in range(KEYS_PER_TRIP):
                rows = pl.ds(pl.multiple_of((p * KEYS_PER_TRIP + k) * N_KEYS, N_KEYS), N_KEYS)
                w_scr[rows, lanes] = (gates[k] * h_scr[rows, lanes]).astype(BF16)
        return carry

    lax.fori_loop(0, keys_per_block // KEYS_PER_TRIP, key_rows, 0)
    acc_scr[...] += _dot(vt_ref[...], w_scr[...])

    @pl.when(e == n_e - 1)
    def _():
        y = x_ref[...] + acc_scr[...].T
        if final_norm:
            y = _rms(y, gf_ref[...])
        o_ref[...] = y


def _peer(x2d, g, g_final, kq_bf, u_bf, v_t_bf, *, layer, final_norm,
          tm=PEER_TOKEN_TILE, eb=PEER_EXPERT_BLOCK):
    n, d = x2d.shape
    n_tc = tm // LANES
    n_e = N_EXPERTS // eb
    const2 = lambda i, e: (0, 0)
    kern = functools.partial(_peer_kernel, tm=tm, eb=eb, final_norm=final_norm)
    n_tiles = PEER_HEADS * n_tc
    assert n_tiles % SUBLANES == 0
    tiles = (n_tiles * KEY_PITCH, LANES)
    per_key = (N_KEYS, SUBLANES, LANES)
    per_key_groups = (n_tiles // SUBLANES,) + per_key
    return pl.pallas_call(
        kern,
        grid=(n // tm, n_e),
        in_specs=[
            pl.BlockSpec((tm, d), lambda i, e: (i, 0)),
            pl.BlockSpec((1, d), const2),
            pl.BlockSpec((1, d), const2),
            pl.BlockSpec((None, PEER_HEADS * 2 * N_KEYS, d), lambda i, e: (layer, 0, 0)),
            pl.BlockSpec((None, eb, d), lambda i, e: (layer, e, 0)),
            pl.BlockSpec((None, d, eb), lambda i, e: (layer, 0, e)),
        ],
        out_specs=pl.BlockSpec((tm, d), lambda i, e: (i, 0)),
        out_shape=jax.ShapeDtypeStruct((n, d), F32),
        scratch_shapes=[
            pltpu.VMEM((d, tm), BF16),
            pltpu.VMEM((d, tm), BF16),
            pltpu.VMEM(tiles, F32),
            pltpu.VMEM(tiles, F32),
            pltpu.VMEM(tiles, F32),
            pltpu.VMEM(per_key, F32),
            pltpu.VMEM(per_key, F32),
            pltpu.VMEM(per_key_groups, F32),
            pltpu.VMEM(per_key_groups, F32),
            pltpu.VMEM((eb, tm), F32),
            pltpu.VMEM((eb, tm), BF16),
            pltpu.VMEM((d, tm), F32),
        ],
        compiler_params=_params(("parallel", "arbitrary")),
        name="peer",
    )(x2d, g, g_final, kq_bf, u_bf, v_t_bf)


def _peer_score_weights_kernel(wq_ref, k1_ref, k2_ref, o_ref):
    o_ref[:N_KEYS, :] = _dot(k1_ref[...], wq_ref[:PEER_HALF, :]).astype(BF16)
    o_ref[N_KEYS:, :] = _dot(k2_ref[...], wq_ref[PEER_HALF:, :]).astype(BF16)


def _peer_score_weights(wq_t_bf, k1_bf, k2_bf):
    n_layers, rows, d = wq_t_bf.shape
    key_spec = pl.BlockSpec((None, N_KEYS, PEER_HALF), lambda l, h: (l, 0, 0))
    return pl.pallas_call(
        _peer_score_weights_kernel,
        grid=(n_layers, PEER_HEADS),
        in_specs=[pl.BlockSpec((None, PEER_QUERY, d), lambda l, h: (l, h, 0)), key_spec, key_spec],
        out_specs=pl.BlockSpec((None, 2 * N_KEYS, d), lambda l, h: (l, h, 0)),
        out_shape=jax.ShapeDtypeStruct((n_layers, PEER_HEADS * 2 * N_KEYS, d), BF16),
        compiler_params=_params(("parallel", "parallel")),
        name="peer_score_weights",
    )(wq_t_bf, k1_bf, k2_bf)


def _peer_tables_kernel(u_ref, v_ref, ub_ref, vt_ref):
    ub_ref[...] = u_ref[...].astype(BF16)
    vt_ref[0] = v_ref[0].T.astype(BF16)


def _peer_tables(peer_u, peer_v, *, tile=1024):
    n_layers, n_exp, d = peer_u.shape
    return pl.pallas_call(
        _peer_tables_kernel,
        grid=(n_layers, n_exp // tile),
        in_specs=[
            pl.BlockSpec((1, tile, d), lambda l, i: (l, i, 0)),
            pl.BlockSpec((1, tile, d), lambda l, i: (l, i, 0)),
        ],
        out_specs=[
            pl.BlockSpec((1, tile, d), lambda l, i: (l, i, 0)),
            pl.BlockSpec((1, d, tile), lambda l, i: (l, 0, i)),
        ],
        out_shape=[
            jax.ShapeDtypeStruct((n_layers, n_exp, d), BF16),
            jax.ShapeDtypeStruct((n_layers, d, n_exp), BF16),
        ],
        compiler_params=_params(("parallel", "parallel")),
        name="peer_tables",
    )(peer_u, peer_v)


def _trunk(x, h0_re, h0_im, w, *, scan_steps):
    bsz, t, d = x.shape
    n = bsz * t
    x2d = x.reshape(n, d)

    if bsz == SUBLANES and t % S5_ROW_TILE == 0:
        u_tm = _norm_matmul(x, w["norm_mix_g"][0], w["perm"], w["ssm_w_in"], tile=S5_ROW_TILE)
        g_tm, hr, hi = _s5_scan(u_tm, h0_re, h0_im, w["disc"], batch=bsz, steps=scan_steps)
        x2d = _glu_out(g_tm, x, w["perm"].T, w["ssm_w_glu"], w["ssm_b_glu"], w["ssm_w_out"],
                       tile=S5_ROW_TILE).reshape(n, d)
    else:
        u_bm = _norm_matmul(x2d[None], w["norm_mix_g"][0], w["perm"], w["ssm_w_in"], tile=n)
        u_tm = u_bm.reshape(bsz, t, d).transpose(1, 0, 2).reshape(n, d)
        g_tm, hr, hi = _s5_scan(u_tm, h0_re, h0_im, w["disc"], batch=bsz, steps=scan_steps)
        g_bm = g_tm.reshape(t, bsz, d).transpose(1, 0, 2).reshape(n, d)
        x2d = _glu_out(g_bm, x2d[None], w["perm"].T, w["ssm_w_glu"], w["ssm_b_glu"], w["ssm_w_out"],
                       tile=n).reshape(n, d)
    x2d = _peer(x2d, w["norm_ffn_g"][0], w["norm_final_g"], *w["peer_weights"], layer=0,
                final_norm=False)

    if t % CHUNK == 0:
        mix, mix_bias = w["mix_full"]
    else:
        mix, mix_bias = w["mix_short"]
    x2d, v = _gmlp(x2d, w["norm_mix_g"][1], w["gmlp_w_in"], w["gmlp_b_in"], w["gmlp_ln_g"], w["gmlp_ln_b"],
                   mix, mix_bias, w["gmlp_w_out"], tile=GMLP_ROW_TILE)
    y2d = _peer(x2d, w["norm_ffn_g"][1], w["norm_final_g"], *w["peer_weights"], layer=1,
                final_norm=True)
    return y2d.reshape(bsz, t, d), hr, hi, v


def kernel(x_prompt, x_sample, state_ssm_re, state_ssm_im, norm_mix_g, norm_ffn_g, norm_final_g, ssm_w_in, ssm_a_re, ssm_a_im, ssm_log_dt, ssm_b_re, ssm_b_im, ssm_c_re, ssm_c_im, ssm_d, ssm_w_glu, ssm_b_glu, ssm_w_out, gmlp_w_in, gmlp_b_in, gmlp_ln_g, gmlp_ln_b, gmlp_w_s, gmlp_b_s, gmlp_w_out, peer_w_q, peer_k1, peer_k2, peer_u, peer_v):
    bp, tp, d = x_prompt.shape
    bs, ts, _ = x_sample.shape
    assert tp % CHUNK == 0 and CHUNK % ts == 0 and d == D_MODEL

    tril = jnp.tril(jnp.ones((CHUNK, CHUNK), F32))
    ws = gmlp_w_s[0] * tril
    bias_full = jnp.broadcast_to(gmlp_b_s[0][:, :, None], (GMLP_HEADS, CHUNK, GMLP_HEAD_DIM))
    reps = CHUNK // ts
    ws_short = jnp.einsum("ab,hij->haibj", jnp.eye(reps, dtype=F32), ws[:, :ts, :ts]).reshape(GMLP_HEADS, CHUNK, CHUNK)
    bias_short = jnp.broadcast_to(jnp.tile(gmlp_b_s[0][:, :ts], (1, reps))[:, :, None],
                                  (GMLP_HEADS, CHUNK, GMLP_HEAD_DIM))

    w = {
        "norm_mix_g": norm_mix_g[:, None, :],
        "norm_ffn_g": norm_ffn_g[:, None, :],
        "norm_final_g": norm_final_g[None, :],
        "perm": _time_major_perm(),
        "ssm_w_in": ssm_w_in[0].astype(BF16),
        "disc": _s5_discretize(ssm_a_re[0], ssm_a_im[0], ssm_log_dt[0], ssm_b_re[0], ssm_b_im[0],
                               ssm_c_re[0], ssm_c_im[0], ssm_d[0]),
        "ssm_w_glu": ssm_w_glu[0].astype(BF16),
        "ssm_b_glu": ssm_b_glu[0][None, :],
        "ssm_w_out": ssm_w_out[0].astype(BF16),
        "gmlp_w_in": gmlp_w_in[0].astype(BF16),
        "gmlp_b_in": gmlp_b_in[0][None, :],
        "gmlp_ln_g": gmlp_ln_g[0][None, :],
        "gmlp_ln_b": gmlp_ln_b[0][None, :],
        "mix_full": (ws.astype(BF16), bias_full),
        "mix_short": (ws_short.astype(BF16), bias_short),
        "gmlp_w_out": gmlp_w_out[0].astype(BF16),
        "peer_weights": (_peer_score_weights(jnp.swapaxes(peer_w_q, 1, 2).astype(BF16), peer_k1.astype(BF16),
                                             peer_k2.astype(BF16)),) + tuple(_peer_tables(peer_u, peer_v)),
    }

    zeros = jnp.zeros((bp, SSM_COLS), F32)
    y_p, hr_p, hi_p, _ = _trunk(x_prompt, zeros, zeros, w, scan_steps=S5_SCAN_STEPS)
    y_s, hr_s, hi_s, v_s = _trunk(x_sample, state_ssm_re[0].reshape(bs, SSM_COLS),
                                  state_ssm_im[0].reshape(bs, SSM_COLS), w, scan_steps=ts)
    st = lambda a, b: a.reshape(1, b, SSM_GROUPS, SSM_STATE)
    return (y_p, y_s, st(hr_p, bp), st(hi_p, bp), st(hr_s, bs), st(hi_s, bs),
            v_s.reshape(1, bs, ts, GMLP_HALF))
```
